```python
import math
import jax
import jax.numpy as jnp
from jax import lax
import numpy as np

D_MODEL = 2048
BATCH = 2
SEQ = 4096
DEPTH = 2
DEC_BATCH = 8
DEC_SEQ = 4
PAST_LEN = 16384
PAGE_SIZE = 128

MOBA_HEADS = 8
HEAD_DIM = 128
MOBA_BLOCK = 256
MOBA_TOPK = 3
MOBA_Q_BLOCK = 32
GLA_HEADS = 4
GLA_DK = 128
GLA_DV = 256
GLA_GATE_RANK = 16
GLA_TAU = 16.0
GLA_CHUNK = 32

RMS_EPS = 1e-6
NEG_INF = -1e30
MOBA_WIDTH = MOBA_HEADS * HEAD_DIM
GLA_K_WIDTH = GLA_HEADS * GLA_DK
GLA_V_WIDTH = GLA_HEADS * GLA_DV
IN_WIDTHS = (MOBA_WIDTH, MOBA_WIDTH, MOBA_WIDTH, MOBA_WIDTH,
             GLA_K_WIDTH, GLA_K_WIDTH, GLA_V_WIDTH, GLA_V_WIDTH, GLA_GATE_RANK,
             2 * D_MODEL)
N_IN = sum(IN_WIDTHS)

kernel_name = 'hybrid_moba_gla_gated_merge_step'


def rmsnorm(x, g):
    xf = x.astype(jnp.float32)
    r = lax.rsqrt(jnp.mean(xf * xf, axis=-1, keepdims=True) + RMS_EPS)
    return (xf * r * g.astype(jnp.float32)).astype(x.dtype)


def moba_blocks(k, v):
    B, H, L, Dh = k.shape
    nb = L // MOBA_BLOCK
    kb = k[:, :, : nb * MOBA_BLOCK].reshape(B, H, nb, MOBA_BLOCK, Dh)
    vb = v[:, :, : nb * MOBA_BLOCK].reshape(B, H, nb, MOBA_BLOCK, Dh)
    if nb < MOBA_TOPK:
        pad = ((0, 0), (0, 0), (0, MOBA_TOPK - nb), (0, 0), (0, 0))
        kb = jnp.pad(kb, pad)
        vb = jnp.pad(vb, pad)
    means = jnp.mean(kb.astype(jnp.float32), axis=3)
    return kb, vb, means


def moba_query_block(q, q_pos, k, v, kb, vb, means):
    B, H, Q, Dh = q.shape
    L = k.shape[2]
    nb = means.shape[2]
    n_past = q_pos // MOBA_BLOCK
    s = jnp.einsum('bhqd,bhnd->bhqn', q.astype(jnp.float32), means)
    blk_ok = jnp.arange(nb)[None, :] < n_past[:, None]
    s = jnp.where(blk_ok, s, -jnp.inf)
    _, sel = lax.top_k(s, MOBA_TOPK)
    sel_ok = sel < n_past[:, None]
    bi = jnp.arange(B)[:, None, None, None]
    hi = jnp.arange(H)[None, :, None, None]
    ks = kb[bi, hi, sel]
    vs = vb[bi, hi, sel]
    own_idx = (q_pos // MOBA_BLOCK * MOBA_BLOCK)[:, None] + jnp.arange(MOBA_BLOCK)[None, :]
    own_ok = own_idx <= q_pos[:, None]
    own_idx = jnp.minimum(own_idx, L - 1)
    ko = k[:, :, own_idx]
    vo = v[:, :, own_idx]
    scale = HEAD_DIM ** -0.5
    l_sel = jnp.einsum('bhqd,bhqnkd->bhqnk', q, ks).astype(jnp.float32) * scale
    l_sel = jnp.where(sel_ok[..., None], l_sel, NEG_INF).reshape(B, H, Q, MOBA_TOPK * MOBA_BLOCK)
    l_own = jnp.einsum('bhqd,bhqkd->bhqk', q, ko).astype(jnp.float32) * scale
    l_own = jnp.where(own_ok, l_own, NEG_INF)
    p = jax.nn.softmax(jnp.concatenate([l_sel, l_own], axis=-1), axis=-1).astype(v.dtype)
    p_sel = p[..., : MOBA_TOPK * MOBA_BLOCK].reshape(B, H, Q, MOBA_TOPK, MOBA_BLOCK)
    p_own = p[..., MOBA_TOPK * MOBA_BLOCK:]
    return (jnp.einsum('bhqnk,bhqnkd->bhqd', p_sel, vs)
            + jnp.einsum('bhqk,bhqkd->bhqd', p_own, vo))


def moba(q, k, v, q_pos):
    B, H, T, Dh = q.shape
    kb, vb, means = moba_blocks(k, v)

    def attend(qp):
        return moba_query_block(qp[0], qp[1], k, v, kb, vb, means)

    if T > MOBA_Q_BLOCK and T % MOBA_Q_BLOCK == 0:
        nq = T // MOBA_Q_BLOCK
        qs = q.reshape(B, H, nq, MOBA_Q_BLOCK, Dh).transpose(2, 0, 1, 3, 4)
        ps = q_pos.reshape(nq, MOBA_Q_BLOCK)
        out = lax.map(attend, (qs, ps))
        return out.transpose(1, 2, 0, 3, 4).reshape(B, H, T, Dh)
    return attend((q, q_pos))


def gla(q, k, v, log_a, s0):
    B, H, T, dk = q.shape
    dv = v.shape[-1]
    c = math.gcd(T, GLA_CHUNK)
    nc = T // c

    def chunks(a):
        return a.astype(jnp.float32).reshape(B, H, nc, c, a.shape[-1]).transpose(2, 0, 1, 3, 4)

    causal = jnp.tril(jnp.ones((c, c), dtype=bool))

    def step(s, inp):
        qc, kc, vc, gc = inp
        b = jnp.cumsum(gc, axis=2)
        b_last = b[:, :, -1:, :]
        qi = qc * jnp.exp(b)
        ki = kc * jnp.exp(-b)
        a = jnp.where(causal, jnp.einsum('bhcd,bhsd->bhcs', qi, ki), 0.0)
        o = jnp.einsum('bhcs,bhsv->bhcv', a, vc) + jnp.einsum('bhcd,bhdv->bhcv', qi, s)
        kd = kc * jnp.exp(b_last - b)
        s = jnp.exp(b_last[:, :, 0, :])[..., None] * s + jnp.einsum('bhsd,bhsv->bhdv', kd, vc)
        return s, o

    s_fin, o = lax.scan(step, s0.astype(jnp.float32),
                        (chunks(q), chunks(k), chunks(v), chunks(log_a)))
    return o.transpose(1, 2, 0, 3, 4).reshape(B, H, T, dv), s_fin


def decoder_layer(x, past_k, past_v, s0, pos0, norm_g, w_in, w_gate2, b_gate, gla_norm_g,
                  w_branch_a, w_branch_b, b_merge, w_out):
    B, T, _ = x.shape
    xn = rmsnorm(x, norm_g)
    proj = xn @ w_in
    splits = [int(s) for s in np.cumsum(IN_WIDTHS)[:-1]]
    qa, ka, va, za, qb, kb, vb, zb, a_lr, g_merge = jnp.split(proj, splits, axis=-1)

    def heads(a, h):
        return a.reshape(B, T, h, -1).transpose(0, 2, 1, 3)

    k_rows = ka.reshape(B, T, MOBA_HEADS, HEAD_DIM)
    v_rows = va.reshape(B, T, MOBA_HEADS, HEAD_DIM)
    k_full = k_rows.transpose(0, 2, 1, 3)
    v_full = v_rows.transpose(0, 2, 1, 3)
    if past_k is not None:
        k_full = jnp.concatenate([past_k, k_full], axis=2)
        v_full = jnp.concatenate([past_v, v_full], axis=2)
    q_pos = pos0 + jnp.arange(T, dtype=jnp.int32)
    o_a = moba(heads(qa, MOBA_HEADS), k_full, v_full, q_pos)
    o_a = o_a.transpose(0, 2, 1, 3).reshape(B, T, MOBA_WIDTH) * jax.nn.silu(za)
    y_a = o_a @ w_branch_a

    log_a = jax.nn.log_sigmoid((a_lr @ w_gate2 + b_gate).astype(jnp.float32)) / GLA_TAU
    o_b, s_new = gla(heads(qb, GLA_HEADS) * GLA_DK ** -0.5, heads(kb, GLA_HEADS),
                     heads(vb, GLA_HEADS), heads(log_a, GLA_HEADS), s0)
    o_b = rmsnorm(o_b, gla_norm_g[None, :, None, :]).astype(x.dtype)
    o_b = o_b.transpose(0, 2, 1, 3).reshape(B, T, GLA_V_WIDTH) * jax.nn.silu(zb)
    y_b = o_b @ w_branch_b

    g_a, g_b = jnp.split(jax.nn.sigmoid(g_merge + b_merge), 2, axis=-1)
    x = x + (g_a * y_a + g_b * y_b) @ w_out
    return x, k_rows, v_rows, s_new.astype(s0.dtype)


def setup_inputs(seed: int = 0) -> dict:
    key = jax.random.key(seed)
    ks = jax.random.split(key, 16)
    n_pages = PAST_LEN // PAGE_SIZE
    n_used = DEC_BATCH * n_pages
    n_pool = (5 * n_used + 3) // 4

    def nrm(k, shape, scale):
        return jax.random.normal(k, shape, jnp.float32) * scale

    x_prompt = nrm(ks[0], (BATCH, SEQ, D_MODEL), 1.0)
    x_sample = nrm(ks[1], (DEC_BATCH, DEC_SEQ, D_MODEL), 1.0)
    cache_k = nrm(ks[2], (DEPTH, n_pool, PAGE_SIZE, MOBA_HEADS, HEAD_DIM), 1.0)
    cache_v = nrm(ks[3], (DEPTH, n_pool, PAGE_SIZE, MOBA_HEADS, HEAD_DIM), 1.0)
    state_gla = nrm(ks[4], (DEPTH, DEC_BATCH, GLA_HEADS, GLA_DK, GLA_DV), 0.5)
    page_table = jax.random.permutation(ks[5], n_pool)[:n_used].reshape(DEC_BATCH, n_pages).astype(jnp.int32)
    norm_g = 1.0 + nrm(ks[6], (DEPTH, D_MODEL), 0.01)
    w_in = nrm(ks[7], (DEPTH, D_MODEL, N_IN), D_MODEL ** -0.5)
    w_gate2 = nrm(ks[8], (DEPTH, GLA_GATE_RANK, GLA_K_WIDTH), GLA_GATE_RANK ** -0.5)
    b_gate = nrm(ks[9], (DEPTH, GLA_K_WIDTH), 0.01)
    gla_norm_g = 1.0 + nrm(ks[10], (DEPTH, GLA_HEADS, GLA_DV), 0.01)
    w_branch_a = nrm(ks[11], (DEPTH, MOBA_WIDTH, D_MODEL), MOBA_WIDTH ** -0.5)
    w_branch_b = nrm(ks[12], (DEPTH, GLA_V_WIDTH, D_MODEL), GLA_V_WIDTH ** -0.5)
    b_merge = nrm(ks[13], (DEPTH, 2 * D_MODEL), 0.01)
    w_out = nrm(ks[14], (DEPTH, D_MODEL, D_MODEL), D_MODEL ** -0.5)
    final_norm_g = 1.0 + nrm(ks[15], (D_MODEL,), 0.01)
    return {'x_prompt': x_prompt, 'x_sample': x_sample, 'cache_k': cache_k, 'cache_v': cache_v,
            'state_gla': state_gla, 'page_table': page_table, 'norm_g': norm_g, 'w_in': w_in,
            'w_gate2': w_gate2, 'b_gate': b_gate, 'gla_norm_g': gla_norm_g,
            'w_branch_a': w_branch_a, 'w_branch_b': w_branch_b, 'b_merge': b_merge,
            'w_out': w_out, 'final_norm_g': final_norm_g}


def reference(x_prompt, x_sample, cache_k, cache_v, state_gla, page_table, norm_g, w_in,
              w_gate2, b_gate, gla_norm_g, w_branch_a, w_branch_b, b_merge, w_out, final_norm_g):
    dec_b, n_pages = page_table.shape
    past_len = n_pages * cache_k.shape[2]
    bp = x_prompt.shape[0]
    xp, xs = x_prompt, x_sample
    kp_l, vp_l, sp_l, ks_l, vs_l, ss_l = [], [], [], [], [], []
    for l in range(DEPTH):
        w = (norm_g[l], w_in[l], w_gate2[l], b_gate[l], gla_norm_g[l],
             w_branch_a[l], w_branch_b[l], b_merge[l], w_out[l])
        s0p = jnp.zeros((bp, GLA_HEADS, GLA_DK, GLA_DV), state_gla.dtype)
        xp, kp, vp, sp = decoder_layer(xp, None, None, s0p, 0, *w)
        past_k = cache_k[l][page_table].reshape(dec_b, past_len, MOBA_HEADS, HEAD_DIM).transpose(0, 2, 1, 3)
        past_v = cache_v[l][page_table].reshape(dec_b, past_len, MOBA_HEADS, HEAD_DIM).transpose(0, 2, 1, 3)
        xs, ksm, vsm, ssm = decoder_layer(xs, past_k, past_v, state_gla[l], past_len, *w)
        kp_l.append(kp)
        vp_l.append(vp)
        sp_l.append(sp)
        ks_l.append(ksm)
        vs_l.append(vsm)
        ss_l.append(ssm)
    y_prompt = rmsnorm(xp, final_norm_g)
    y_sample = rmsnorm(xs, final_norm_g)
    return (y_prompt, y_sample, jnp.stack(kp_l), jnp.stack(vp_l), jnp.stack(sp_l),
            jnp.stack(ks_l), jnp.stack(vs_l), jnp.stack(ss_l))
```

```python
import functools

import jax
import jax.numpy as jnp
from jax import lax
from jax.experimental import pallas as pl
from jax.experimental.pallas import tpu as pltpu

D_MODEL = 2048
MOBA_HEADS = 8
HEAD_DIM = 128
MOBA_BLOCK = 256
MOBA_TOPK = 3
GLA_HEADS = 4
GLA_DK = 128
GLA_DV = 256
GLA_GATE_RANK = 16
GLA_TAU = 16.0
GLA_SUB = 32
RMS_EPS = 1e-6
NEG_INF = -1e30
MOBA_WIDTH = MOBA_HEADS * HEAD_DIM
GLA_K_WIDTH = GLA_HEADS * GLA_DK
GLA_V_WIDTH = GLA_HEADS * GLA_DV

COL_QA = 0
COL_ZA = COL_QA + MOBA_WIDTH
COL_QB = COL_ZA + MOBA_WIDTH
COL_KB = COL_QB + GLA_K_WIDTH
COL_VB = COL_KB + GLA_K_WIDTH
COL_ZB = COL_VB + GLA_V_WIDTH
COL_GA = COL_ZB + GLA_V_WIDTH
COL_GB = COL_GA + D_MODEL
MAIN_WIDTH = COL_GB + D_MODEL

LANES = 128
SUBLANES = 8
SAMPLE_ROWS = SUBLANES
VMEM_LIMIT = 48 * 1024 * 1024

F32 = jnp.float32
BF16 = jnp.bfloat16
NT_DIMS = (((1,), (1,)), ((), ()))
TN_DIMS = (((0,), (0,)), ((), ()))


def _params(*sem):
    return pltpu.CompilerParams(dimension_semantics=sem, vmem_limit_bytes=VMEM_LIMIT)


def _silu(z):
    return z * (1.0 / (1.0 + jnp.exp(-z)))


def _sigmoid(z):
    return 1.0 / (1.0 + jnp.exp(-z))


def _rmsnorm_kernel(x_ref, g_ref, o_ref):
    x = x_ref[...]
    r = lax.rsqrt(jnp.mean(x * x, axis=-1, keepdims=True) + RMS_EPS)
    o_ref[...] = (x * r * g_ref[...]).astype(o_ref.dtype)


def _rmsnorm(x, g, out_dtype):
    m, d = x.shape
    tm = min(m, 512)
    return pl.pallas_call(
        _rmsnorm_kernel,
        grid=(pl.cdiv(m, tm),),
        in_specs=[pl.BlockSpec((tm, d), lambda i: (i, 0)),
                  pl.BlockSpec((1, d), lambda i: (0, 0))],
        out_specs=pl.BlockSpec((tm, d), lambda i: (i, 0)),
        out_shape=jax.ShapeDtypeStruct((m, d), out_dtype),
        compiler_params=_params("parallel"),
        name="rmsnorm",
    )(x, g.reshape(1, d))


def _mm_kernel(x_ref, w_ref, *o_refs):
    acc = jnp.dot(x_ref[...], w_ref[...], preferred_element_type=F32)
    for o_ref in o_refs:
        o_ref[...] = acc.astype(o_ref.dtype)


def _mm(x, w, out_dtypes, name):
    m, k = x.shape
    n = w.shape[1]
    tm = min(m, 512)
    tn = min(n, 1024)
    outs = pl.pallas_call(
        _mm_kernel,
        grid=(n // tn, pl.cdiv(m, tm)),
        in_specs=[pl.BlockSpec((tm, k), lambda j, i: (i, 0)),
                  pl.BlockSpec((k, tn), lambda j, i: (0, j))],
        out_specs=[pl.BlockSpec((tm, tn), lambda j, i: (i, j)) for _ in out_dtypes],
        out_shape=[jax.ShapeDtypeStruct((m, n), dt) for dt in out_dtypes],
        compiler_params=_params("parallel", "parallel"),
        name=name,
    )(x, w)
    return outs


def _gate_kernel(x_ref, w1_ref, w2_ref, b_ref, o_ref):
    a = jnp.dot(x_ref[...], w1_ref[...], preferred_element_type=F32).astype(BF16)
    y = jnp.dot(a, w2_ref[...], preferred_element_type=F32) + b_ref[...]
    log_sig = jnp.minimum(y, 0.0) - jnp.log(1.0 + jnp.exp(-jnp.abs(y)))
    o_ref[...] = log_sig * (1.0 / GLA_TAU)


def _gate(xn, w1, w2, b):
    m, k = xn.shape
    tm = min(m, 512)
    return pl.pallas_call(
        _gate_kernel,
        grid=(pl.cdiv(m, tm),),
        in_specs=[pl.BlockSpec((tm, k), lambda i: (i, 0)),
                  pl.BlockSpec((k, LANES), lambda i: (0, 0)),
                  pl.BlockSpec((LANES, GLA_K_WIDTH), lambda i: (0, 0)),
                  pl.BlockSpec((1, GLA_K_WIDTH), lambda i: (0, 0))],
        out_specs=pl.BlockSpec((tm, GLA_K_WIDTH), lambda i: (i, 0)),
        out_shape=jax.ShapeDtypeStruct((m, GLA_K_WIDTH), F32),
        compiler_params=_params("parallel"),
        name="gla_gate",
    )(xn, w1, w2, b.reshape(1, GLA_K_WIDTH))


def _merge_kernel(oa_ref, ob_ref, wa_ref, wb_ref, ga_ref, gb_ref, ba_ref, bb_ref, o_ref):
    ya = jnp.dot(oa_ref[...], wa_ref[...], preferred_element_type=F32)
    yb = jnp.dot(ob_ref[...], wb_ref[...], preferred_element_type=F32)
    ga = _sigmoid(ga_ref[...].astype(F32) + ba_ref[...])
    gb = _sigmoid(gb_ref[...].astype(F32) + bb_ref[...])
    o_ref[...] = (ga * ya + gb * yb).astype(o_ref.dtype)


def _merge(oa, ob, wa, wb, main, b_merge):
    m = oa.shape[0]
    tm = min(m, 512)
    tn = 1024
    nj = D_MODEL // tn
    bm = b_merge.reshape(1, 2 * D_MODEL)
    return pl.pallas_call(
        _merge_kernel,
        grid=(nj, pl.cdiv(m, tm)),
        in_specs=[pl.BlockSpec((tm, MOBA_WIDTH), lambda j, i: (i, 0)),
                  pl.BlockSpec((tm, GLA_V_WIDTH), lambda j, i: (i, 0)),
                  pl.BlockSpec((MOBA_WIDTH, tn), lambda j, i: (0, j)),
                  pl.BlockSpec((GLA_V_WIDTH, tn), lambda j, i: (0, j)),
                  pl.BlockSpec((tm, tn), lambda j, i: (i, COL_GA // tn + j)),
                  pl.BlockSpec((tm, tn), lambda j, i: (i, COL_GB // tn + j)),
                  pl.BlockSpec((1, tn), lambda j, i: (0, j)),
                  pl.BlockSpec((1, tn), lambda j, i: (0, nj + j))],
        out_specs=pl.BlockSpec((tm, tn), lambda j, i: (i, j)),
        out_shape=jax.ShapeDtypeStruct((m, D_MODEL), BF16),
        compiler_params=_params("parallel", "parallel"),
        name="gated_merge",
    )(oa, ob, wa, wb, main, main, bm, bm)


def _out_kernel(m_ref, w_ref, x_ref, g_ref, *o_refs, emit_x):
    x_new = x_ref[...] + jnp.dot(m_ref[...], w_ref[...], preferred_element_type=F32)
    r = lax.rsqrt(jnp.mean(x_new * x_new, axis=-1, keepdims=True) + RMS_EPS)
    xn = x_new * r * g_ref[...]
    if emit_x:
        o_refs[0][...] = x_new
        o_refs[1][...] = xn.astype(o_refs[1].dtype)
    else:
        o_refs[0][...] = xn.astype(o_refs[0].dtype)


def _out_proj(merged, w_out, x, g_next, emit_x):
    m = x.shape[0]
    tm = min(m, 256)
    row = pl.BlockSpec((tm, D_MODEL), lambda i: (i, 0))
    if emit_x:
        out_specs = [row, row]
        out_shape = [jax.ShapeDtypeStruct((m, D_MODEL), F32),
                     jax.ShapeDtypeStruct((m, D_MODEL), BF16)]
    else:
        out_specs = [row]
        out_shape = [jax.ShapeDtypeStruct((m, D_MODEL), F32)]
    return pl.pallas_call(
        functools.partial(_out_kernel, emit_x=emit_x),
        grid=(pl.cdiv(m, tm),),
        in_specs=[row,
                  pl.BlockSpec((D_MODEL, D_MODEL), lambda i: (0, 0)),
                  row,
                  pl.BlockSpec((1, D_MODEL), lambda i: (0, 0))],
        out_specs=out_specs,
        out_shape=out_shape,
        compiler_params=_params("parallel"),
        name="out_proj",
    )(merged, w_out, x, g_next.reshape(1, D_MODEL))


def _block_mean_kernel(k_ref, o_ref):
    o_ref[...] = jnp.sum(k_ref[...], axis=0, keepdims=True) * (1.0 / MOBA_BLOCK)


def _block_means(k32):
    nblk = k32.shape[0] // MOBA_BLOCK
    return pl.pallas_call(
        _block_mean_kernel,
        grid=(nblk,),
        in_specs=[pl.BlockSpec((MOBA_BLOCK, MOBA_WIDTH), lambda i: (i, 0))],
        out_specs=pl.BlockSpec((None, 1, MOBA_WIDTH), lambda i: (i, 0, 0)),
        out_shape=jax.ShapeDtypeStruct((nblk, 1, MOBA_WIDTH), F32),
        compiler_params=_params("parallel"),
        name="moba_block_means",
    )(k32)


def _top_k_mask(s, idx, n_cand):
    sel = jnp.zeros(s.shape, F32)
    for _ in range(MOBA_TOPK):
        mx = jnp.max(s, axis=1, keepdims=True)
        cand = jnp.where((s == mx) & (mx > -jnp.inf), idx, float(n_cand))
        first = jnp.min(cand, axis=1, keepdims=True)
        pick = idx == first
        sel = jnp.where(pick, 1.0, sel)
        s = jnp.where(pick, -jnp.inf, s)
    return sel


def _moba_prompt_kernel(q_ref, z_ref, k_ref, v_ref, mean_ref, o_ref, m_scr, l_scr, acc_scr, *, nblk):
    j = pl.program_id(2)
    scale = HEAD_DIM ** -0.5
    q = q_ref[...]

    s = lax.dot_general(q, mean_ref[...].astype(BF16), NT_DIMS, preferred_element_type=F32)
    blk = lax.broadcasted_iota(jnp.int32, s.shape, 1)
    blk_f = blk.astype(F32)
    s = jnp.where(blk < j, s, -jnp.inf)
    sel = _top_k_mask(s, blk_f, nblk)

    r0 = pl.multiple_of(j * MOBA_BLOCK, MOBA_BLOCK)
    logits = lax.dot_general(q, k_ref[pl.ds(r0, MOBA_BLOCK), :], NT_DIMS, preferred_element_type=F32) * scale
    row = lax.broadcasted_iota(jnp.int32, logits.shape, 0)
    col = lax.broadcasted_iota(jnp.int32, logits.shape, 1)
    logits = jnp.where(col <= row, logits, NEG_INF)
    m0 = jnp.max(logits, axis=1, keepdims=True)
    p = jnp.exp(logits - m0)
    m_scr[...] = m0
    l_scr[...] = jnp.sum(p, axis=1, keepdims=True)
    acc_scr[...] = jnp.dot(p.astype(BF16), v_ref[pl.ds(r0, MOBA_BLOCK), :], preferred_element_type=F32)

    def past_block(n, carry):
        r = pl.multiple_of(n * MOBA_BLOCK, MOBA_BLOCK)
        lg = lax.dot_general(q, k_ref[pl.ds(r, MOBA_BLOCK), :], NT_DIMS, preferred_element_type=F32) * scale
        chosen = jnp.max(jnp.where(blk == n, sel, 0.0), axis=1, keepdims=True) > 0.0
        m_old = m_scr[...]
        m_new = jnp.where(chosen, jnp.maximum(m_old, jnp.max(lg, axis=1, keepdims=True)), m_old)
        alpha = jnp.exp(m_old - m_new)
        pb = jnp.exp(lg - jnp.where(chosen, m_new, -NEG_INF))
        l_scr[...] = alpha * l_scr[...] + jnp.sum(pb, axis=1, keepdims=True)
        acc_scr[...] = alpha * acc_scr[...] + jnp.dot(
            pb.astype(BF16), v_ref[pl.ds(r, MOBA_BLOCK), :], preferred_element_type=F32)
        m_scr[...] = m_new
        return carry

    lax.fori_loop(0, j, past_block, 0)

    z = z_ref[...].astype(F32)
    o_ref[...] = (acc_scr[...] / l_scr[...] * _silu(z)).astype(o_ref.dtype)


def _moba_prompt(main, k16, v16, means, batch, seq):
    nblk = seq // MOBA_BLOCK
    qmap = lambda b, h, j: (b * nblk + j, COL_QA // HEAD_DIM + h)
    zmap = lambda b, h, j: (b * nblk + j, COL_ZA // HEAD_DIM + h)
    kvmap = lambda b, h, j: (b, h)
    return pl.pallas_call(
        functools.partial(_moba_prompt_kernel, nblk=nblk),
        grid=(batch, MOBA_HEADS, nblk),
        in_specs=[pl.BlockSpec((MOBA_BLOCK, HEAD_DIM), qmap),
                  pl.BlockSpec((MOBA_BLOCK, HEAD_DIM), zmap),
                  pl.BlockSpec((seq, HEAD_DIM), kvmap),
                  pl.BlockSpec((seq, HEAD_DIM), kvmap),
                  pl.BlockSpec((None, nblk, HEAD_DIM), lambda b, h, j: (b, 0, h))],
        out_specs=pl.BlockSpec((MOBA_BLOCK, HEAD_DIM), lambda b, h, j: (b * nblk + j, h)),
        out_shape=jax.ShapeDtypeStruct((batch * seq, MOBA_WIDTH), BF16),
        scratch_shapes=[pltpu.VMEM((MOBA_BLOCK, 1), F32),
                        pltpu.VMEM((MOBA_BLOCK, 1), F32),
                        pltpu.VMEM((MOBA_BLOCK, HEAD_DIM), F32)],
        compiler_params=_params("parallel", "parallel", "arbitrary"),
        name="moba_prompt",
    )(main, main, k16, v16, means)


def _gla_kernel(q_ref, k_ref, v_ref, g_ref, z_ref, s0_ref, gn_ref, o_ref, sout_ref, s_scr,
                *, chunk, sub, n_valid):
    t = pl.program_id(2)

    @pl.when(t == 0)
    def _load_state():
        s_scr[...] = s0_ref[...]

    q = q_ref[...].astype(F32) * (GLA_DK ** -0.5)
    k = k_ref[...].astype(F32)
    v = v_ref[...]
    g = g_ref[...]
    if n_valid < chunk:
        live = lax.broadcasted_iota(jnp.int32, g.shape, 0) < n_valid
        g = jnp.where(live, g, 0.0)
        k = jnp.where(live, k, 0.0)

    ri = lax.broadcasted_iota(jnp.int32, (chunk, chunk), 0)
    ci = lax.broadcasted_iota(jnp.int32, (chunk, chunk), 1)
    causal = ci <= ri
    b = jnp.dot(causal.astype(F32), g, precision=lax.Precision.HIGHEST, preferred_element_type=F32)
    b_last = b[chunk - 1:chunk]

    nsub = chunk // sub
    ends = [b[(i + 1) * sub - 1:(i + 1) * sub] for i in range(nsub)]
    own_end = jnp.concatenate([jnp.broadcast_to(e, (sub, GLA_DK)) for e in ends], axis=0)
    k_rel = k * jnp.exp(own_end - b)
    row_sub = lax.broadcasted_iota(jnp.int32, (chunk, GLA_DK), 0) // sub
    q_parts, k_parts = [], []
    for i in range(nsub):
        q_parts.append(jnp.where(row_sub >= i, q * jnp.exp(b - ends[i]), 0.0).astype(BF16))
        k_parts.append(jnp.where(row_sub == i, k_rel, 0.0).astype(BF16))
    q_cat = jnp.concatenate(q_parts, axis=1) if nsub > 1 else q_parts[0]
    k_cat = jnp.concatenate(k_parts, axis=1) if nsub > 1 else k_parts[0]
    a = lax.dot_general(q_cat, k_cat, NT_DIMS, preferred_element_type=F32)
    a = jnp.where(causal, a, 0.0)

    state = s_scr[...]
    o = (jnp.dot(a.astype(BF16), v, preferred_element_type=F32)
         + jnp.dot((q * jnp.exp(b)).astype(BF16), state.astype(BF16), preferred_element_type=F32))

    k_dec = (k * jnp.exp(b_last - b)).astype(BF16)
    update = lax.dot_general(k_dec, v, TN_DIMS, preferred_element_type=F32)
    decay_t = jnp.transpose(jnp.broadcast_to(jnp.exp(b_last), (GLA_DK, GLA_DK)))
    new_state = jnp.concatenate([decay_t] * (GLA_DV // GLA_DK), axis=1) * state + update
    s_scr[...] = new_state

    r = lax.rsqrt(jnp.mean(o * o, axis=-1, keepdims=True) + RMS_EPS)
    z = z_ref[...].astype(F32)
    o_ref[...] = (o * r * gn_ref[...] * _silu(z)).astype(o_ref.dtype)

    @pl.when(t == pl.num_programs(2) - 1)
    def _store_state():
        sout_ref[...] = new_state


def _gla(main, log_a, s0, gn, batch, rows_per_seq, chunk, sub, n_valid):
    nt = rows_per_seq // chunk
    m = main.shape[0]
    rmap = lambda off: (lambda b, h, t: (b * nt + t, off + h))
    smap = lambda b, h, t: (b, h, 0, 0)
    return pl.pallas_call(
        functools.partial(_gla_kernel, chunk=chunk, sub=sub, n_valid=n_valid),
        grid=(batch, GLA_HEADS, nt),
        in_specs=[pl.BlockSpec((chunk, GLA_DK), rmap(COL_QB // GLA_DK)),
                  pl.BlockSpec((chunk, GLA_DK), rmap(COL_KB // GLA_DK)),
                  pl.BlockSpec((chunk, GLA_DV), rmap(COL_VB // GLA_DV)),
                  pl.BlockSpec((chunk, GLA_DK), rmap(0)),
                  pl.BlockSpec((chunk, GLA_DV), rmap(COL_ZB // GLA_DV)),
                  pl.BlockSpec((None, None, GLA_DK, GLA_DV), smap),
                  pl.BlockSpec((None, 1, GLA_DV), lambda b, h, t: (h, 0, 0))],
        out_specs=[pl.BlockSpec((chunk, GLA_DV), rmap(0)),
                   pl.BlockSpec((None, None, GLA_DK, GLA_DV), smap)],
        out_shape=[jax.ShapeDtypeStruct((m, GLA_V_WIDTH), BF16),
                   jax.ShapeDtypeStruct((batch, GLA_HEADS, GLA_DK, GLA_DV), F32)],
        scratch_shapes=[pltpu.VMEM((GLA_DK, GLA_DV), F32)],
        compiler_params=_params("parallel", "parallel", "arbitrary"),
        name="gla",
    )(main, main, main, log_a, main, s0, gn.reshape(GLA_HEADS, 1, GLA_DV))


PAGES_PER_STEP = 4


def _page_mean_kernel(pt_ref, *refs):
    del pt_ref
    page_refs, o_ref = refs[:-1], refs[-1]
    pages_per_block = len(page_refs) // o_ref.shape[0]
    for blk in range(o_ref.shape[0]):
        tot = None
        for pg in range(pages_per_block):
            part = jnp.sum(page_refs[blk * pages_per_block + pg][...], axis=0)
            tot = part if tot is None else tot + part
        o_ref[blk] = tot * (1.0 / MOBA_BLOCK)


def _page_means(cache_k, page_table, layer):
    dec_b, n_pages = page_table.shape
    page = cache_k.shape[2]
    pages_per_block = MOBA_BLOCK // page
    blocks_per_step = PAGES_PER_STEP // pages_per_block
    n_blocks = n_pages // pages_per_block
    page_spec = lambda i: pl.BlockSpec(
        (None, None, page, MOBA_HEADS, HEAD_DIM),
        lambda b, g, pt: (layer, pt[b, g * PAGES_PER_STEP + i], 0, 0, 0))
    return pl.pallas_call(
        _page_mean_kernel,
        grid_spec=pltpu.PrefetchScalarGridSpec(
            num_scalar_prefetch=1,
            grid=(dec_b, n_pages // PAGES_PER_STEP),
            in_specs=[page_spec(i) for i in range(PAGES_PER_STEP)],
            out_specs=pl.BlockSpec((None, blocks_per_step, MOBA_HEADS, HEAD_DIM),
                                   lambda b, g, pt: (b, g, 0, 0)),
        ),
        out_shape=jax.ShapeDtypeStruct((dec_b, n_blocks, MOBA_HEADS, HEAD_DIM), F32),
        compiler_params=_params("parallel", "arbitrary"),
        name="moba_page_means",
    )(page_table, *([cache_k] * PAGES_PER_STEP))


def _sample_select_kernel(q_ref, mean_ref, o_ref, *, n_blocks):
    q = q_ref[...]
    means = mean_ref[...].astype(BF16)
    rows = q.shape[0]
    ncol = n_blocks * MOBA_HEADS
    col = lax.broadcasted_iota(jnp.int32, (rows, ncol), 1)
    col_head = col % MOBA_HEADS
    col_f = col.astype(F32)
    s_all = jnp.zeros((rows, ncol), F32)
    for h in range(MOBA_HEADS):
        s_h = lax.dot_general(q[:, h * HEAD_DIM:(h + 1) * HEAD_DIM], means, NT_DIMS, preferred_element_type=F32)
        s_all = jnp.where(col_head == h, s_h, s_all)
    lane = lax.broadcasted_iota(jnp.int32, o_ref.shape, 1)
    out = jnp.zeros(o_ref.shape, jnp.int32)
    for h in range(MOBA_HEADS):
        s = jnp.where(col_head == h, s_all, -jnp.inf)
        for t in range(MOBA_TOPK):
            mx = jnp.max(s, axis=1, keepdims=True)
            cand = jnp.where((s == mx) & (mx > -jnp.inf), col_f, float(ncol))
            first = jnp.min(cand, axis=1, keepdims=True)
            s = jnp.where(col_f == first, -jnp.inf, s)
            blk = first.astype(jnp.int32) // MOBA_HEADS
            out = jnp.where(lane == h * MOBA_TOPK + t, blk, out)
    o_ref[...] = out


def _sample_select(main, means, dec_b):
    n_blocks = means.shape[1]
    means2 = means.reshape(dec_b, n_blocks * MOBA_HEADS, HEAD_DIM)
    return pl.pallas_call(
        functools.partial(_sample_select_kernel, n_blocks=n_blocks),
        grid=(dec_b,),
        in_specs=[pl.BlockSpec((SAMPLE_ROWS, MOBA_WIDTH), lambda b: (b, COL_QA // MOBA_WIDTH)),
                  pl.BlockSpec((None, n_blocks * MOBA_HEADS, HEAD_DIM), lambda b: (b, 0, 0))],
        out_specs=pl.BlockSpec((None, SAMPLE_ROWS, LANES), lambda b: (b, 0, 0)),
        out_shape=jax.ShapeDtypeStruct((dec_b, SAMPLE_ROWS, LANES), jnp.int32),
        compiler_params=_params("parallel"),
        name="moba_sample_select",
    )(main, means2)


def _moba_sample_kernel(pt_ref, sel_ref, q_ref, z_ref, ko_ref, vo_ref, ck_ref, cv_ref, o_ref,
                        kbuf, vbuf, sem, *, layer, n_new, page):
    b = pl.program_id(0)
    h = pl.program_id(1)
    pages_per_block = MOBA_BLOCK // page
    n_slots = n_new * MOBA_TOPK * pages_per_block

    def page_copies(slot):
        qi, rest = divmod(slot, MOBA_TOPK * pages_per_block)
        rank, pg = divmod(rest, pages_per_block)
        blk = sel_ref[b, qi * (MOBA_HEADS * MOBA_TOPK) + h * MOBA_TOPK + rank]
        pid = pt_ref[b, blk * pages_per_block + pg]
        dst = pl.ds(slot * page, page)
        return (pltpu.make_async_copy(ck_ref.at[layer, pid, :, h, :], kbuf.at[dst, :], sem.at[0]),
                pltpu.make_async_copy(cv_ref.at[layer, pid, :, h, :], vbuf.at[dst, :], sem.at[1]))

    for slot in range(n_slots):
        for c in page_copies(slot):
            c.start()
    for slot in range(n_slots):
        for c in page_copies(slot):
            c.wait()

    scale = HEAD_DIM ** -0.5
    q = q_ref[...]
    rows = q.shape[0]
    n_keys = n_slots * page
    lg = lax.dot_general(q, kbuf[...].astype(BF16), NT_DIMS, preferred_element_type=F32) * scale
    row = lax.broadcasted_iota(jnp.int32, (rows, n_keys), 0)
    key_owner = lax.broadcasted_iota(jnp.int32, (rows, n_keys), 1) // (MOBA_TOPK * MOBA_BLOCK)
    lg = jnp.where(key_owner == row, lg, NEG_INF)
    lg_own = lax.dot_general(q, ko_ref[...], NT_DIMS, preferred_element_type=F32) * scale
    r2 = lax.broadcasted_iota(jnp.int32, lg_own.shape, 0)
    c2 = lax.broadcasted_iota(jnp.int32, lg_own.shape, 1)
    lg_own = jnp.where((c2 <= r2) & (c2 < n_new), lg_own, NEG_INF)
    m = jnp.maximum(jnp.max(lg, axis=1, keepdims=True), jnp.max(lg_own, axis=1, keepdims=True))
    p = jnp.exp(lg - m)
    p_own = jnp.exp(lg_own - m)
    denom = jnp.sum(p, axis=1, keepdims=True) + jnp.sum(p_own, axis=1, keepdims=True)
    acc = (jnp.dot(p.astype(BF16), vbuf[...].astype(BF16), preferred_element_type=F32)
           + jnp.dot(p_own.astype(BF16), vo_ref[...], preferred_element_type=F32))
    z = z_ref[...].astype(F32)
    o_ref[...] = (acc / denom * _silu(z)).astype(o_ref.dtype)


def _moba_sample(main, k16, v16, cache_k, cache_v, page_table, sel, layer, n_new):
    dec_b = page_table.shape[0]
    page = cache_k.shape[2]
    n_keys = n_new * MOBA_TOPK * MOBA_BLOCK
    tile = lambda off: pl.BlockSpec((SAMPLE_ROWS, HEAD_DIM), lambda b, h, pt, sl: (b, off + h))
    return pl.pallas_call(
        functools.partial(_moba_sample_kernel, layer=layer, n_new=n_new, page=page),
        grid_spec=pltpu.PrefetchScalarGridSpec(
            num_scalar_prefetch=2,
            grid=(dec_b, MOBA_HEADS),
            in_specs=[tile(COL_QA // HEAD_DIM), tile(COL_ZA // HEAD_DIM), tile(0), tile(0),
                      pl.BlockSpec(memory_space=pl.ANY), pl.BlockSpec(memory_space=pl.ANY)],
            out_specs=tile(0),
            scratch_shapes=[pltpu.VMEM((n_keys, HEAD_DIM), F32),
                            pltpu.VMEM((n_keys, HEAD_DIM), F32),
                            pltpu.SemaphoreType.DMA((2,))],
        ),
        out_shape=jax.ShapeDtypeStruct((dec_b * SAMPLE_ROWS, MOBA_WIDTH), BF16),
        compiler_params=_params("arbitrary", "arbitrary"),
        name="moba_sample",
    )(page_table, sel, main, main, k16, v16, cache_k, cache_v)


def _layer_weights(l, w_in, w_gate2):
    w = w_in[l].astype(BF16)
    o = 0
    bounds = []
    for width in (MOBA_WIDTH, MOBA_WIDTH, MOBA_WIDTH, MOBA_WIDTH, GLA_K_WIDTH, GLA_K_WIDTH,
                  GLA_V_WIDTH, GLA_V_WIDTH, GLA_GATE_RANK, 2 * D_MODEL):
        bounds.append((o, o + width))
        o += width
    qa, ka, va, za, qb, kb, vb, zb, alr, gm = [w[:, lo:hi] for lo, hi in bounds]
    w_main = jnp.concatenate([qa, za, qb, kb, vb, zb, gm], axis=1)
    w_alr = jnp.pad(alr, ((0, 0), (0, LANES - GLA_GATE_RANK)))
    w_g2 = jnp.pad(w_gate2[l].astype(BF16), ((0, LANES - GLA_GATE_RANK), (0, 0)))
    return ka, va, w_main, w_alr, w_g2


def _token_proj(xn, wts, b_gate_l):
    ka, va, w_main, w_alr, w_g2 = wts
    k32, k16 = _mm(xn, ka, (F32, BF16), "proj_k")
    v32, v16 = _mm(xn, va, (F32, BF16), "proj_v")
    (main,) = _mm(xn, w_main, (BF16,), "proj_main")
    log_a = _gate(xn, w_alr, w_g2, b_gate_l)
    return k32, k16, v32, v16, main, log_a


def kernel(x_prompt, x_sample, cache_k, cache_v, state_gla, page_table, norm_g, w_in, w_gate2, b_gate,
           gla_norm_g, w_branch_a, w_branch_b, b_merge, w_out, final_norm_g):
    depth = w_in.shape[0]
    bp, seq, _ = x_prompt.shape
    dec_b, n_new, _ = x_sample.shape
    assert n_new <= SAMPLE_ROWS and seq % MOBA_BLOCK == 0
    assert cache_k.shape[2] * page_table.shape[1] % MOBA_BLOCK == 0

    xp = x_prompt.reshape(bp * seq, D_MODEL)
    xs = jnp.pad(x_sample, ((0, 0), (0, SAMPLE_ROWS - n_new), (0, 0))).reshape(dec_b * SAMPLE_ROWS, D_MODEL)
    xnp = _rmsnorm(xp, norm_g[0], BF16)
    xns = _rmsnorm(xs, norm_g[0], BF16)
    zero_state = jnp.zeros((bp, GLA_HEADS, GLA_DK, GLA_DV), F32)

    kp_l, vp_l, sp_l, ks_l, vs_l, ss_l = [], [], [], [], [], []
    for l in range(depth):
        wts = _layer_weights(l, w_in, w_gate2)
        wa = w_branch_a[l].astype(BF16)
        wb = w_branch_b[l].astype(BF16)
        wo = w_out[l].astype(BF16)
        last = l == depth - 1
        g_next = final_norm_g if last else norm_g[l + 1]

        k32, k16, v32, v16, main, log_a = _token_proj(xnp, wts, b_gate[l])
        means = _block_means(k32).reshape(bp, seq // MOBA_BLOCK, MOBA_WIDTH)
        oa = _moba_prompt(main, k16, v16, means, bp, seq)
        ob, sp = _gla(main, log_a, zero_state, gla_norm_g[l], bp, seq, MOBA_BLOCK, GLA_SUB, MOBA_BLOCK)
        merged = _merge(oa, ob, wa, wb, main, b_merge[l])
        outs = _out_proj(merged, wo, xp, g_next, not last)
        if last:
            (yp,) = outs
        else:
            xp, xnp = outs
        kp_l.append(k32.reshape(bp, seq, MOBA_HEADS, HEAD_DIM))
        vp_l.append(v32.reshape(bp, seq, MOBA_HEADS, HEAD_DIM))
        sp_l.append(sp)

        k32, k16, v32, v16, main, log_a = _token_proj(xns, wts, b_gate[l])
        pmeans = _page_means(cache_k, page_table, l)
        sel = _sample_select(main, pmeans, dec_b)
        sel = sel[:, :n_new, :MOBA_HEADS * MOBA_TOPK].reshape(dec_b, n_new * MOBA_HEADS * MOBA_TOPK)
        oa = _moba_sample(main, k16, v16, cache_k, cache_v, page_table, sel, l, n_new)
        ob, ss = _gla(main, log_a, state_gla[l], gla_norm_g[l], dec_b, SAMPLE_ROWS, SAMPLE_ROWS,
                      SAMPLE_ROWS, n_new)
        merged = _merge(oa, ob, wa, wb, main, b_merge[l])
        outs = _out_proj(merged, wo, xs, g_next, not last)
        if last:
            (ys,) = outs
        else:
            xs, xns = outs
        rows = lambda a: a.reshape(dec_b, SAMPLE_ROWS, MOBA_HEADS, HEAD_DIM)[:, :n_new]
        ks_l.append(rows(k32))
        vs_l.append(rows(v32))
        ss_l.append(ss)

    y_prompt = yp.reshape(bp, seq, D_MODEL)
    y_sample = ys.reshape(dec_b, SAMPLE_ROWS, D_MODEL)[:, :n_new]
    return (y_prompt, y_sample, jnp.stack(kp_l), jnp.stack(vp_l), jnp.stack(sp_l),
            jnp.stack(ks_l), jnp.stack(vs_l), jnp.stack(ss_l))
```

```python
import functools
import math

import jax
import jax.numpy as jnp
from jax import lax
from jax.experimental import pallas as pl
from jax.experimental.pallas import tpu as pltpu

D_MODEL = 2048
MOBA_HEADS = 8
HEAD_DIM = 128
MOBA_BLOCK = 256
MOBA_TOPK = 3
GLA_HEADS = 4
GLA_DK = 128
GLA_DV = 256
GLA_GATE_RANK = 16
GLA_TAU = 16.0
GLA_SUB = 32
RMS_EPS = 1e-6
NEG_INF = -1e30
MOBA_WIDTH = MOBA_HEADS * HEAD_DIM
GLA_K_WIDTH = GLA_HEADS * GLA_DK
GLA_V_WIDTH = GLA_HEADS * GLA_DV

W_QA = 0
W_KA = W_QA + MOBA_WIDTH
W_VA = W_KA + MOBA_WIDTH
W_ZA = W_VA + MOBA_WIDTH
W_QB = W_ZA + MOBA_WIDTH
W_KB = W_QB + GLA_K_WIDTH
W_VB = W_KB + GLA_K_WIDTH
W_ZB = W_VB + GLA_V_WIDTH
W_ALR = W_ZB + GLA_V_WIDTH
W_GM = W_ALR + GLA_GATE_RANK

COL_QA = 0
COL_ZA = COL_QA + MOBA_WIDTH
COL_QB = COL_ZA + MOBA_WIDTH
COL_KB = COL_QB + GLA_K_WIDTH
COL_VB = COL_KB + GLA_K_WIDTH
COL_ZB = COL_VB + GLA_V_WIDTH
MAIN_WIDTH = COL_ZB + GLA_V_WIDTH

LANES = 128
SUBLANES = 8
SAMPLE_ROWS = SUBLANES
PROJ_TN = 1024
VMEM_LIMIT = 56 * 1024 * 1024

F32 = jnp.float32
BF16 = jnp.bfloat16
NT_DIMS = (((1,), (1,)), ((), ()))
TN_DIMS = (((0,), (0,)), ((), ()))
LOG2E = math.log2(math.e)


def _params(*sem):
    return pltpu.CompilerParams(dimension_semantics=sem, vmem_limit_bytes=VMEM_LIMIT)


def _silu(z):
    return z * (1.0 / (1.0 + jnp.exp(-z)))


def _sigmoid(z):
    return 1.0 / (1.0 + jnp.exp(-z))


def _rmsnorm_kernel(x_ref, g_ref, o_ref):
    x = x_ref[...]
    r = lax.rsqrt(jnp.mean(x * x, axis=-1, keepdims=True) + RMS_EPS)
    o_ref[...] = (x * r * g_ref[...]).astype(o_ref.dtype)


def _rmsnorm(x, g, out_dtype):
    m, d = x.shape
    tm = min(m, 512)
    return pl.pallas_call(
        _rmsnorm_kernel,
        grid=(pl.cdiv(m, tm),),
        in_specs=[pl.BlockSpec((tm, d), lambda i: (i, 0)),
                  pl.BlockSpec((1, d), lambda i: (0, 0))],
        out_specs=pl.BlockSpec((tm, d), lambda i: (i, 0)),
        out_shape=jax.ShapeDtypeStruct((m, d), out_dtype),
        compiler_params=_params("parallel"),
        name="rmsnorm",
    )(x, g.reshape(1, d))


def _proj_kernel(x_ref, w_ref, *refs, transposed):
    *o_refs, w_bf = refs

    @pl.when(pl.program_id(1) == 0)
    def _cast_weight_tile():
        w_bf[...] = w_ref[...].astype(BF16)

    acc = jnp.dot(x_ref[...], w_bf[...], preferred_element_type=F32)
    for o_ref, tr in zip(o_refs, transposed):
        if tr:
            for r in range(o_ref.shape[0]):
                rows = acc[r * MOBA_BLOCK:(r + 1) * MOBA_BLOCK, :]
                o_ref[r] = jnp.transpose(rows).astype(o_ref.dtype)
        else:
            o_ref[...] = acc.astype(o_ref.dtype)


def _proj(xn, w_in, layer, col_block, n_blocks, outs, name):
    m, k = xn.shape
    tm = min(m, 512)
    tn = PROJ_TN
    n = n_blocks * tn
    out_specs, out_shape = [], []
    for dt, tr in outs:
        if tr:
            out_specs.append(pl.BlockSpec((tm // MOBA_BLOCK, tn, MOBA_BLOCK), lambda j, i: (i, j, 0)))
            out_shape.append(jax.ShapeDtypeStruct((m // MOBA_BLOCK, n, MOBA_BLOCK), dt))
        else:
            out_specs.append(pl.BlockSpec((tm, tn), lambda j, i: (i, j)))
            out_shape.append(jax.ShapeDtypeStruct((m, n), dt))
    return pl.pallas_call(
        functools.partial(_proj_kernel, transposed=tuple(tr for _, tr in outs)),
        grid=(n_blocks, pl.cdiv(m, tm)),
        in_specs=[pl.BlockSpec((tm, k), lambda j, i: (i, 0)),
                  pl.BlockSpec((None, k, tn), lambda j, i: (layer, 0, col_block(j)))],
        out_specs=out_specs,
        out_shape=out_shape,
        scratch_shapes=[pltpu.VMEM((k, tn), BF16)],
        compiler_params=_params("parallel", "arbitrary"),
        name=name,
    )(xn, w_in)


def _gates_kernel(x_ref, wa_ref, wb_ref, o_ref, w_bf, *, shift):
    @pl.when(pl.program_id(1) == 0)
    def _cast_weight_tile():
        w = jnp.concatenate([wa_ref[:, shift:], wb_ref[:, :shift]], axis=1)
        w_bf[...] = w.astype(BF16)

    o_ref[...] = jnp.dot(x_ref[...], w_bf[...], preferred_element_type=F32).astype(o_ref.dtype)


def _gates(xn, w_in, layer):
    m, k = xn.shape
    tm = min(m, 512)
    tn = PROJ_TN
    base = W_GM // LANES * LANES
    shift = W_GM - base
    assert base % tn == 0
    return pl.pallas_call(
        functools.partial(_gates_kernel, shift=shift),
        grid=(2 * D_MODEL // tn, pl.cdiv(m, tm)),
        in_specs=[pl.BlockSpec((tm, k), lambda j, i: (i, 0)),
                  pl.BlockSpec((None, k, tn), lambda j, i: (layer, 0, base // tn + j)),
                  pl.BlockSpec((None, k, LANES), lambda j, i: (layer, 0, (base + (j + 1) * tn) // LANES))],
        out_specs=pl.BlockSpec((tm, tn), lambda j, i: (i, j)),
        out_shape=jax.ShapeDtypeStruct((m, 2 * D_MODEL), BF16),
        scratch_shapes=[pltpu.VMEM((k, tn), BF16)],
        compiler_params=_params("parallel", "arbitrary"),
        name="proj_gates",
    )(xn, w_in, w_in)


def _gate_kernel(x_ref, w1_ref, w2_ref, b_ref, o_ref):
    a = jnp.dot(x_ref[...], w1_ref[...].astype(BF16), preferred_element_type=F32).astype(BF16)
    y = jnp.dot(a, w2_ref[...], preferred_element_type=F32) + b_ref[...]
    log_sig = jnp.minimum(y, 0.0) - jnp.log(1.0 + jnp.exp(-jnp.abs(y)))
    o_ref[...] = log_sig * (1.0 / GLA_TAU)


def _gate(xn, w_in, layer, w2, b):
    m, k = xn.shape
    tm = min(m, 512)
    assert W_ALR % LANES == 0
    return pl.pallas_call(
        _gate_kernel,
        grid=(pl.cdiv(m, tm),),
        in_specs=[pl.BlockSpec((tm, k), lambda i: (i, 0)),
                  pl.BlockSpec((None, k, LANES), lambda i: (layer, 0, W_ALR // LANES)),
                  pl.BlockSpec((LANES, GLA_K_WIDTH), lambda i: (0, 0)),
                  pl.BlockSpec((1, GLA_K_WIDTH), lambda i: (0, 0))],
        out_specs=pl.BlockSpec((tm, GLA_K_WIDTH), lambda i: (i, 0)),
        out_shape=jax.ShapeDtypeStruct((m, GLA_K_WIDTH), F32),
        compiler_params=_params("parallel"),
        name="gla_gate",
    )(xn, w_in, w2, b.reshape(1, GLA_K_WIDTH))


def _merge_kernel(oa_ref, ob_ref, wa_ref, wb_ref, ga_ref, gb_ref, ba_ref, bb_ref, o_ref):
    ya = jnp.dot(oa_ref[...], wa_ref[...], preferred_element_type=F32)
    yb = jnp.dot(ob_ref[...], wb_ref[...], preferred_element_type=F32)
    ga = _sigmoid(ga_ref[...].astype(F32) + ba_ref[...])
    gb = _sigmoid(gb_ref[...].astype(F32) + bb_ref[...])
    o_ref[...] = (ga * ya + gb * yb).astype(o_ref.dtype)


def _merge(oa, ob, wa, wb, gates, b_merge):
    m = oa.shape[0]
    tm = min(m, 512)
    tn = 1024
    nj = D_MODEL // tn
    bm = b_merge.reshape(1, 2 * D_MODEL)
    return pl.pallas_call(
        _merge_kernel,
        grid=(nj, pl.cdiv(m, tm)),
        in_specs=[pl.BlockSpec((tm, MOBA_WIDTH), lambda j, i: (i, 0)),
                  pl.BlockSpec((tm, GLA_V_WIDTH), lambda j, i: (i, 0)),
                  pl.BlockSpec((MOBA_WIDTH, tn), lambda j, i: (0, j)),
                  pl.BlockSpec((GLA_V_WIDTH, tn), lambda j, i: (0, j)),
                  pl.BlockSpec((tm, tn), lambda j, i: (i, j)),
                  pl.BlockSpec((tm, tn), lambda j, i: (i, nj + j)),
                  pl.BlockSpec((1, tn), lambda j, i: (0, j)),
                  pl.BlockSpec((1, tn), lambda j, i: (0, nj + j))],
        out_specs=pl.BlockSpec((tm, tn), lambda j, i: (i, j)),
        out_shape=jax.ShapeDtypeStruct((m, D_MODEL), BF16),
        compiler_params=_params("parallel", "parallel"),
        name="gated_merge",
    )(oa, ob, wa, wb, gates, gates, bm, bm)


def _out_kernel(m_ref, w_ref, x_ref, g_ref, *o_refs, emit_x):
    x_new = x_ref[...] + jnp.dot(m_ref[...], w_ref[...], preferred_element_type=F32)
    r = lax.rsqrt(jnp.mean(x_new * x_new, axis=-1, keepdims=True) + RMS_EPS)
    xn = x_new * r * g_ref[...]
    if emit_x:
        o_refs[0][...] = x_new
        o_refs[1][...] = xn.astype(o_refs[1].dtype)
    else:
        o_refs[0][...] = xn.astype(o_refs[0].dtype)


def _out_proj(merged, w_out, x, g_next, emit_x):
    m = x.shape[0]
    tm = min(m, 256)
    row = pl.BlockSpec((tm, D_MODEL), lambda i: (i, 0))
    if emit_x:
        out_specs = [row, row]
        out_shape = [jax.ShapeDtypeStruct((m, D_MODEL), F32),
                     jax.ShapeDtypeStruct((m, D_MODEL), BF16)]
    else:
        out_specs = [row]
        out_shape = [jax.ShapeDtypeStruct((m, D_MODEL), F32)]
    return pl.pallas_call(
        functools.partial(_out_kernel, emit_x=emit_x),
        grid=(pl.cdiv(m, tm),),
        in_specs=[row,
                  pl.BlockSpec((D_MODEL, D_MODEL), lambda i: (0, 0)),
                  row,
                  pl.BlockSpec((1, D_MODEL), lambda i: (0, 0))],
        out_specs=out_specs,
        out_shape=out_shape,
        compiler_params=_params("parallel"),
        name="out_proj",
    )(merged, w_out, x, g_next.reshape(1, D_MODEL))


def _block_mean_kernel(k_ref, o_ref):
    o_ref[...] = jnp.sum(k_ref[...], axis=0, keepdims=True) * (1.0 / MOBA_BLOCK)


def _block_means(k32):
    nblk = k32.shape[0] // MOBA_BLOCK
    return pl.pallas_call(
        _block_mean_kernel,
        grid=(nblk,),
        in_specs=[pl.BlockSpec((MOBA_BLOCK, MOBA_WIDTH), lambda i: (i, 0))],
        out_specs=pl.BlockSpec((None, 1, MOBA_WIDTH), lambda i: (i, 0, 0)),
        out_shape=jax.ShapeDtypeStruct((nblk, 1, MOBA_WIDTH), F32),
        compiler_params=_params("parallel"),
        name="moba_block_means",
    )(k32)


def _top_k_mask(s, idx, n_cand, axis):
    sel = jnp.zeros(s.shape, F32)
    for _ in range(MOBA_TOPK):
        mx = jnp.max(s, axis=axis, keepdims=True)
        cand = jnp.where((s == mx) & (mx > -jnp.inf), idx, float(n_cand))
        first = jnp.min(cand, axis=axis, keepdims=True)
        pick = idx == first
        sel = jnp.where(pick, 1.0, sel)
        s = jnp.where(pick, -jnp.inf, s)
    return sel


def _moba_prompt_kernel(q_ref, z_ref, k_ref, vt_ref, mean_ref, o_ref, sel_scr, acc_scr, *, nblk, heads, group):
    j = pl.program_id(2)
    c = (HEAD_DIM ** -0.5) * LOG2E
    lanes = [slice(h * HEAD_DIM, (h + 1) * HEAD_DIM) for h in range(heads)]
    qs = [q_ref[:, ln] for ln in lanes]

    def logits(h, n):
        r = pl.multiple_of(n * MOBA_BLOCK, MOBA_BLOCK)
        return lax.dot_general(k_ref[pl.ds(r, MOBA_BLOCK), lanes[h]], qs[h], NT_DIMS, preferred_element_type=F32)

    m0, l0 = [], []
    for h in range(heads):
        s = lax.dot_general(mean_ref[:, lanes[h]].astype(BF16), qs[h], NT_DIMS, preferred_element_type=F32)
        blk = lax.broadcasted_iota(jnp.int32, s.shape, 0)
        s = jnp.where(blk < j, s, -jnp.inf)
        sel_scr[h] = _top_k_mask(s, blk.astype(F32), nblk, 0)

        lg = logits(h, j)
        key = lax.broadcasted_iota(jnp.int32, lg.shape, 0)
        qry = lax.broadcasted_iota(jnp.int32, lg.shape, 1)
        lg = jnp.where(key <= qry, lg, NEG_INF)
        m = jnp.max(lg, axis=0, keepdims=True)
        p = jnp.exp2((lg - m) * c)
        m0.append(m)
        l0.append(jnp.sum(p, axis=0, keepdims=True))
        acc_scr[h] = jnp.dot(vt_ref[j, lanes[h], :], p.astype(BF16), preferred_element_type=F32)

    def past_blocks(i, carry):
        ms, ls = carry
        ns = [jnp.minimum(group * i + g, nblk - 1) for g in range(group)]
        lgs = [[logits(h, n) for n in ns] for h in range(heads)]
        new_m, new_l, alphas, ps = [], [], [], []
        for h in range(heads):
            chosen = [(sel_scr[h, pl.ds(n, 1), :] > 0.0) & (group * i + g < j) for g, n in enumerate(ns)]
            m_new = ms[h]
            for lg, ch in zip(lgs[h], chosen):
                m_new = jnp.maximum(m_new, jnp.where(ch, jnp.max(lg, axis=0, keepdims=True), NEG_INF))
            alpha = jnp.exp2((ms[h] - m_new) * c)
            l_new = alpha * ls[h]
            p_h = []
            for lg, ch in zip(lgs[h], chosen):
                p = jnp.exp2((lg - jnp.where(ch, m_new, -NEG_INF)) * c)
                l_new = l_new + jnp.sum(p, axis=0, keepdims=True)
                p_h.append(p.astype(BF16))
            ps.append(p_h)
            alphas.append(alpha)
            new_m.append(m_new)
            new_l.append(l_new)
        for h in range(heads):
            pv = None
            for p, n in zip(ps[h], ns):
                part = jnp.dot(vt_ref[n, lanes[h], :], p, preferred_element_type=F32)
                pv = part if pv is None else pv + part
            acc_scr[h] = alphas[h] * acc_scr[h] + pv
        return tuple(new_m), tuple(new_l)

    _, l_fin = lax.fori_loop(0, (j + group - 1) // group, past_blocks, (tuple(m0), tuple(l0)))

    for h in range(heads):
        z = z_ref[:, lanes[h]].astype(F32)
        o_ref[:, lanes[h]] = (jnp.transpose(acc_scr[h] / l_fin[h]) * _silu(z)).astype(o_ref.dtype)


MOBA_HEADS_PER_STEP = 4
MOBA_BLOCKS_PER_TRIP = 2


def _moba_prompt(main, k16, vt16, means, batch, seq):
    nblk = seq // MOBA_BLOCK
    hp = MOBA_HEADS_PER_STEP
    width = hp * HEAD_DIM
    qmap = lambda b, h, j: (b * nblk + j, COL_QA // width + h)
    zmap = lambda b, h, j: (b * nblk + j, COL_ZA // width + h)
    return pl.pallas_call(
        functools.partial(_moba_prompt_kernel, nblk=nblk, heads=hp, group=MOBA_BLOCKS_PER_TRIP),
        grid=(batch, MOBA_HEADS // hp, nblk),
        in_specs=[pl.BlockSpec((MOBA_BLOCK, width), qmap),
                  pl.BlockSpec((MOBA_BLOCK, width), zmap),
                  pl.BlockSpec((seq, width), lambda b, h, j: (b, h)),
                  pl.BlockSpec((nblk, width, MOBA_BLOCK), lambda b, h, j: (b, h, 0)),
                  pl.BlockSpec((None, nblk, width), lambda b, h, j: (b, 0, h))],
        out_specs=pl.BlockSpec((MOBA_BLOCK, width), lambda b, h, j: (b * nblk + j, h)),
        out_shape=jax.ShapeDtypeStruct((batch * seq, MOBA_WIDTH), BF16),
        scratch_shapes=[pltpu.VMEM((hp, nblk, MOBA_BLOCK), F32),
                        pltpu.VMEM((hp, HEAD_DIM, MOBA_BLOCK), F32)],
        compiler_params=_params("parallel", "parallel", "arbitrary"),
        name="moba_prompt",
    )(main, main, k16, vt16, means)


def _gla_kernel(q_ref, k_ref, v_ref, g_ref, z_ref, s0_ref, gn_ref, o_ref, sout_ref, s_scr,
                *, chunk, sub, n_valid):
    t = pl.program_id(2)

    @pl.when(t == 0)
    def _load_state():
        s_scr[...] = s0_ref[...]

    q = q_ref[...].astype(F32) * (GLA_DK ** -0.5)
    k = k_ref[...].astype(F32)
    v = v_ref[...]
    g = g_ref[...]
    if n_valid < chunk:
        live = lax.broadcasted_iota(jnp.int32, g.shape, 0) < n_valid
        g = jnp.where(live, g, 0.0)
        k = jnp.where(live, k, 0.0)

    ri = lax.broadcasted_iota(jnp.int32, (chunk, chunk), 0)
    ci = lax.broadcasted_iota(jnp.int32, (chunk, chunk), 1)
    causal = ci <= ri
    b = jnp.dot(causal.astype(F32), g, precision=lax.Precision.HIGHEST, preferred_element_type=F32)
    b_last = b[chunk - 1:chunk]

    nsub = chunk // sub
    ends = [b[(i + 1) * sub - 1:(i + 1) * sub] for i in range(nsub)]
    own_end = jnp.concatenate([jnp.broadcast_to(e, (sub, GLA_DK)) for e in ends], axis=0)
    k_rel = k * jnp.exp(own_end - b)
    row_sub = lax.broadcasted_iota(jnp.int32, (chunk, GLA_DK), 0) // sub
    q_parts, k_parts = [], []
    for i in range(nsub):
        q_parts.append(jnp.where(row_sub >= i, q * jnp.exp(b - ends[i]), 0.0).astype(BF16))
        k_parts.append(jnp.where(row_sub == i, k_rel, 0.0).astype(BF16))
    q_cat = jnp.concatenate(q_parts, axis=1) if nsub > 1 else q_parts[0]
    k_cat = jnp.concatenate(k_parts, axis=1) if nsub > 1 else k_parts[0]
    a = lax.dot_general(q_cat, k_cat, NT_DIMS, preferred_element_type=F32)
    a = jnp.where(causal, a, 0.0)

    state = s_scr[...]
    o = (jnp.dot(a.astype(BF16), v, preferred_element_type=F32)
         + jnp.dot((q * jnp.exp(b)).astype(BF16), state.astype(BF16), preferred_element_type=F32))

    k_dec = (k * jnp.exp(b_last - b)).astype(BF16)
    update = lax.dot_general(k_dec, v, TN_DIMS, preferred_element_type=F32)
    decay_t = jnp.transpose(jnp.broadcast_to(jnp.exp(b_last), (GLA_DK, GLA_DK)))
    new_state = jnp.concatenate([decay_t] * (GLA_DV // GLA_DK), axis=1) * state + update
    s_scr[...] = new_state

    r = lax.rsqrt(jnp.mean(o * o, axis=-1, keepdims=True) + RMS_EPS)
    z = z_ref[...].astype(F32)
    o_ref[...] = (o * r * gn_ref[...] * _silu(z)).astype(o_ref.dtype)

    @pl.when(t == pl.num_programs(2) - 1)
    def _store_state():
        sout_ref[...] = new_state


def _gla(main, log_a, s0, gn, batch, rows_per_seq, chunk, sub, n_valid):
    nt = rows_per_seq // chunk
    m = main.shape[0]
    rmap = lambda off: (lambda b, h, t: (b * nt + t, off + h))
    smap = lambda b, h, t: (b, h, 0, 0)
    return pl.pallas_call(
        functools.partial(_gla_kernel, chunk=chunk, sub=sub, n_valid=n_valid),
        grid=(batch, GLA_HEADS, nt),
        in_specs=[pl.BlockSpec((chunk, GLA_DK), rmap(COL_QB // GLA_DK)),
                  pl.BlockSpec((chunk, GLA_DK), rmap(COL_KB // GLA_DK)),
                  pl.BlockSpec((chunk, GLA_DV), rmap(COL_VB // GLA_DV)),
                  pl.BlockSpec((chunk, GLA_DK), rmap(0)),
                  pl.BlockSpec((chunk, GLA_DV), rmap(COL_ZB // GLA_DV)),
                  pl.BlockSpec((None, None, GLA_DK, GLA_DV), smap),
                  pl.BlockSpec((None, 1, GLA_DV), lambda b, h, t: (h, 0, 0))],
        out_specs=[pl.BlockSpec((chunk, GLA_DV), rmap(0)),
                   pl.BlockSpec((None, None, GLA_DK, GLA_DV), smap)],
        out_shape=[jax.ShapeDtypeStruct((m, GLA_V_WIDTH), BF16),
                   jax.ShapeDtypeStruct((batch, GLA_HEADS, GLA_DK, GLA_DV), F32)],
        scratch_shapes=[pltpu.VMEM((GLA_DK, GLA_DV), F32)],
        compiler_params=_params("parallel", "parallel", "arbitrary"),
        name="gla",
    )(main, main, main, log_a, main, s0, gn.reshape(GLA_HEADS, 1, GLA_DV))


PAGES_PER_STEP = 8


def _page_mean_kernel(pt_ref, *refs):
    del pt_ref
    page_refs, o_ref = refs[:-1], refs[-1]
    pages_per_block = len(page_refs) // o_ref.shape[0]
    for blk in range(o_ref.shape[0]):
        tot = None
        for pg in range(pages_per_block):
            part = jnp.sum(page_refs[blk * pages_per_block + pg][...], axis=0)
            tot = part if tot is None else tot + part
        o_ref[blk] = tot * (1.0 / MOBA_BLOCK)


def _page_means(cache_k, page_table, layer):
    dec_b, n_pages = page_table.shape
    page = cache_k.shape[2]
    pages_per_block = MOBA_BLOCK // page
    blocks_per_step = PAGES_PER_STEP // pages_per_block
    n_blocks = n_pages // pages_per_block
    page_spec = lambda i: pl.BlockSpec(
        (None, None, page, MOBA_HEADS, HEAD_DIM),
        lambda b, g, pt: (layer, pt[b, g * PAGES_PER_STEP + i], 0, 0, 0))
    return pl.pallas_call(
        _page_mean_kernel,
        grid_spec=pltpu.PrefetchScalarGridSpec(
            num_scalar_prefetch=1,
            grid=(dec_b, n_pages // PAGES_PER_STEP),
            in_specs=[page_spec(i) for i in range(PAGES_PER_STEP)],
            out_specs=pl.BlockSpec((None, blocks_per_step, MOBA_HEADS, HEAD_DIM),
                                   lambda b, g, pt: (b, g, 0, 0)),
        ),
        out_shape=jax.ShapeDtypeStruct((dec_b, n_blocks, MOBA_HEADS, HEAD_DIM), F32),
        compiler_params=_params("parallel", "arbitrary"),
        name="moba_page_means",
    )(page_table, *([cache_k] * PAGES_PER_STEP))


def _sample_select_kernel(q_ref, mean_ref, o_ref, *, n_blocks):
    q = q_ref[...]
    means = mean_ref[...].astype(BF16)
    rows = q.shape[0]
    ncol = n_blocks * MOBA_HEADS
    col = lax.broadcasted_iota(jnp.int32, (rows, ncol), 1)
    col_head = col % MOBA_HEADS
    col_f = col.astype(F32)
    s_all = jnp.zeros((rows, ncol), F32)
    for h in range(MOBA_HEADS):
        s_h = lax.dot_general(q[:, h * HEAD_DIM:(h + 1) * HEAD_DIM], means, NT_DIMS, preferred_element_type=F32)
        s_all = jnp.where(col_head == h, s_h, s_all)
    lane = lax.broadcasted_iota(jnp.int32, o_ref.shape, 1)
    out = jnp.zeros(o_ref.shape, jnp.int32)
    for h in range(MOBA_HEADS):
        s = jnp.where(col_head == h, s_all, -jnp.inf)
        for t in range(MOBA_TOPK):
            mx = jnp.max(s, axis=1, keepdims=True)
            cand = jnp.where((s == mx) & (mx > -jnp.inf), col_f, float(ncol))
            first = jnp.min(cand, axis=1, keepdims=True)
            s = jnp.where(col_f == first, -jnp.inf, s)
            blk = first.astype(jnp.int32) // MOBA_HEADS
            out = jnp.where(lane == h * MOBA_TOPK + t, blk, out)
    o_ref[...] = out


def _sample_select(main, means, dec_b):
    n_blocks = means.shape[1]
    means2 = means.reshape(dec_b, n_blocks * MOBA_HEADS, HEAD_DIM)
    return pl.pallas_call(
        functools.partial(_sample_select_kernel, n_blocks=n_blocks),
        grid=(dec_b,),
        in_specs=[pl.BlockSpec((SAMPLE_ROWS, MOBA_WIDTH), lambda b: (b, COL_QA // MOBA_WIDTH)),
                  pl.BlockSpec((None, n_blocks * MOBA_HEADS, HEAD_DIM), lambda b: (b, 0, 0))],
        out_specs=pl.BlockSpec((None, SAMPLE_ROWS, LANES), lambda b: (b, 0, 0)),
        out_shape=jax.ShapeDtypeStruct((dec_b, SAMPLE_ROWS, LANES), jnp.int32),
        compiler_params=_params("parallel"),
        name="moba_sample_select",
    )(main, means2)


def _moba_sample_kernel(pt_ref, sel_ref, q_ref, z_ref, ko_ref, vo_ref, ck_ref, cv_ref, o_ref,
                        kbuf, vbuf, sem, *, layer, n_new, page):
    b = pl.program_id(0)
    h = pl.program_id(1)
    pages_per_block = MOBA_BLOCK // page
    n_slots = n_new * MOBA_TOPK * pages_per_block

    def page_copies(slot):
        qi, rest = divmod(slot, MOBA_TOPK * pages_per_block)
        rank, pg = divmod(rest, pages_per_block)
        blk = sel_ref[b, qi * (MOBA_HEADS * MOBA_TOPK) + h * MOBA_TOPK + rank]
        pid = pt_ref[b, blk * pages_per_block + pg]
        dst = pl.ds(slot * page, page)
        return (pltpu.make_async_copy(ck_ref.at[layer, pid, :, h, :], kbuf.at[dst, :], sem.at[0]),
                pltpu.make_async_copy(cv_ref.at[layer, pid, :, h, :], vbuf.at[dst, :], sem.at[1]))

    for slot in range(n_slots):
        for c in page_copies(slot):
            c.start()
    for slot in range(n_slots):
        for c in page_copies(slot):
            c.wait()

    scale = HEAD_DIM ** -0.5
    q = q_ref[...]
    rows = q.shape[0]
    n_keys = n_slots * page
    lg = lax.dot_general(q, kbuf[...].astype(BF16), NT_DIMS, preferred_element_type=F32) * scale
    row = lax.broadcasted_iota(jnp.int32, (rows, n_keys), 0)
    key_owner = lax.broadcasted_iota(jnp.int32, (rows, n_keys), 1) // (MOBA_TOPK * MOBA_BLOCK)
    lg = jnp.where(key_owner == row, lg, NEG_INF)
    lg_own = lax.dot_general(q, ko_ref[...], NT_DIMS, preferred_element_type=F32) * scale
    r2 = lax.broadcasted_iota(jnp.int32, lg_own.shape, 0)
    c2 = lax.broadcasted_iota(jnp.int32, lg_own.shape, 1)
    lg_own = jnp.where((c2 <= r2) & (c2 < n_new), lg_own, NEG_INF)
    m = jnp.maximum(jnp.max(lg, axis=1, keepdims=True), jnp.max(lg_own, axis=1, keepdims=True))
    p = jnp.exp(lg - m)
    p_own = jnp.exp(lg_own - m)
    denom = jnp.sum(p, axis=1, keepdims=True) + jnp.sum(p_own, axis=1, keepdims=True)
    acc = (jnp.dot(p.astype(BF16), vbuf[...].astype(BF16), preferred_element_type=F32)
           + jnp.dot(p_own.astype(BF16), vo_ref[...], preferred_element_type=F32))
    z = z_ref[...].astype(F32)
    o_ref[...] = (acc / denom * _silu(z)).astype(o_ref.dtype)


def _moba_sample(main, k16, v16, cache_k, cache_v, page_table, sel, layer, n_new):
    dec_b = page_table.shape[0]
    page = cache_k.shape[2]
    n_keys = n_new * MOBA_TOPK * MOBA_BLOCK
    tile = lambda off: pl.BlockSpec((SAMPLE_ROWS, HEAD_DIM), lambda b, h, pt, sl: (b, off + h))
    return pl.pallas_call(
        functools.partial(_moba_sample_kernel, layer=layer, n_new=n_new, page=page),
        grid_spec=pltpu.PrefetchScalarGridSpec(
            num_scalar_prefetch=2,
            grid=(dec_b, MOBA_HEADS),
            in_specs=[tile(COL_QA // HEAD_DIM), tile(COL_ZA // HEAD_DIM), tile(0), tile(0),
                      pl.BlockSpec(memory_space=pl.ANY), pl.BlockSpec(memory_space=pl.ANY)],
            out_specs=tile(0),
            scratch_shapes=[pltpu.VMEM((n_keys, HEAD_DIM), F32),
                            pltpu.VMEM((n_keys, HEAD_DIM), F32),
                            pltpu.SemaphoreType.DMA((2,))],
        ),
        out_shape=jax.ShapeDtypeStruct((dec_b * SAMPLE_ROWS, MOBA_WIDTH), BF16),
        compiler_params=_params("arbitrary", "arbitrary"),
        name="moba_sample",
    )(page_table, sel, main, main, k16, v16, cache_k, cache_v)


def _main_col_block(j):
    return jnp.where(j == 0, W_QA // PROJ_TN, j + (W_ZA // PROJ_TN - 1))


def _token_proj(xn, w_in, layer, w_g2, b_gate_l, prompt):
    k32, k16 = _proj(xn, w_in, layer, lambda j: W_KA // PROJ_TN, 1, ((F32, False), (BF16, False)), "proj_k")
    v32, v16 = _proj(xn, w_in, layer, lambda j: W_VA // PROJ_TN, 1, ((F32, False), (BF16, prompt)), "proj_v")
    (main,) = _proj(xn, w_in, layer, _main_col_block, MAIN_WIDTH // PROJ_TN, ((BF16, False),), "proj_main")
    gates = _gates(xn, w_in, layer)
    log_a = _gate(xn, w_in, layer, w_g2, b_gate_l)
    return k32, k16, v32, v16, main, gates, log_a


def kernel(x_prompt, x_sample, cache_k, cache_v, state_gla, page_table, norm_g, w_in, w_gate2, b_gate,
           gla_norm_g, w_branch_a, w_branch_b, b_merge, w_out, final_norm_g):
    depth = w_in.shape[0]
    bp, seq, _ = x_prompt.shape
    dec_b, n_new, _ = x_sample.shape
    assert n_new <= SAMPLE_ROWS and seq % MOBA_BLOCK == 0
    assert cache_k.shape[2] * page_table.shape[1] % MOBA_BLOCK == 0
    assert (W_ZA // PROJ_TN, W_ZB // PROJ_TN) == (3, 6) and W_QB % PROJ_TN == 0

    xp = x_prompt.reshape(bp * seq, D_MODEL)
    xs = jnp.pad(x_sample, ((0, 0), (0, SAMPLE_ROWS - n_new), (0, 0))).reshape(dec_b * SAMPLE_ROWS, D_MODEL)
    xnp = _rmsnorm(xp, norm_g[0], BF16)
    xns = _rmsnorm(xs, norm_g[0], BF16)
    zero_state = jnp.zeros((bp, GLA_HEADS, GLA_DK, GLA_DV), F32)

    kp_l, vp_l, sp_l, ks_l, vs_l, ss_l = [], [], [], [], [], []
    for l in range(depth):
        w_g2 = jnp.pad(w_gate2[l].astype(BF16), ((0, LANES - GLA_GATE_RANK), (0, 0)))
        wa = w_branch_a[l].astype(BF16)
        wb = w_branch_b[l].astype(BF16)
        wo = w_out[l].astype(BF16)
        last = l == depth - 1
        g_next = final_norm_g if last else norm_g[l + 1]

        k32, k16, v32, vt16, main, gates, log_a = _token_proj(xnp, w_in, l, w_g2, b_gate[l], True)
        means = _block_means(k32).reshape(bp, seq // MOBA_BLOCK, MOBA_WIDTH)
        oa = _moba_prompt(main, k16, vt16, means, bp, seq)
        ob, sp = _gla(main, log_a, zero_state, gla_norm_g[l], bp, seq, MOBA_BLOCK, GLA_SUB, MOBA_BLOCK)
        merged = _merge(oa, ob, wa, wb, gates, b_merge[l])
        outs = _out_proj(merged, wo, xp, g_next, not last)
        if last:
            (yp,) = outs
        else:
            xp, xnp = outs
        kp_l.append(k32.reshape(bp, seq, MOBA_HEADS, HEAD_DIM))
        vp_l.append(v32.reshape(bp, seq, MOBA_HEADS, HEAD_DIM))
        sp_l.append(sp)

        k32, k16, v32, v16, main, gates, log_a = _token_proj(xns, w_in, l, w_g2, b_gate[l], False)
        pmeans = _page_means(cache_k, page_table, l)
        sel = _sample_select(main, pmeans, dec_b)
        sel = sel[:, :n_new, :MOBA_HEADS * MOBA_TOPK].reshape(dec_b, n_new * MOBA_HEADS * MOBA_TOPK)
        oa = _moba_sample(main, k16, v16, cache_k, cache_v, page_table, sel, l, n_new)
        ob, ss = _gla(main, log_a, state_gla[l], gla_norm_g[l], dec_b, SAMPLE_ROWS, SAMPLE_ROWS,
                      SAMPLE_ROWS, n_new)
        merged = _merge(oa, ob, wa, wb, gates, b_merge[l])
        outs = _out_proj(merged, wo, xs, g_next, not last)
        if last:
            (ys,) = outs
        else:
            xs, xns = outs
        rows = lambda a: a.reshape(dec_b, SAMPLE_ROWS, MOBA_HEADS, HEAD_DIM)[:, :n_new]
        ks_l.append(rows(k32))
        vs_l.append(rows(v32))
        ss_l.append(ss)

    y_prompt = yp.reshape(bp, seq, D_MODEL)
    y_sample = ys.reshape(dec_b, SAMPLE_ROWS, D_MODEL)[:, :n_new]
    return (y_prompt, y_sample, jnp.stack(kp_l), jnp.stack(vp_l), jnp.stack(sp_l),
            jnp.stack(ks_l), jnp.stack(vs_l), jnp.stack(ss_l))
```

```python
import functools
import math

import jax
import jax.numpy as jnp
from jax import lax
from jax.experimental import pallas as pl
from jax.experimental.pallas import tpu as pltpu

D_MODEL = 2048
MOBA_HEADS = 8
HEAD_DIM = 128
MOBA_BLOCK = 256
MOBA_TOPK = 3
GLA_HEADS = 4
GLA_DK = 128
GLA_DV = 256
GLA_GATE_RANK = 16
GLA_TAU = 16.0
GLA_SUB = 32
RMS_EPS = 1e-6
NEG_INF = -1e30
MOBA_WIDTH = MOBA_HEADS * HEAD_DIM
GLA_K_WIDTH = GLA_HEADS * GLA_DK
GLA_V_WIDTH = GLA_HEADS * GLA_DV

W_QA = 0
W_KA = W_QA + MOBA_WIDTH
W_VA = W_KA + MOBA_WIDTH
W_ZA = W_VA + MOBA_WIDTH
W_QB = W_ZA + MOBA_WIDTH
W_KB = W_QB + GLA_K_WIDTH
W_VB = W_KB + GLA_K_WIDTH
W_ZB = W_VB + GLA_V_WIDTH
W_ALR = W_ZB + GLA_V_WIDTH
W_GM = W_ALR + GLA_GATE_RANK

COL_QA = 0
COL_ZA = COL_QA + MOBA_WIDTH
COL_QB = COL_ZA + MOBA_WIDTH
COL_KB = COL_QB + GLA_K_WIDTH
COL_VB = COL_KB + GLA_K_WIDTH
COL_ZB = COL_VB + GLA_V_WIDTH
MAIN_WIDTH = COL_ZB + GLA_V_WIDTH

LANES = 128
SUBLANES = 8
SAMPLE_ROWS = SUBLANES
PROJ_TN = 1024
PROJ_TM = 1024
VMEM_LIMIT = 56 * 1024 * 1024

F32 = jnp.float32
BF16 = jnp.bfloat16
NT_DIMS = (((1,), (1,)), ((), ()))
TN_DIMS = (((0,), (0,)), ((), ()))
LOG2E = math.log2(math.e)


def _params(*sem):
    return pltpu.CompilerParams(dimension_semantics=sem, vmem_limit_bytes=VMEM_LIMIT)


def _silu(z):
    return z * (1.0 / (1.0 + jnp.exp(-z)))


def _sigmoid(z):
    return 1.0 / (1.0 + jnp.exp(-z))


def _rmsnorm_kernel(x_ref, g_ref, o_ref):
    x = x_ref[...]
    r = lax.rsqrt(jnp.mean(x * x, axis=-1, keepdims=True) + RMS_EPS)
    o_ref[...] = (x * r * g_ref[...]).astype(o_ref.dtype)


def _rmsnorm(x, g, out_dtype):
    m, d = x.shape
    tm = min(m, 512)
    return pl.pallas_call(
        _rmsnorm_kernel,
        grid=(pl.cdiv(m, tm),),
        in_specs=[pl.BlockSpec((tm, d), lambda i: (i, 0)),
                  pl.BlockSpec((1, d), lambda i: (0, 0))],
        out_specs=pl.BlockSpec((tm, d), lambda i: (i, 0)),
        out_shape=jax.ShapeDtypeStruct((m, d), out_dtype),
        compiler_params=_params("parallel"),
        name="rmsnorm",
    )(x, g.reshape(1, d))


def _proj_kernel(x_ref, w_ref, *refs, transposed):
    *o_refs, w_bf = refs

    @pl.when(pl.program_id(1) == 0)
    def _cast_weight_tile():
        w_bf[...] = w_ref[...].astype(BF16)

    acc = lax.dot_general(x_ref[...], w_bf[...], NT_DIMS, preferred_element_type=F32)
    for o_ref, tr in zip(o_refs, transposed):
        if tr:
            for r in range(o_ref.shape[0]):
                rows = acc[r * MOBA_BLOCK:(r + 1) * MOBA_BLOCK, :]
                o_ref[r] = jnp.transpose(rows).astype(o_ref.dtype)
        else:
            o_ref[...] = acc.astype(o_ref.dtype)


def _proj(xn, w_t, layer, col_block, n_blocks, outs, name):
    m, k = xn.shape
    tm = min(m, PROJ_TM)
    tn = PROJ_TN
    n = n_blocks * tn
    out_specs, out_shape = [], []
    for dt, tr in outs:
        if tr:
            out_specs.append(pl.BlockSpec((tm // MOBA_BLOCK, tn, MOBA_BLOCK), lambda j, i: (i, j, 0)))
            out_shape.append(jax.ShapeDtypeStruct((m // MOBA_BLOCK, n, MOBA_BLOCK), dt))
        else:
            out_specs.append(pl.BlockSpec((tm, tn), lambda j, i: (i, j)))
            out_shape.append(jax.ShapeDtypeStruct((m, n), dt))
    return pl.pallas_call(
        functools.partial(_proj_kernel, transposed=tuple(tr for _, tr in outs)),
        grid=(n_blocks, pl.cdiv(m, tm)),
        in_specs=[pl.BlockSpec((tm, k), lambda j, i: (i, 0)),
                  pl.BlockSpec((None, tn, k), lambda j, i: (layer, col_block(j), 0))],
        out_specs=out_specs,
        out_shape=out_shape,
        scratch_shapes=[pltpu.VMEM((tn, k), BF16)],
        compiler_params=_params("parallel", "arbitrary"),
        name=name,
    )(xn, w_t)


def _gates_kernel(x_ref, wa_ref, wb_ref, o_ref, w_bf, *, shift):
    @pl.when(pl.program_id(1) == 0)
    def _cast_weight_tile():
        w = jnp.concatenate([wa_ref[shift:, :], wb_ref[...]], axis=0)
        w_bf[...] = w.astype(BF16)

    o_ref[...] = lax.dot_general(x_ref[...], w_bf[...], NT_DIMS, preferred_element_type=F32).astype(o_ref.dtype)


def _gates(xn, w_t, layer):
    m, k = xn.shape
    tm = min(m, PROJ_TM)
    tn = PROJ_TN
    base = W_GM // tn * tn
    shift = W_GM - base
    assert shift % SUBLANES == 0 and tn % shift == 0
    return pl.pallas_call(
        functools.partial(_gates_kernel, shift=shift),
        grid=(2 * D_MODEL // tn, pl.cdiv(m, tm)),
        in_specs=[pl.BlockSpec((tm, k), lambda j, i: (i, 0)),
                  pl.BlockSpec((None, tn, k), lambda j, i: (layer, base // tn + j, 0)),
                  pl.BlockSpec((None, shift, k), lambda j, i: (layer, (base + (j + 1) * tn) // shift, 0))],
        out_specs=pl.BlockSpec((tm, tn), lambda j, i: (i, j)),
        out_shape=jax.ShapeDtypeStruct((m, 2 * D_MODEL), BF16),
        scratch_shapes=[pltpu.VMEM((tn, k), BF16)],
        compiler_params=_params("parallel", "arbitrary"),
        name="proj_gates",
    )(xn, w_t, w_t)


def _gate_kernel(x_ref, w1_ref, w2_ref, b_ref, o_ref):
    a = lax.dot_general(x_ref[...], w1_ref[...].astype(BF16), NT_DIMS, preferred_element_type=F32).astype(BF16)
    y = jnp.dot(a, w2_ref[...], preferred_element_type=F32) + b_ref[...]
    log_sig = jnp.minimum(y, 0.0) - jnp.log(1.0 + jnp.exp(-jnp.abs(y)))
    o_ref[...] = log_sig * (1.0 / GLA_TAU)


def _gate(xn, w_t, layer, w2, b):
    m, k = xn.shape
    tm = min(m, 512)
    assert W_ALR % LANES == 0
    return pl.pallas_call(
        _gate_kernel,
        grid=(pl.cdiv(m, tm),),
        in_specs=[pl.BlockSpec((tm, k), lambda i: (i, 0)),
                  pl.BlockSpec((None, LANES, k), lambda i: (layer, W_ALR // LANES, 0)),
                  pl.BlockSpec((LANES, GLA_K_WIDTH), lambda i: (0, 0)),
                  pl.BlockSpec((1, GLA_K_WIDTH), lambda i: (0, 0))],
        out_specs=pl.BlockSpec((tm, GLA_K_WIDTH), lambda i: (i, 0)),
        out_shape=jax.ShapeDtypeStruct((m, GLA_K_WIDTH), F32),
        compiler_params=_params("parallel"),
        name="gla_gate",
    )(xn, w_t, w2, b.reshape(1, GLA_K_WIDTH))


def _merge_kernel(oa_ref, ob_ref, wa_ref, wb_ref, ga_ref, gb_ref, ba_ref, bb_ref, o_ref):
    ya = jnp.dot(oa_ref[...], wa_ref[...], preferred_element_type=F32)
    yb = jnp.dot(ob_ref[...], wb_ref[...], preferred_element_type=F32)
    ga = _sigmoid(ga_ref[...].astype(F32) + ba_ref[...])
    gb = _sigmoid(gb_ref[...].astype(F32) + bb_ref[...])
    o_ref[...] = (ga * ya + gb * yb).astype(o_ref.dtype)


def _merge(oa, ob, wa, wb, gates, b_merge):
    m = oa.shape[0]
    tm = min(m, 1024)
    tn = 1024
    nj = D_MODEL // tn
    bm = b_merge.reshape(1, 2 * D_MODEL)
    return pl.pallas_call(
        _merge_kernel,
        grid=(nj, pl.cdiv(m, tm)),
        in_specs=[pl.BlockSpec((tm, MOBA_WIDTH), lambda j, i: (i, 0)),
                  pl.BlockSpec((tm, GLA_V_WIDTH), lambda j, i: (i, 0)),
                  pl.BlockSpec((MOBA_WIDTH, tn), lambda j, i: (0, j)),
                  pl.BlockSpec((GLA_V_WIDTH, tn), lambda j, i: (0, j)),
                  pl.BlockSpec((tm, tn), lambda j, i: (i, j)),
                  pl.BlockSpec((tm, tn), lambda j, i: (i, nj + j)),
                  pl.BlockSpec((1, tn), lambda j, i: (0, j)),
                  pl.BlockSpec((1, tn), lambda j, i: (0, nj + j))],
        out_specs=pl.BlockSpec((tm, tn), lambda j, i: (i, j)),
        out_shape=jax.ShapeDtypeStruct((m, D_MODEL), BF16),
        compiler_params=_params("parallel", "parallel"),
        name="gated_merge",
    )(oa, ob, wa, wb, gates, gates, bm, bm)


def _out_kernel(m_ref, w_ref, x_ref, g_ref, *o_refs, emit_x):
    x_new = x_ref[...] + jnp.dot(m_ref[...], w_ref[...], preferred_element_type=F32)
    r = lax.rsqrt(jnp.mean(x_new * x_new, axis=-1, keepdims=True) + RMS_EPS)
    xn = x_new * r * g_ref[...]
    if emit_x:
        o_refs[0][...] = x_new
        o_refs[1][...] = xn.astype(o_refs[1].dtype)
    else:
        o_refs[0][...] = xn.astype(o_refs[0].dtype)


def _out_proj(merged, w_out, x, g_next, emit_x):
    m = x.shape[0]
    tm = min(m, 512)
    row = pl.BlockSpec((tm, D_MODEL), lambda i: (i, 0))
    if emit_x:
        out_specs = [row, row]
        out_shape = [jax.ShapeDtypeStruct((m, D_MODEL), F32),
                     jax.ShapeDtypeStruct((m, D_MODEL), BF16)]
    else:
        out_specs = [row]
        out_shape = [jax.ShapeDtypeStruct((m, D_MODEL), F32)]
    return pl.pallas_call(
        functools.partial(_out_kernel, emit_x=emit_x),
        grid=(pl.cdiv(m, tm),),
        in_specs=[row,
                  pl.BlockSpec((D_MODEL, D_MODEL), lambda i: (0, 0)),
                  row,
                  pl.BlockSpec((1, D_MODEL), lambda i: (0, 0))],
        out_specs=out_specs,
        out_shape=out_shape,
        compiler_params=_params("parallel"),
        name="out_proj",
    )(merged, w_out, x, g_next.reshape(1, D_MODEL))


def _block_mean_kernel(k_ref, o_ref):
    o_ref[...] = jnp.sum(k_ref[...], axis=0, keepdims=True) * (1.0 / MOBA_BLOCK)


def _block_means(k32):
    nblk = k32.shape[0] // MOBA_BLOCK
    return pl.pallas_call(
        _block_mean_kernel,
        grid=(nblk,),
        in_specs=[pl.BlockSpec((MOBA_BLOCK, MOBA_WIDTH), lambda i: (i, 0))],
        out_specs=pl.BlockSpec((None, 1, MOBA_WIDTH), lambda i: (i, 0, 0)),
        out_shape=jax.ShapeDtypeStruct((nblk, 1, MOBA_WIDTH), F32),
        compiler_params=_params("parallel"),
        name="moba_block_means",
    )(k32)


def _top_k_mask(s, idx, n_cand, axis):
    sel = jnp.zeros(s.shape, F32)
    for _ in range(MOBA_TOPK):
        mx = jnp.max(s, axis=axis, keepdims=True)
        cand = jnp.where((s == mx) & (mx > -jnp.inf), idx, float(n_cand))
        first = jnp.min(cand, axis=axis, keepdims=True)
        pick = idx == first
        sel = jnp.where(pick, 1.0, sel)
        s = jnp.where(pick, -jnp.inf, s)
    return sel


def _moba_prompt_kernel(q_ref, z_ref, k_ref, vt_ref, mean_ref, o_ref, sel_scr, acc_scr, *, nblk, heads, group):
    j = pl.program_id(2)
    c = (HEAD_DIM ** -0.5) * LOG2E
    lanes = [slice(h * HEAD_DIM, (h + 1) * HEAD_DIM) for h in range(heads)]
    qs = [q_ref[:, ln] for ln in lanes]

    def logits(h, n):
        r = pl.multiple_of(n * MOBA_BLOCK, MOBA_BLOCK)
        return lax.dot_general(k_ref[pl.ds(r, MOBA_BLOCK), lanes[h]], qs[h], NT_DIMS, preferred_element_type=F32)

    m0, l0 = [], []
    for h in range(heads):
        s = lax.dot_general(mean_ref[:, lanes[h]].astype(BF16), qs[h], NT_DIMS, preferred_element_type=F32)
        blk = lax.broadcasted_iota(jnp.int32, s.shape, 0)
        s = jnp.where(blk < j, s, -jnp.inf)
        sel_scr[h] = _top_k_mask(s, blk.astype(F32), nblk, 0)

        lg = logits(h, j)
        key = lax.broadcasted_iota(jnp.int32, lg.shape, 0)
        qry = lax.broadcasted_iota(jnp.int32, lg.shape, 1)
        lg = jnp.where(key <= qry, lg, NEG_INF)
        m = jnp.max(lg, axis=0, keepdims=True)
        p = jnp.exp2((lg - m) * c)
        m0.append(m)
        l0.append(jnp.sum(p, axis=0, keepdims=True))
        acc_scr[h] = jnp.dot(vt_ref[j, lanes[h], :], p.astype(BF16), preferred_element_type=F32)

    def past_blocks(i, carry):
        ms, ls = carry
        ns = [jnp.minimum(group * i + g, nblk - 1) for g in range(group)]
        lgs = [[logits(h, n) for n in ns] for h in range(heads)]
        new_m, new_l, alphas, ps = [], [], [], []
        for h in range(heads):
            chosen = [(sel_scr[h, pl.ds(n, 1), :] > 0.0) & (group * i + g < j) for g, n in enumerate(ns)]
            m_new = ms[h]
            for lg, ch in zip(lgs[h], chosen):
                m_new = jnp.maximum(m_new, jnp.where(ch, jnp.max(lg, axis=0, keepdims=True), NEG_INF))
            alpha = jnp.exp2((ms[h] - m_new) * c)
            l_new = alpha * ls[h]
            p_h = []
            for lg, ch in zip(lgs[h], chosen):
                p = jnp.exp2((lg - jnp.where(ch, m_new, -NEG_INF)) * c)
                l_new = l_new + jnp.sum(p, axis=0, keepdims=True)
                p_h.append(p.astype(BF16))
            ps.append(p_h)
            alphas.append(alpha)
            new_m.append(m_new)
            new_l.append(l_new)
        for h in range(heads):
            pv = None
            for p, n in zip(ps[h], ns):
                part = jnp.dot(vt_ref[n, lanes[h], :], p, preferred_element_type=F32)
                pv = part if pv is None else pv + part
            acc_scr[h] = alphas[h] * acc_scr[h] + pv
        return tuple(new_m), tuple(new_l)

    _, l_fin = lax.fori_loop(0, (j + group - 1) // group, past_blocks, (tuple(m0), tuple(l0)))

    for h in range(heads):
        z = z_ref[:, lanes[h]].astype(F32)
        o_ref[:, lanes[h]] = (jnp.transpose(acc_scr[h] / l_fin[h]) * _silu(z)).astype(o_ref.dtype)


MOBA_HEADS_PER_STEP = 4
MOBA_BLOCKS_PER_TRIP = 2


def _moba_prompt(main, k16, vt16, means, batch, seq):
    nblk = seq // MOBA_BLOCK
    hp = MOBA_HEADS_PER_STEP
    width = hp * HEAD_DIM
    qmap = lambda b, h, j: (b * nblk + j, COL_QA // width + h)
    zmap = lambda b, h, j: (b * nblk + j, COL_ZA // width + h)
    return pl.pallas_call(
        functools.partial(_moba_prompt_kernel, nblk=nblk, heads=hp, group=MOBA_BLOCKS_PER_TRIP),
        grid=(batch, MOBA_HEADS // hp, nblk),
        in_specs=[pl.BlockSpec((MOBA_BLOCK, width), qmap),
                  pl.BlockSpec((MOBA_BLOCK, width), zmap),
                  pl.BlockSpec((seq, width), lambda b, h, j: (b, h)),
                  pl.BlockSpec((nblk, width, MOBA_BLOCK), lambda b, h, j: (b, h, 0)),
                  pl.BlockSpec((None, nblk, width), lambda b, h, j: (b, 0, h))],
        out_specs=pl.BlockSpec((MOBA_BLOCK, width), lambda b, h, j: (b * nblk + j, h)),
        out_shape=jax.ShapeDtypeStruct((batch * seq, MOBA_WIDTH), BF16),
        scratch_shapes=[pltpu.VMEM((hp, nblk, MOBA_BLOCK), F32),
                        pltpu.VMEM((hp, HEAD_DIM, MOBA_BLOCK), F32)],
        compiler_params=_params("parallel", "parallel", "arbitrary"),
        name="moba_prompt",
    )(main, main, k16, vt16, means)


def _gla_kernel(q_ref, k_ref, v_ref, g_ref, z_ref, s0_ref, gn_ref, o_ref, sout_ref, s_scr,
                *, chunk, sub, n_valid):
    t = pl.program_id(2)

    @pl.when(t == 0)
    def _load_state():
        s_scr[...] = s0_ref[...]

    q = q_ref[...].astype(F32) * (GLA_DK ** -0.5)
    k = k_ref[...].astype(F32)
    v = v_ref[...]
    g = g_ref[...]
    if n_valid < chunk:
        live = lax.broadcasted_iota(jnp.int32, g.shape, 0) < n_valid
        g = jnp.where(live, g, 0.0)
        k = jnp.where(live, k, 0.0)

    ri = lax.broadcasted_iota(jnp.int32, (chunk, chunk), 0)
    ci = lax.broadcasted_iota(jnp.int32, (chunk, chunk), 1)
    causal = ci <= ri
    g_hi = g.astype(BF16)
    g_r1 = g - g_hi.astype(F32)
    g_mid = g_r1.astype(BF16)
    g_lo = (g_r1 - g_mid.astype(F32)).astype(BF16)
    tri = jnp.where(causal, 1.0, 0.0).astype(BF16)
    b3 = jnp.dot(tri, jnp.concatenate([g_hi, g_mid, g_lo], axis=1), preferred_element_type=F32)
    b = (b3[:, :GLA_DK] + b3[:, GLA_DK:2 * GLA_DK]) + b3[:, 2 * GLA_DK:]
    b_last = b[chunk - 1:chunk]

    nsub = chunk // sub
    ends = [b[(i + 1) * sub - 1:(i + 1) * sub] for i in range(nsub)]
    own_end = jnp.concatenate([jnp.broadcast_to(e, (sub, GLA_DK)) for e in ends], axis=0)
    k_rel = k * jnp.exp(own_end - b)
    row_sub = lax.broadcasted_iota(jnp.int32, (chunk, GLA_DK), 0) // sub
    q_parts, k_parts = [], []
    for i in range(nsub):
        q_parts.append(jnp.where(row_sub >= i, q * jnp.exp(b - ends[i]), 0.0).astype(BF16))
        k_parts.append(jnp.where(row_sub == i, k_rel, 0.0).astype(BF16))
    q_cat = jnp.concatenate(q_parts, axis=1) if nsub > 1 else q_parts[0]
    k_cat = jnp.concatenate(k_parts, axis=1) if nsub > 1 else k_parts[0]
    a = lax.dot_general(q_cat, k_cat, NT_DIMS, preferred_element_type=F32)
    a = jnp.where(causal, a, 0.0)

    state = s_scr[...]
    o = (jnp.dot(a.astype(BF16), v, preferred_element_type=F32)
         + jnp.dot((q * jnp.exp(b)).astype(BF16), state.astype(BF16), preferred_element_type=F32))

    k_dec = (k * jnp.exp(b_last - b)).astype(BF16)
    update = lax.dot_general(k_dec, v, TN_DIMS, preferred_element_type=F32)
    decay_t = jnp.transpose(jnp.broadcast_to(jnp.exp(b_last), (GLA_DK, GLA_DK)))
    new_state = jnp.concatenate([decay_t] * (GLA_DV // GLA_DK), axis=1) * state + update
    s_scr[...] = new_state

    r = lax.rsqrt(jnp.mean(o * o, axis=-1, keepdims=True) + RMS_EPS)
    z = z_ref[...].astype(F32)
    o_ref[...] = (o * r * gn_ref[...] * _silu(z)).astype(o_ref.dtype)

    @pl.when(t == pl.num_programs(2) - 1)
    def _store_state():
        sout_ref[...] = new_state


def _gla(main, log_a, s0, gn, batch, rows_per_seq, chunk, sub, n_valid):
    nt = rows_per_seq // chunk
    m = main.shape[0]
    rmap = lambda off: (lambda b, h, t: (b * nt + t, off + h))
    smap = lambda b, h, t: (b, h, 0, 0)
    return pl.pallas_call(
        functools.partial(_gla_kernel, chunk=chunk, sub=sub, n_valid=n_valid),
        grid=(batch, GLA_HEADS, nt),
        in_specs=[pl.BlockSpec((chunk, GLA_DK), rmap(COL_QB // GLA_DK)),
                  pl.BlockSpec((chunk, GLA_DK), rmap(COL_KB // GLA_DK)),
                  pl.BlockSpec((chunk, GLA_DV), rmap(COL_VB // GLA_DV)),
                  pl.BlockSpec((chunk, GLA_DK), rmap(0)),
                  pl.BlockSpec((chunk, GLA_DV), rmap(COL_ZB // GLA_DV)),
                  pl.BlockSpec((None, None, GLA_DK, GLA_DV), smap),
                  pl.BlockSpec((None, 1, GLA_DV), lambda b, h, t: (h, 0, 0))],
        out_specs=[pl.BlockSpec((chunk, GLA_DV), rmap(0)),
                   pl.BlockSpec((None, None, GLA_DK, GLA_DV), smap)],
        out_shape=[jax.ShapeDtypeStruct((m, GLA_V_WIDTH), BF16),
                   jax.ShapeDtypeStruct((batch, GLA_HEADS, GLA_DK, GLA_DV), F32)],
        scratch_shapes=[pltpu.VMEM((GLA_DK, GLA_DV), F32)],
        compiler_params=_params("parallel", "parallel", "arbitrary"),
        name="gla",
    )(main, main, main, log_a, main, s0, gn.reshape(GLA_HEADS, 1, GLA_DV))


PAGES_PER_STEP = 8


def _page_mean_kernel(pt_ref, *refs):
    del pt_ref
    page_refs, o_ref = refs[:-1], refs[-1]
    pages_per_block = len(page_refs) // o_ref.shape[0]
    for blk in range(o_ref.shape[0]):
        tot = None
        for pg in range(pages_per_block):
            part = jnp.sum(page_refs[blk * pages_per_block + pg][...], axis=0)
            tot = part if tot is None else tot + part
        o_ref[blk] = tot * (1.0 / MOBA_BLOCK)


def _page_means(cache_k, page_table, layer):
    dec_b, n_pages = page_table.shape
    page = cache_k.shape[2]
    pages_per_block = MOBA_BLOCK // page
    blocks_per_step = PAGES_PER_STEP // pages_per_block
    n_blocks = n_pages // pages_per_block
    page_spec = lambda i: pl.BlockSpec(
        (None, None, page, MOBA_HEADS, HEAD_DIM),
        lambda b, g, pt: (layer, pt[b, g * PAGES_PER_STEP + i], 0, 0, 0))
    return pl.pallas_call(
        _page_mean_kernel,
        grid_spec=pltpu.PrefetchScalarGridSpec(
            num_scalar_prefetch=1,
            grid=(dec_b, n_pages // PAGES_PER_STEP),
            in_specs=[page_spec(i) for i in range(PAGES_PER_STEP)],
            out_specs=pl.BlockSpec((None, blocks_per_step, MOBA_HEADS, HEAD_DIM),
                                   lambda b, g, pt: (b, g, 0, 0)),
        ),
        out_shape=jax.ShapeDtypeStruct((dec_b, n_blocks, MOBA_HEADS, HEAD_DIM), F32),
        compiler_params=_params("parallel", "arbitrary"),
        name="moba_page_means",
    )(page_table, *([cache_k] * PAGES_PER_STEP))


def _sample_select_kernel(q_ref, mean_ref, o_ref, *, n_blocks):
    q = q_ref[...]
    means = mean_ref[...].astype(BF16)
    rows = q.shape[0]
    ncol = n_blocks * MOBA_HEADS
    col = lax.broadcasted_iota(jnp.int32, (rows, ncol), 1)
    col_head = col % MOBA_HEADS
    col_f = col.astype(F32)
    s_all = jnp.zeros((rows, ncol), F32)
    for h in range(MOBA_HEADS):
        s_h = lax.dot_general(q[:, h * HEAD_DIM:(h + 1) * HEAD_DIM], means, NT_DIMS, preferred_element_type=F32)
        s_all = jnp.where(col_head == h, s_h, s_all)
    lane = lax.broadcasted_iota(jnp.int32, o_ref.shape, 1)
    out = jnp.zeros(o_ref.shape, jnp.int32)
    for h in range(MOBA_HEADS):
        s = jnp.where(col_head == h, s_all, -jnp.inf)
        for t in range(MOBA_TOPK):
            mx = jnp.max(s, axis=1, keepdims=True)
            cand = jnp.where((s == mx) & (mx > -jnp.inf), col_f, float(ncol))
            first = jnp.min(cand, axis=1, keepdims=True)
            s = jnp.where(col_f == first, -jnp.inf, s)
            blk = first.astype(jnp.int32) // MOBA_HEADS
            out = jnp.where(lane == h * MOBA_TOPK + t, blk, out)
    o_ref[...] = out


def _sample_select(main, means, dec_b):
    n_blocks = means.shape[1]
    means2 = means.reshape(dec_b, n_blocks * MOBA_HEADS, HEAD_DIM)
    return pl.pallas_call(
        functools.partial(_sample_select_kernel, n_blocks=n_blocks),
        grid=(dec_b,),
        in_specs=[pl.BlockSpec((SAMPLE_ROWS, MOBA_WIDTH), lambda b: (b, COL_QA // MOBA_WIDTH)),
                  pl.BlockSpec((None, n_blocks * MOBA_HEADS, HEAD_DIM), lambda b: (b, 0, 0))],
        out_specs=pl.BlockSpec((None, SAMPLE_ROWS, LANES), lambda b: (b, 0, 0)),
        out_shape=jax.ShapeDtypeStruct((dec_b, SAMPLE_ROWS, LANES), jnp.int32),
        compiler_params=_params("parallel"),
        name="moba_sample_select",
    )(main, means2)


def _moba_sample_kernel(pt_ref, sel_ref, q_ref, z_ref, ko_ref, vo_ref, ck_ref, cv_ref, o_ref,
                        kbuf, vbuf, sem, *, layer, n_new, page):
    b = pl.program_id(0)
    h = pl.program_id(1)
    pages_per_block = MOBA_BLOCK // page
    n_slots = n_new * MOBA_TOPK * pages_per_block

    def page_copies(slot):
        qi, rest = divmod(slot, MOBA_TOPK * pages_per_block)
        rank, pg = divmod(rest, pages_per_block)
        blk = sel_ref[b, qi * (MOBA_HEADS * MOBA_TOPK) + h * MOBA_TOPK + rank]
        pid = pt_ref[b, blk * pages_per_block + pg]
        dst = pl.ds(slot * page, page)
        return (pltpu.make_async_copy(ck_ref.at[layer, pid, :, h, :], kbuf.at[dst, :], sem.at[0]),
                pltpu.make_async_copy(cv_ref.at[layer, pid, :, h, :], vbuf.at[dst, :], sem.at[1]))

    for slot in range(n_slots):
        for c in page_copies(slot):
            c.start()
    for slot in range(n_slots):
        for c in page_copies(slot):
            c.wait()

    scale = HEAD_DIM ** -0.5
    q = q_ref[...]
    rows = q.shape[0]
    n_keys = n_slots * page
    lg = lax.dot_general(q, kbuf[...].astype(BF16), NT_DIMS, preferred_element_type=F32) * scale
    row = lax.broadcasted_iota(jnp.int32, (rows, n_keys), 0)
    key_owner = lax.broadcasted_iota(jnp.int32, (rows, n_keys), 1) // (MOBA_TOPK * MOBA_BLOCK)
    lg = jnp.where(key_owner == row, lg, NEG_INF)
    lg_own = lax.dot_general(q, ko_ref[...], NT_DIMS, preferred_element_type=F32) * scale
    r2 = lax.broadcasted_iota(jnp.int32, lg_own.shape, 0)
    c2 = lax.broadcasted_iota(jnp.int32, lg_own.shape, 1)
    lg_own = jnp.where((c2 <= r2) & (c2 < n_new), lg_own, NEG_INF)
    m = jnp.maximum(jnp.max(lg, axis=1, keepdims=True), jnp.max(lg_own, axis=1, keepdims=True))
    p = jnp.exp(lg - m)
    p_own = jnp.exp(lg_own - m)
    denom = jnp.sum(p, axis=1, keepdims=True) + jnp.sum(p_own, axis=1, keepdims=True)
    acc = (jnp.dot(p.astype(BF16), vbuf[...].astype(BF16), preferred_element_type=F32)
           + jnp.dot(p_own.astype(BF16), vo_ref[...], preferred_element_type=F32))
    z = z_ref[...].astype(F32)
    o_ref[...] = (acc / denom * _silu(z)).astype(o_ref.dtype)


def _moba_sample(main, k16, v16, cache_k, cache_v, page_table, sel, layer, n_new):
    dec_b = page_table.shape[0]
    page = cache_k.shape[2]
    n_keys = n_new * MOBA_TOPK * MOBA_BLOCK
    tile = lambda off: pl.BlockSpec((SAMPLE_ROWS, HEAD_DIM), lambda b, h, pt, sl: (b, off + h))
    return pl.pallas_call(
        functools.partial(_moba_sample_kernel, layer=layer, n_new=n_new, page=page),
        grid_spec=pltpu.PrefetchScalarGridSpec(
            num_scalar_prefetch=2,
            grid=(dec_b, MOBA_HEADS),
            in_specs=[tile(COL_QA // HEAD_DIM), tile(COL_ZA // HEAD_DIM), tile(0), tile(0),
                      pl.BlockSpec(memory_space=pl.ANY), pl.BlockSpec(memory_space=pl.ANY)],
            out_specs=tile(0),
            scratch_shapes=[pltpu.VMEM((n_keys, HEAD_DIM), F32),
                            pltpu.VMEM((n_keys, HEAD_DIM), F32),
                            pltpu.SemaphoreType.DMA((2,))],
        ),
        out_shape=jax.ShapeDtypeStruct((dec_b * SAMPLE_ROWS, MOBA_WIDTH), BF16),
        compiler_params=_params("arbitrary", "arbitrary"),
        name="moba_sample",
    )(page_table, sel, main, main, k16, v16, cache_k, cache_v)


def _main_col_block(j):
    return jnp.where(j == 0, W_QA // PROJ_TN, j + (W_ZA // PROJ_TN - 1))


def _token_proj(xn, w_t, layer, w_g2, b_gate_l, prompt):
    k32, k16 = _proj(xn, w_t, layer, lambda j: W_KA // PROJ_TN, 1, ((F32, False), (BF16, False)), "proj_k")
    v32, v16 = _proj(xn, w_t, layer, lambda j: W_VA // PROJ_TN, 1, ((F32, False), (BF16, prompt)), "proj_v")
    (main,) = _proj(xn, w_t, layer, _main_col_block, MAIN_WIDTH // PROJ_TN, ((BF16, False),), "proj_main")
    gates = _gates(xn, w_t, layer)
    log_a = _gate(xn, w_t, layer, w_g2, b_gate_l)
    return k32, k16, v32, v16, main, gates, log_a


def kernel(x_prompt, x_sample, cache_k, cache_v, state_gla, page_table, norm_g, w_in, w_gate2, b_gate,
           gla_norm_g, w_branch_a, w_branch_b, b_merge, w_out, final_norm_g):
    depth = w_in.shape[0]
    bp, seq, _ = x_prompt.shape
    dec_b, n_new, _ = x_sample.shape
    assert n_new <= SAMPLE_ROWS and seq % MOBA_BLOCK == 0
    assert cache_k.shape[2] * page_table.shape[1] % MOBA_BLOCK == 0
    assert (W_ZA // PROJ_TN, W_ZB // PROJ_TN) == (3, 6) and W_QB % PROJ_TN == 0

    xp = x_prompt.reshape(bp * seq, D_MODEL)
    xs = jnp.pad(x_sample, ((0, 0), (0, SAMPLE_ROWS - n_new), (0, 0))).reshape(dec_b * SAMPLE_ROWS, D_MODEL)
    xnp = _rmsnorm(xp, norm_g[0], BF16)
    xns = _rmsnorm(xs, norm_g[0], BF16)
    zero_state = jnp.zeros((bp, GLA_HEADS, GLA_DK, GLA_DV), F32)
    w_t = jnp.swapaxes(w_in, 1, 2)

    kp_l, vp_l, sp_l, ks_l, vs_l, ss_l = [], [], [], [], [], []
    for l in range(depth):
        w_g2 = jnp.pad(w_gate2[l].astype(BF16), ((0, LANES - GLA_GATE_RANK), (0, 0)))
        wa = w_branch_a[l].astype(BF16)
        wb = w_branch_b[l].astype(BF16)
        wo = w_out[l].astype(BF16)
        last = l == depth - 1
        g_next = final_norm_g if last else norm_g[l + 1]

        k32, k16, v32, vt16, main, gates, log_a = _token_proj(xnp, w_t, l, w_g2, b_gate[l], True)
        means = _block_means(k32).reshape(bp, seq // MOBA_BLOCK, MOBA_WIDTH)
        oa = _moba_prompt(main, k16, vt16, means, bp, seq)
        ob, sp = _gla(main, log_a, zero_state, gla_norm_g[l], bp, seq, MOBA_BLOCK, GLA_SUB, MOBA_BLOCK)
        merged = _merge(oa, ob, wa, wb, gates, b_merge[l])
        outs = _out_proj(merged, wo, xp, g_next, not last)
        if last:
            (yp,) = outs
        else:
            xp, xnp = outs
        kp_l.append(k32.reshape(bp, seq, MOBA_HEADS, HEAD_DIM))
        vp_l.append(v32.reshape(bp, seq, MOBA_HEADS, HEAD_DIM))
        sp_l.append(sp)

        k32, k16, v32, v16, main, gates, log_a = _token_proj(xns, w_t, l, w_g2, b_gate[l], False)
        pmeans = _page_means(cache_k, page_table, l)
        sel = _sample_select(main, pmeans, dec_b)
        sel = sel[:, :n_new, :MOBA_HEADS * MOBA_TOPK].reshape(dec_b, n_new * MOBA_HEADS * MOBA_TOPK)
        oa = _moba_sample(main, k16, v16, cache_k, cache_v, page_table, sel, l, n_new)
        ob, ss = _gla(main, log_a, state_gla[l], gla_norm_g[l], dec_b, SAMPLE_ROWS, SAMPLE_ROWS,
                      SAMPLE_ROWS, n_new)
        merged = _merge(oa, ob, wa, wb, gates, b_merge[l])
        outs = _out_proj(merged, wo, xs, g_next, not last)
        if last:
            (ys,) = outs
        else:
            xs, xns = outs
        rows = lambda a: a.reshape(dec_b, SAMPLE_ROWS, MOBA_HEADS, HEAD_DIM)[:, :n_new]
        ks_l.append(rows(k32))
        vs_l.append(rows(v32))
        ss_l.append(ss)

    y_prompt = yp.reshape(bp, seq, D_MODEL)
    y_sample = ys.reshape(dec_b, SAMPLE_ROWS, D_MODEL)[:, :n_new]
    return (y_prompt, y_sample, jnp.stack(kp_l), jnp.stack(vp_l), jnp.stack(sp_l),
            jnp.stack(ks_l), jnp.stack(vs_l), jnp.stack(ss_l))
```

```python
import functools
import math

import jax
import jax.numpy as jnp
from jax import lax
from jax.experimental import pallas as pl
from jax.experimental.pallas import tpu as pltpu

D_MODEL = 2048
MOBA_HEADS = 8
HEAD_DIM = 128
MOBA_BLOCK = 256
MOBA_TOPK = 3
GLA_HEADS = 4
GLA_DK = 128
GLA_DV = 256
GLA_GATE_RANK = 16
GLA_TAU = 16.0
GLA_SUB = 32
RMS_EPS = 1e-6
NEG_INF = -1e30
MOBA_WIDTH = MOBA_HEADS * HEAD_DIM
GLA_K_WIDTH = GLA_HEADS * GLA_DK
GLA_V_WIDTH = GLA_HEADS * GLA_DV

W_QA = 0
W_KA = W_QA + MOBA_WIDTH
W_VA = W_KA + MOBA_WIDTH
W_ZA = W_VA + MOBA_WIDTH
W_QB = W_ZA + MOBA_WIDTH
W_KB = W_QB + GLA_K_WIDTH
W_VB = W_KB + GLA_K_WIDTH
W_ZB = W_VB + GLA_V_WIDTH
W_ALR = W_ZB + GLA_V_WIDTH
W_GM = W_ALR + GLA_GATE_RANK

COL_QA = 0
COL_ZA = COL_QA + MOBA_WIDTH
COL_QB = COL_ZA + MOBA_WIDTH
COL_KB = COL_QB + GLA_K_WIDTH
COL_VB = COL_KB + GLA_K_WIDTH
COL_ZB = COL_VB + GLA_V_WIDTH
MAIN_WIDTH = COL_ZB + GLA_V_WIDTH

LANES = 128
SUBLANES = 8
SAMPLE_ROWS = SUBLANES
PROJ_TN = 1024
PROJ_TM = 1024
VMEM_LIMIT = 56 * 1024 * 1024

F32 = jnp.float32
BF16 = jnp.bfloat16
NT_DIMS = (((1,), (1,)), ((), ()))
TN_DIMS = (((0,), (0,)), ((), ()))
LOG2E = math.log2(math.e)


def _params(*sem):
    return pltpu.CompilerParams(dimension_semantics=sem, vmem_limit_bytes=VMEM_LIMIT)


def _silu(z):
    return z * (1.0 / (1.0 + jnp.exp(-z)))


def _sigmoid(z):
    return 1.0 / (1.0 + jnp.exp(-z))


def _rmsnorm_kernel(x_ref, g_ref, o_ref):
    x = x_ref[...]
    r = lax.rsqrt(jnp.mean(x * x, axis=-1, keepdims=True) + RMS_EPS)
    o_ref[...] = (x * r * g_ref[...]).astype(o_ref.dtype)


def _rmsnorm(x, g, out_dtype):
    m, d = x.shape
    tm = min(m, 512)
    return pl.pallas_call(
        _rmsnorm_kernel,
        grid=(pl.cdiv(m, tm),),
        in_specs=[pl.BlockSpec((tm, d), lambda i: (i, 0)),
                  pl.BlockSpec((1, d), lambda i: (0, 0))],
        out_specs=pl.BlockSpec((tm, d), lambda i: (i, 0)),
        out_shape=jax.ShapeDtypeStruct((m, d), out_dtype),
        compiler_params=_params("parallel"),
        name="rmsnorm",
    )(x, g.reshape(1, d))


def _proj_kernel(x_ref, w_ref, *refs, transposed, fill_layer):
    *o_refs, w_bf = refs[-(len(transposed) + 1):]

    @pl.when(pl.program_id(1) == 0)
    def _cast_weight_tile():
        w_bf[...] = w_ref[...].astype(BF16)

    acc = lax.dot_general(x_ref[...], w_bf[...], NT_DIMS, preferred_element_type=F32)
    for pos, (o_ref, tr) in enumerate(zip(o_refs, transposed)):
        if tr:
            for r in range(o_ref.shape[0]):
                rows = acc[r * MOBA_BLOCK:(r + 1) * MOBA_BLOCK, :]
                o_ref[r] = jnp.transpose(rows).astype(o_ref.dtype)
        elif pos == 0 and fill_layer is not None:
            for d in range(o_ref.shape[0]):
                o_ref[d] = acc.astype(o_ref.dtype) if d == fill_layer else jnp.zeros(acc.shape, o_ref.dtype)
        else:
            o_ref[...] = acc.astype(o_ref.dtype)


def _proj(xn, w_t, layer, col_block, n_blocks, outs, name, stack_depth=1, stacked=None):
    m, k = xn.shape
    tm = min(m, PROJ_TM)
    tn = PROJ_TN
    n = n_blocks * tn
    ni = pl.cdiv(m, tm)
    fill_layer = layer if stack_depth > 1 and stacked is None else None
    out_specs, out_shape = [], []
    for pos, (dt, tr) in enumerate(outs):
        if tr:
            out_specs.append(pl.BlockSpec((tm // MOBA_BLOCK, tn, MOBA_BLOCK), lambda j, i: (i, j, 0)))
            out_shape.append(jax.ShapeDtypeStruct((m // MOBA_BLOCK, n, MOBA_BLOCK), dt))
        elif pos == 0 and stack_depth > 1:
            if stacked is None:
                out_specs.append(pl.BlockSpec((stack_depth, tm, tn), lambda j, i: (0, i, j)))
            else:
                out_specs.append(pl.BlockSpec((None, tm, tn), lambda j, i: (layer, i, j)))
            out_shape.append(jax.ShapeDtypeStruct((stack_depth, m, n), dt))
        else:
            out_specs.append(pl.BlockSpec((tm, tn), lambda j, i: (i, j)))
            out_shape.append(jax.ShapeDtypeStruct((m, n), dt))
    in_specs = [pl.BlockSpec((tm, k), lambda j, i: (i, 0)),
                pl.BlockSpec((None, tn, k), lambda j, i: (layer, col_block(j), 0))]
    operands = [xn, w_t]
    aliases = {}
    if stacked is not None:
        in_specs.append(pl.BlockSpec(memory_space=pl.ANY))
        operands.append(stacked)
        aliases = {2: 0}
    return pl.pallas_call(
        functools.partial(_proj_kernel, transposed=tuple(tr for _, tr in outs), fill_layer=fill_layer),
        grid=(n_blocks, ni),
        in_specs=in_specs,
        out_specs=out_specs,
        out_shape=out_shape,
        scratch_shapes=[pltpu.VMEM((tn, k), BF16)],
        input_output_aliases=aliases,
        compiler_params=_params("parallel", "arbitrary"),
        name=name,
    )(*operands)


def _gates_kernel(x_ref, wa_ref, wb_ref, o_ref, w_bf, *, shift):
    @pl.when(pl.program_id(1) == 0)
    def _cast_weight_tile():
        w = jnp.concatenate([wa_ref[shift:, :], wb_ref[...]], axis=0)
        w_bf[...] = w.astype(BF16)

    o_ref[...] = lax.dot_general(x_ref[...], w_bf[...], NT_DIMS, preferred_element_type=F32).astype(o_ref.dtype)


def _gates(xn, w_t, layer):
    m, k = xn.shape
    tm = min(m, PROJ_TM)
    tn = PROJ_TN
    base = W_GM // tn * tn
    shift = W_GM - base
    assert shift % SUBLANES == 0 and tn % shift == 0
    return pl.pallas_call(
        functools.partial(_gates_kernel, shift=shift),
        grid=(2 * D_MODEL // tn, pl.cdiv(m, tm)),
        in_specs=[pl.BlockSpec((tm, k), lambda j, i: (i, 0)),
                  pl.BlockSpec((None, tn, k), lambda j, i: (layer, base // tn + j, 0)),
                  pl.BlockSpec((None, shift, k), lambda j, i: (layer, (base + (j + 1) * tn) // shift, 0))],
        out_specs=pl.BlockSpec((tm, tn), lambda j, i: (i, j)),
        out_shape=jax.ShapeDtypeStruct((m, 2 * D_MODEL), BF16),
        scratch_shapes=[pltpu.VMEM((tn, k), BF16)],
        compiler_params=_params("parallel", "arbitrary"),
        name="proj_gates",
    )(xn, w_t, w_t)


def _gate_kernel(x_ref, w1_ref, w2_ref, b_ref, o_ref):
    a = lax.dot_general(x_ref[...], w1_ref[...].astype(BF16), NT_DIMS, preferred_element_type=F32).astype(BF16)
    y = jnp.dot(a, w2_ref[...], preferred_element_type=F32) + b_ref[...]
    log_sig = jnp.minimum(y, 0.0) - jnp.log(1.0 + jnp.exp(-jnp.abs(y)))
    o_ref[...] = log_sig * (1.0 / GLA_TAU)


def _gate(xn, w_t, layer, w2, b):
    m, k = xn.shape
    tm = min(m, 512)
    assert W_ALR % LANES == 0
    return pl.pallas_call(
        _gate_kernel,
        grid=(pl.cdiv(m, tm),),
        in_specs=[pl.BlockSpec((tm, k), lambda i: (i, 0)),
                  pl.BlockSpec((None, LANES, k), lambda i: (layer, W_ALR // LANES, 0)),
                  pl.BlockSpec((LANES, GLA_K_WIDTH), lambda i: (0, 0)),
                  pl.BlockSpec((1, GLA_K_WIDTH), lambda i: (0, 0))],
        out_specs=pl.BlockSpec((tm, GLA_K_WIDTH), lambda i: (i, 0)),
        out_shape=jax.ShapeDtypeStruct((m, GLA_K_WIDTH), F32),
        compiler_params=_params("parallel"),
        name="gla_gate",
    )(xn, w_t, w2, b.reshape(1, GLA_K_WIDTH))


def _merge_kernel(oa_ref, ob_ref, wa_ref, wb_ref, ga_ref, gb_ref, ba_ref, bb_ref, o_ref):
    ya = jnp.dot(oa_ref[...], wa_ref[...], preferred_element_type=F32)
    yb = jnp.dot(ob_ref[...], wb_ref[...], preferred_element_type=F32)
    ga = _sigmoid(ga_ref[...].astype(F32) + ba_ref[...])
    gb = _sigmoid(gb_ref[...].astype(F32) + bb_ref[...])
    o_ref[...] = (ga * ya + gb * yb).astype(o_ref.dtype)


def _merge(oa, ob, wa, wb, gates, b_merge):
    m = oa.shape[0]
    tm = min(m, 1024)
    tn = 1024
    nj = D_MODEL // tn
    bm = b_merge.reshape(1, 2 * D_MODEL)
    return pl.pallas_call(
        _merge_kernel,
        grid=(nj, pl.cdiv(m, tm)),
        in_specs=[pl.BlockSpec((tm, MOBA_WIDTH), lambda j, i: (i, 0)),
                  pl.BlockSpec((tm, GLA_V_WIDTH), lambda j, i: (i, 0)),
                  pl.BlockSpec((MOBA_WIDTH, tn), lambda j, i: (0, j)),
                  pl.BlockSpec((GLA_V_WIDTH, tn), lambda j, i: (0, j)),
                  pl.BlockSpec((tm, tn), lambda j, i: (i, j)),
                  pl.BlockSpec((tm, tn), lambda j, i: (i, nj + j)),
                  pl.BlockSpec((1, tn), lambda j, i: (0, j)),
                  pl.BlockSpec((1, tn), lambda j, i: (0, nj + j))],
        out_specs=pl.BlockSpec((tm, tn), lambda j, i: (i, j)),
        out_shape=jax.ShapeDtypeStruct((m, D_MODEL), BF16),
        compiler_params=_params("parallel", "parallel"),
        name="gated_merge",
    )(oa, ob, wa, wb, gates, gates, bm, bm)


def _out_kernel(m_ref, w_ref, x_ref, g_ref, *o_refs, emit_x):
    x_new = x_ref[...] + jnp.dot(m_ref[...], w_ref[...], preferred_element_type=F32)
    r = lax.rsqrt(jnp.mean(x_new * x_new, axis=-1, keepdims=True) + RMS_EPS)
    xn = x_new * r * g_ref[...]
    if emit_x:
        o_refs[0][...] = x_new
        o_refs[1][...] = xn.astype(o_refs[1].dtype)
    else:
        o_refs[0][...] = xn.astype(o_refs[0].dtype)


def _out_proj(merged, w_out, x, g_next, emit_x):
    m = x.shape[0]
    tm = min(m, 512)
    row = pl.BlockSpec((tm, D_MODEL), lambda i: (i, 0))
    if emit_x:
        out_specs = [row, row]
        out_shape = [jax.ShapeDtypeStruct((m, D_MODEL), F32),
                     jax.ShapeDtypeStruct((m, D_MODEL), BF16)]
    else:
        out_specs = [row]
        out_shape = [jax.ShapeDtypeStruct((m, D_MODEL), F32)]
    return pl.pallas_call(
        functools.partial(_out_kernel, emit_x=emit_x),
        grid=(pl.cdiv(m, tm),),
        in_specs=[row,
                  pl.BlockSpec((D_MODEL, D_MODEL), lambda i: (0, 0)),
                  row,
                  pl.BlockSpec((1, D_MODEL), lambda i: (0, 0))],
        out_specs=out_specs,
        out_shape=out_shape,
        compiler_params=_params("parallel"),
        name="out_proj",
    )(merged, w_out, x, g_next.reshape(1, D_MODEL))


def _block_mean_kernel(k_ref, o_ref):
    o_ref[...] = jnp.sum(k_ref[...], axis=0, keepdims=True) * (1.0 / MOBA_BLOCK)


def _block_means(k32, layer):
    nblk = k32.shape[1] // MOBA_BLOCK
    return pl.pallas_call(
        _block_mean_kernel,
        grid=(nblk,),
        in_specs=[pl.BlockSpec((None, MOBA_BLOCK, MOBA_WIDTH), lambda i: (layer, i, 0))],
        out_specs=pl.BlockSpec((None, 1, MOBA_WIDTH), lambda i: (i, 0, 0)),
        out_shape=jax.ShapeDtypeStruct((nblk, 1, MOBA_WIDTH), F32),
        compiler_params=_params("parallel"),
        name="moba_block_means",
    )(k32)


def _top_k_mask(s, idx, n_cand, axis):
    sel = jnp.zeros(s.shape, F32)
    for _ in range(MOBA_TOPK):
        mx = jnp.max(s, axis=axis, keepdims=True)
        cand = jnp.where((s == mx) & (mx > -jnp.inf), idx, float(n_cand))
        first = jnp.min(cand, axis=axis, keepdims=True)
        pick = idx == first
        sel = jnp.where(pick, 1.0, sel)
        s = jnp.where(pick, -jnp.inf, s)
    return sel


def _moba_prompt_kernel(pt_ref, q_ref, z_ref, k_ref, vt_ref, mean_ref, ck_ref, o_ref, pm_ref,
                        sel_scr, acc_scr, page_buf, page_sem, *, nblk, heads, group, layer, pages_per_step):
    j = pl.program_id(2)

    step = (pl.program_id(0) * pl.num_programs(1) + pl.program_id(1)) * nblk + j
    n_steps = pl.num_programs(0) * pl.num_programs(1) * nblk
    pages_per_seq = pt_ref.shape[1]
    pages_per_block = MOBA_BLOCK // page_buf.shape[2]

    def page_copies(s):
        first = s * pages_per_step
        seq = first // pages_per_seq
        pg0 = first % pages_per_seq
        slot = s % 2
        return [pltpu.make_async_copy(ck_ref.at[layer, pt_ref[seq, pg0 + p]], page_buf.at[slot, p],
                                      page_sem.at[slot]) for p in range(pages_per_step)]

    @pl.when(step == 0)
    def _first_pages():
        for cp in page_copies(step):
            cp.start()

    @pl.when(step + 1 < n_steps)
    def _next_pages():
        for cp in page_copies(step + 1):
            cp.start()

    for cp in page_copies(step):
        cp.wait()
    first_page = step * pages_per_step
    for i in range(pages_per_step // pages_per_block):
        tot = None
        for pg in range(pages_per_block):
            part = jnp.sum(page_buf[step % 2, i * pages_per_block + pg], axis=0)
            tot = part if tot is None else tot + part
        pm_ref[first_page // pages_per_seq, (first_page % pages_per_seq) // pages_per_block + i] = (
            tot * (1.0 / MOBA_BLOCK))

    c = (HEAD_DIM ** -0.5) * LOG2E
    lanes = [slice(h * HEAD_DIM, (h + 1) * HEAD_DIM) for h in range(heads)]
    qs = [q_ref[:, ln] for ln in lanes]

    def logits(h, n):
        r = pl.multiple_of(n * MOBA_BLOCK, MOBA_BLOCK)
        return lax.dot_general(k_ref[pl.ds(r, MOBA_BLOCK), lanes[h]], qs[h], NT_DIMS, preferred_element_type=F32)

    m0, l0 = [], []
    for h in range(heads):
        s = lax.dot_general(mean_ref[:, lanes[h]].astype(BF16), qs[h], NT_DIMS, preferred_element_type=F32)
        blk = lax.broadcasted_iota(jnp.int32, s.shape, 0)
        s = jnp.where(blk < j, s, -jnp.inf)
        sel_scr[h] = _top_k_mask(s, blk.astype(F32), nblk, 0)

        lg = logits(h, j)
        key = lax.broadcasted_iota(jnp.int32, lg.shape, 0)
        qry = lax.broadcasted_iota(jnp.int32, lg.shape, 1)
        lg = jnp.where(key <= qry, lg, NEG_INF)
        m = jnp.max(lg, axis=0, keepdims=True)
        p = jnp.exp2((lg - m) * c)
        m0.append(m)
        l0.append(jnp.sum(p, axis=0, keepdims=True))
        acc_scr[h] = jnp.dot(vt_ref[j, lanes[h], :], p.astype(BF16), preferred_element_type=F32)

    def past_blocks(i, carry):
        ms, ls = carry
        ns = [jnp.minimum(group * i + g, nblk - 1) for g in range(group)]
        lgs = [[logits(h, n) for n in ns] for h in range(heads)]
        new_m, new_l, alphas, ps = [], [], [], []
        for h in range(heads):
            chosen = [(sel_scr[h, pl.ds(n, 1), :] > 0.0) & (group * i + g < j) for g, n in enumerate(ns)]
            m_new = ms[h]
            for lg, ch in zip(lgs[h], chosen):
                m_new = jnp.maximum(m_new, jnp.where(ch, jnp.max(lg, axis=0, keepdims=True), NEG_INF))
            alpha = jnp.exp2((ms[h] - m_new) * c)
            l_new = alpha * ls[h]
            p_h = []
            for lg, ch in zip(lgs[h], chosen):
                p = jnp.exp2((lg - jnp.where(ch, m_new, -NEG_INF)) * c)
                l_new = l_new + jnp.sum(p, axis=0, keepdims=True)
                p_h.append(p.astype(BF16))
            ps.append(p_h)
            alphas.append(alpha)
            new_m.append(m_new)
            new_l.append(l_new)
        for h in range(heads):
            pv = None
            for p, n in zip(ps[h], ns):
                part = jnp.dot(vt_ref[n, lanes[h], :], p, preferred_element_type=F32)
                pv = part if pv is None else pv + part
            acc_scr[h] = alphas[h] * acc_scr[h] + pv
        return tuple(new_m), tuple(new_l)

    _, l_fin = lax.fori_loop(0, (j + group - 1) // group, past_blocks, (tuple(m0), tuple(l0)))

    for h in range(heads):
        z = z_ref[:, lanes[h]].astype(F32)
        o_ref[:, lanes[h]] = (jnp.transpose(acc_scr[h] / l_fin[h]) * _silu(z)).astype(o_ref.dtype)


MOBA_HEADS_PER_STEP = 4
MOBA_BLOCKS_PER_TRIP = 2


def _moba_prompt(main, k16, vt16, means, cache_k, page_table, layer, batch, seq):
    nblk = seq // MOBA_BLOCK
    hp = MOBA_HEADS_PER_STEP
    width = hp * HEAD_DIM
    dec_b, n_pages = page_table.shape
    page = cache_k.shape[2]
    pages_per_block = MOBA_BLOCK // page
    n_steps = batch * (MOBA_HEADS // hp) * nblk
    pages_per_step = dec_b * n_pages // n_steps
    assert pages_per_step * n_steps == dec_b * n_pages, "pages must split evenly over the grid steps"
    assert n_pages % pages_per_step == 0 and pages_per_step % pages_per_block == 0
    n_past_blocks = n_pages // pages_per_block
    qmap = lambda b, h, j, pt: (b * nblk + j, COL_QA // width + h)
    zmap = lambda b, h, j, pt: (b * nblk + j, COL_ZA // width + h)
    return pl.pallas_call(
        functools.partial(_moba_prompt_kernel, nblk=nblk, heads=hp, group=MOBA_BLOCKS_PER_TRIP,
                          layer=layer, pages_per_step=pages_per_step),
        grid_spec=pltpu.PrefetchScalarGridSpec(
            num_scalar_prefetch=1,
            grid=(batch, MOBA_HEADS // hp, nblk),
            in_specs=[pl.BlockSpec((MOBA_BLOCK, width), qmap),
                      pl.BlockSpec((MOBA_BLOCK, width), zmap),
                      pl.BlockSpec((seq, width), lambda b, h, j, pt: (b, h)),
                      pl.BlockSpec((nblk, width, MOBA_BLOCK), lambda b, h, j, pt: (b, h, 0)),
                      pl.BlockSpec((None, nblk, width), lambda b, h, j, pt: (b, 0, h)),
                      pl.BlockSpec(memory_space=pl.ANY)],
            out_specs=[pl.BlockSpec((MOBA_BLOCK, width), lambda b, h, j, pt: (b * nblk + j, h)),
                       pl.BlockSpec((dec_b, n_past_blocks, MOBA_HEADS, HEAD_DIM), lambda b, h, j, pt: (0, 0, 0, 0))],
            scratch_shapes=[pltpu.VMEM((hp, nblk, MOBA_BLOCK), F32),
                            pltpu.VMEM((hp, HEAD_DIM, MOBA_BLOCK), F32),
                            pltpu.VMEM((2, pages_per_step, page, MOBA_HEADS, HEAD_DIM), F32),
                            pltpu.SemaphoreType.DMA((2,))],
        ),
        out_shape=[jax.ShapeDtypeStruct((batch * seq, MOBA_WIDTH), BF16),
                   jax.ShapeDtypeStruct((dec_b, n_past_blocks, MOBA_HEADS, HEAD_DIM), F32)],
        compiler_params=_params("arbitrary", "arbitrary", "arbitrary"),
        name="moba_prompt",
    )(page_table, main, main, k16, vt16, means, cache_k)


def _gla_kernel(q_ref, k_ref, v_ref, g_ref, z_ref, s0_ref, gn_ref, o_ref, sout_ref, s_scr,
                *, chunk, sub, n_valid):
    t = pl.program_id(2)

    @pl.when(t == 0)
    def _load_state():
        s_scr[...] = s0_ref[...]

    q = q_ref[...].astype(F32) * (GLA_DK ** -0.5)
    k = k_ref[...].astype(F32)
    v = v_ref[...]
    g = g_ref[...]
    if n_valid < chunk:
        live = lax.broadcasted_iota(jnp.int32, g.shape, 0) < n_valid
        g = jnp.where(live, g, 0.0)
        k = jnp.where(live, k, 0.0)

    ri = lax.broadcasted_iota(jnp.int32, (chunk, chunk), 0)
    ci = lax.broadcasted_iota(jnp.int32, (chunk, chunk), 1)
    causal = ci <= ri
    g_hi = g.astype(BF16)
    g_r1 = g - g_hi.astype(F32)
    g_mid = g_r1.astype(BF16)
    g_lo = (g_r1 - g_mid.astype(F32)).astype(BF16)
    tri = jnp.where(causal, 1.0, 0.0).astype(BF16)
    b3 = jnp.dot(tri, jnp.concatenate([g_hi, g_mid, g_lo], axis=1), preferred_element_type=F32)
    b = (b3[:, :GLA_DK] + b3[:, GLA_DK:2 * GLA_DK]) + b3[:, 2 * GLA_DK:]
    b_last = b[chunk - 1:chunk]

    nsub = chunk // sub
    ends = [b[(i + 1) * sub - 1:(i + 1) * sub] for i in range(nsub)]
    own_end = jnp.concatenate([jnp.broadcast_to(e, (sub, GLA_DK)) for e in ends], axis=0)
    k_rel = k * jnp.exp(own_end - b)
    row_sub = lax.broadcasted_iota(jnp.int32, (chunk, GLA_DK), 0) // sub
    q_parts, k_parts = [], []
    for i in range(nsub):
        q_parts.append(jnp.where(row_sub >= i, q * jnp.exp(b - ends[i]), 0.0).astype(BF16))
        k_parts.append(jnp.where(row_sub == i, k_rel, 0.0).astype(BF16))
    q_cat = jnp.concatenate(q_parts, axis=1) if nsub > 1 else q_parts[0]
    k_cat = jnp.concatenate(k_parts, axis=1) if nsub > 1 else k_parts[0]
    a = lax.dot_general(q_cat, k_cat, NT_DIMS, preferred_element_type=F32)
    a = jnp.where(causal, a, 0.0)

    state = s_scr[...]
    o = (jnp.dot(a.astype(BF16), v, preferred_element_type=F32)
         + jnp.dot((q * jnp.exp(b)).astype(BF16), state.astype(BF16), preferred_element_type=F32))

    k_dec = (k * jnp.exp(b_last - b)).astype(BF16)
    update = lax.dot_general(k_dec, v, TN_DIMS, preferred_element_type=F32)
    decay_t = jnp.transpose(jnp.broadcast_to(jnp.exp(b_last), (GLA_DK, GLA_DK)))
    new_state = jnp.concatenate([decay_t] * (GLA_DV // GLA_DK), axis=1) * state + update
    s_scr[...] = new_state

    r = lax.rsqrt(jnp.mean(o * o, axis=-1, keepdims=True) + RMS_EPS)
    z = z_ref[...].astype(F32)
    o_ref[...] = (o * r * gn_ref[...] * _silu(z)).astype(o_ref.dtype)

    @pl.when(t == pl.num_programs(2) - 1)
    def _store_state():
        sout_ref[...] = new_state


def _gla(main, log_a, s0, gn, batch, rows_per_seq, chunk, sub, n_valid):
    nt = rows_per_seq // chunk
    m = main.shape[0]
    rmap = lambda off: (lambda b, h, t: (b * nt + t, off + h))
    smap = lambda b, h, t: (b, h, 0, 0)
    return pl.pallas_call(
        functools.partial(_gla_kernel, chunk=chunk, sub=sub, n_valid=n_valid),
        grid=(batch, GLA_HEADS, nt),
        in_specs=[pl.BlockSpec((chunk, GLA_DK), rmap(COL_QB // GLA_DK)),
                  pl.BlockSpec((chunk, GLA_DK), rmap(COL_KB // GLA_DK)),
                  pl.BlockSpec((chunk, GLA_DV), rmap(COL_VB // GLA_DV)),
                  pl.BlockSpec((chunk, GLA_DK), rmap(0)),
                  pl.BlockSpec((chunk, GLA_DV), rmap(COL_ZB // GLA_DV)),
                  pl.BlockSpec((None, None, GLA_DK, GLA_DV), smap),
                  pl.BlockSpec((None, 1, GLA_DV), lambda b, h, t: (h, 0, 0))],
        out_specs=[pl.BlockSpec((chunk, GLA_DV), rmap(0)),
                   pl.BlockSpec((None, None, GLA_DK, GLA_DV), smap)],
        out_shape=[jax.ShapeDtypeStruct((m, GLA_V_WIDTH), BF16),
                   jax.ShapeDtypeStruct((batch, GLA_HEADS, GLA_DK, GLA_DV), F32)],
        scratch_shapes=[pltpu.VMEM((GLA_DK, GLA_DV), F32)],
        compiler_params=_params("parallel", "parallel", "arbitrary"),
        name="gla",
    )(main, main, main, log_a, main, s0, gn.reshape(GLA_HEADS, 1, GLA_DV))


def _sample_select_kernel(q_ref, mean_ref, o_ref, *, n_blocks):
    q = q_ref[...]
    means = mean_ref[...].astype(BF16)
    rows = q.shape[0]
    ncol = n_blocks * MOBA_HEADS
    col = lax.broadcasted_iota(jnp.int32, (rows, ncol), 1)
    col_head = col % MOBA_HEADS
    col_f = col.astype(F32)
    s_all = jnp.zeros((rows, ncol), F32)
    for h in range(MOBA_HEADS):
        s_h = lax.dot_general(q[:, h * HEAD_DIM:(h + 1) * HEAD_DIM], means, NT_DIMS, preferred_element_type=F32)
        s_all = jnp.where(col_head == h, s_h, s_all)
    lane = lax.broadcasted_iota(jnp.int32, o_ref.shape, 1)
    out = jnp.zeros(o_ref.shape, jnp.int32)
    for h in range(MOBA_HEADS):
        s = jnp.where(col_head == h, s_all, -jnp.inf)
        for t in range(MOBA_TOPK):
            mx = jnp.max(s, axis=1, keepdims=True)
            cand = jnp.where((s == mx) & (mx > -jnp.inf), col_f, float(ncol))
            first = jnp.min(cand, axis=1, keepdims=True)
            s = jnp.where(col_f == first, -jnp.inf, s)
            blk = first.astype(jnp.int32) // MOBA_HEADS
            out = jnp.where(lane == h * MOBA_TOPK + t, blk, out)
    o_ref[...] = out


def _sample_select(main, means, dec_b):
    n_blocks = means.shape[1]
    means2 = means.reshape(dec_b, n_blocks * MOBA_HEADS, HEAD_DIM)
    return pl.pallas_call(
        functools.partial(_sample_select_kernel, n_blocks=n_blocks),
        grid=(dec_b,),
        in_specs=[pl.BlockSpec((SAMPLE_ROWS, MOBA_WIDTH), lambda b: (b, COL_QA // MOBA_WIDTH)),
                  pl.BlockSpec((None, n_blocks * MOBA_HEADS, HEAD_DIM), lambda b: (b, 0, 0))],
        out_specs=pl.BlockSpec((None, SAMPLE_ROWS, LANES), lambda b: (b, 0, 0)),
        out_shape=jax.ShapeDtypeStruct((dec_b, SAMPLE_ROWS, LANES), jnp.int32),
        compiler_params=_params("parallel"),
        name="moba_sample_select",
    )(main, means2)


def _moba_sample_kernel(pt_ref, sel_ref, q_ref, z_ref, ko_ref, vo_ref, ck_ref, cv_ref, o_ref,
                        kbuf, vbuf, sem, *, layer, n_new, page):
    b = pl.program_id(0)
    h = pl.program_id(1)
    pages_per_block = MOBA_BLOCK // page
    n_slots = n_new * MOBA_TOPK * pages_per_block

    def page_copies(slot):
        qi, rest = divmod(slot, MOBA_TOPK * pages_per_block)
        rank, pg = divmod(rest, pages_per_block)
        blk = sel_ref[b, qi * (MOBA_HEADS * MOBA_TOPK) + h * MOBA_TOPK + rank]
        pid = pt_ref[b, blk * pages_per_block + pg]
        dst = pl.ds(slot * page, page)
        return (pltpu.make_async_copy(ck_ref.at[layer, pid, :, h, :], kbuf.at[dst, :], sem.at[0]),
                pltpu.make_async_copy(cv_ref.at[layer, pid, :, h, :], vbuf.at[dst, :], sem.at[1]))

    for slot in range(n_slots):
        for c in page_copies(slot):
            c.start()
    for slot in range(n_slots):
        for c in page_copies(slot):
            c.wait()

    scale = HEAD_DIM ** -0.5
    q = q_ref[...]
    rows = q.shape[0]
    n_keys = n_slots * page
    lg = lax.dot_general(q, kbuf[...].astype(BF16), NT_DIMS, preferred_element_type=F32) * scale
    row = lax.broadcasted_iota(jnp.int32, (rows, n_keys), 0)
    key_owner = lax.broadcasted_iota(jnp.int32, (rows, n_keys), 1) // (MOBA_TOPK * MOBA_BLOCK)
    lg = jnp.where(key_owner == row, lg, NEG_INF)
    lg_own = lax.dot_general(q, ko_ref[...], NT_DIMS, preferred_element_type=F32) * scale
    r2 = lax.broadcasted_iota(jnp.int32, lg_own.shape, 0)
    c2 = lax.broadcasted_iota(jnp.int32, lg_own.shape, 1)
    lg_own = jnp.where((c2 <= r2) & (c2 < n_new), lg_own, NEG_INF)
    m = jnp.maximum(jnp.max(lg, axis=1, keepdims=True), jnp.max(lg_own, axis=1, keepdims=True))
    p = jnp.exp(lg - m)
    p_own = jnp.exp(lg_own - m)
    denom = jnp.sum(p, axis=1, keepdims=True) + jnp.sum(p_own, axis=1, keepdims=True)
    acc = (jnp.dot(p.astype(BF16), vbuf[...].astype(BF16), preferred_element_type=F32)
           + jnp.dot(p_own.astype(BF16), vo_ref[...], preferred_element_type=F32))
    z = z_ref[...].astype(F32)
    o_ref[...] = (acc / denom * _silu(z)).astype(o_ref.dtype)


def _moba_sample(main, k16, v16, cache_k, cache_v, page_table, sel, layer, n_new):
    dec_b = page_table.shape[0]
    page = cache_k.shape[2]
    n_keys = n_new * MOBA_TOPK * MOBA_BLOCK
    tile = lambda off: pl.BlockSpec((SAMPLE_ROWS, HEAD_DIM), lambda b, h, pt, sl: (b, off + h))
    return pl.pallas_call(
        functools.partial(_moba_sample_kernel, layer=layer, n_new=n_new, page=page),
        grid_spec=pltpu.PrefetchScalarGridSpec(
            num_scalar_prefetch=2,
            grid=(dec_b, MOBA_HEADS),
            in_specs=[tile(COL_QA // HEAD_DIM), tile(COL_ZA // HEAD_DIM), tile(0), tile(0),
                      pl.BlockSpec(memory_space=pl.ANY), pl.BlockSpec(memory_space=pl.ANY)],
            out_specs=tile(0),
            scratch_shapes=[pltpu.VMEM((n_keys, HEAD_DIM), F32),
                            pltpu.VMEM((n_keys, HEAD_DIM), F32),
                            pltpu.SemaphoreType.DMA((2,))],
        ),
        out_shape=jax.ShapeDtypeStruct((dec_b * SAMPLE_ROWS, MOBA_WIDTH), BF16),
        compiler_params=_params("arbitrary", "arbitrary"),
        name="moba_sample",
    )(page_table, sel, main, main, k16, v16, cache_k, cache_v)


def _main_col_block(j):
    return jnp.where(j == 0, W_QA // PROJ_TN, j + (W_ZA // PROJ_TN - 1))


def _token_proj(xn, w_t, layer, w_g2, b_gate_l, prompt, stack_depth=1, k_stacked=None, v_stacked=None):
    k32, k16 = _proj(xn, w_t, layer, lambda j: W_KA // PROJ_TN, 1, ((F32, False), (BF16, False)), "proj_k",
                     stack_depth, k_stacked)
    v32, v16 = _proj(xn, w_t, layer, lambda j: W_VA // PROJ_TN, 1, ((F32, False), (BF16, prompt)), "proj_v",
                     stack_depth, v_stacked)
    (main,) = _proj(xn, w_t, layer, _main_col_block, MAIN_WIDTH // PROJ_TN, ((BF16, False),), "proj_main")
    gates = _gates(xn, w_t, layer)
    log_a = _gate(xn, w_t, layer, w_g2, b_gate_l)
    return k32, k16, v32, v16, main, gates, log_a


def kernel(x_prompt, x_sample, cache_k, cache_v, state_gla, page_table, norm_g, w_in, w_gate2, b_gate,
           gla_norm_g, w_branch_a, w_branch_b, b_merge, w_out, final_norm_g):
    depth = w_in.shape[0]
    bp, seq, _ = x_prompt.shape
    dec_b, n_new, _ = x_sample.shape
    assert n_new <= SAMPLE_ROWS and seq % MOBA_BLOCK == 0
    assert cache_k.shape[2] * page_table.shape[1] % MOBA_BLOCK == 0
    assert (W_ZA // PROJ_TN, W_ZB // PROJ_TN) == (3, 6) and W_QB % PROJ_TN == 0

    xp = x_prompt.reshape(bp * seq, D_MODEL)
    xs = jnp.pad(x_sample, ((0, 0), (0, SAMPLE_ROWS - n_new), (0, 0))).reshape(dec_b * SAMPLE_ROWS, D_MODEL)
    xnp = _rmsnorm(xp, norm_g[0], BF16)
    xns = _rmsnorm(xs, norm_g[0], BF16)
    zero_state = jnp.zeros((bp, GLA_HEADS, GLA_DK, GLA_DV), F32)
    w_t = jnp.swapaxes(w_in, 1, 2)

    sp_l, ks_l, vs_l, ss_l = [], [], [], []
    kp_all = vp_all = None
    for l in range(depth):
        w_g2 = jnp.pad(w_gate2[l].astype(BF16), ((0, LANES - GLA_GATE_RANK), (0, 0)))
        wa = w_branch_a[l].astype(BF16)
        wb = w_branch_b[l].astype(BF16)
        wo = w_out[l].astype(BF16)
        last = l == depth - 1
        g_next = final_norm_g if last else norm_g[l + 1]

        kp_all, k16, vp_all, vt16, main, gates, log_a = _token_proj(
            xnp, w_t, l, w_g2, b_gate[l], True, depth, kp_all, vp_all)
        means = _block_means(kp_all, l).reshape(bp, seq // MOBA_BLOCK, MOBA_WIDTH)
        oa, pmeans = _moba_prompt(main, k16, vt16, means, cache_k, page_table, l, bp, seq)
        ob, sp = _gla(main, log_a, zero_state, gla_norm_g[l], bp, seq, MOBA_BLOCK, GLA_SUB, MOBA_BLOCK)
        merged = _merge(oa, ob, wa, wb, gates, b_merge[l])
        outs = _out_proj(merged, wo, xp, g_next, not last)
        if last:
            (yp,) = outs
        else:
            xp, xnp = outs
        sp_l.append(sp)

        k32, k16, v32, v16, main, gates, log_a = _token_proj(xns, w_t, l, w_g2, b_gate[l], False)
        sel = _sample_select(main, pmeans, dec_b)
        sel = sel[:, :n_new, :MOBA_HEADS * MOBA_TOPK].reshape(dec_b, n_new * MOBA_HEADS * MOBA_TOPK)
        oa = _moba_sample(main, k16, v16, cache_k, cache_v, page_table, sel, l, n_new)
        ob, ss = _gla(main, log_a, state_gla[l], gla_norm_g[l], dec_b, SAMPLE_ROWS, SAMPLE_ROWS,
                      SAMPLE_ROWS, n_new)
        merged = _merge(oa, ob, wa, wb, gates, b_merge[l])
        outs = _out_proj(merged, wo, xs, g_next, not last)
        if last:
            (ys,) = outs
        else:
            xs, xns = outs
        rows = lambda a: a.reshape(dec_b, SAMPLE_ROWS, MOBA_HEADS, HEAD_DIM)[:, :n_new]
        ks_l.append(rows(k32))
        vs_l.append(rows(v32))
        ss_l.append(ss)

    y_prompt = yp.reshape(bp, seq, D_MODEL)
    y_sample = ys.reshape(dec_b, SAMPLE_ROWS, D_MODEL)[:, :n_new]
    k_prompt = kp_all.reshape(depth, bp, seq, MOBA_HEADS, HEAD_DIM)
    v_prompt = vp_all.reshape(depth, bp, seq, MOBA_HEADS, HEAD_DIM)
    return (y_prompt, y_sample, k_prompt, v_prompt, jnp.stack(sp_l),
            jnp.stack(ks_l), jnp.stack(vs_l), jnp.stack(ss_l))
```

```python
import functools
import math

import jax
import jax.numpy as jnp
from jax import lax
from jax.experimental import pallas as pl
from jax.experimental.pallas import tpu as pltpu

D_MODEL = 2048
MOBA_HEADS = 8
HEAD_DIM = 128
MOBA_BLOCK = 256
MOBA_TOPK = 3
GLA_HEADS = 4
GLA_DK = 128
GLA_DV = 256
GLA_GATE_RANK = 16
GLA_TAU = 16.0
GLA_SUB = 32
RMS_EPS = 1e-6
NEG_INF = -1e30
MOBA_WIDTH = MOBA_HEADS * HEAD_DIM
GLA_K_WIDTH = GLA_HEADS * GLA_DK
GLA_V_WIDTH = GLA_HEADS * GLA_DV

W_QA = 0
W_KA = W_QA + MOBA_WIDTH
W_VA = W_KA + MOBA_WIDTH
W_ZA = W_VA + MOBA_WIDTH
W_QB = W_ZA + MOBA_WIDTH
W_KB = W_QB + GLA_K_WIDTH
W_VB = W_KB + GLA_K_WIDTH
W_ZB = W_VB + GLA_V_WIDTH
W_ALR = W_ZB + GLA_V_WIDTH
W_GM = W_ALR + GLA_GATE_RANK

COL_QA = 0
COL_ZA = COL_QA + MOBA_WIDTH
COL_QB = COL_ZA + MOBA_WIDTH
COL_KB = COL_QB + GLA_K_WIDTH
COL_VB = COL_KB + GLA_K_WIDTH
COL_ZB = COL_VB + GLA_V_WIDTH
MAIN_WIDTH = COL_ZB + GLA_V_WIDTH

LANES = 128
SUBLANES = 8
SAMPLE_ROWS = SUBLANES
PROJ_TN = 1024
PROJ_TM = 1024
VMEM_LIMIT = 56 * 1024 * 1024

F32 = jnp.float32
BF16 = jnp.bfloat16
NT_DIMS = (((1,), (1,)), ((), ()))
TN_DIMS = (((0,), (0,)), ((), ()))
LOG2E = math.log2(math.e)


def _params(*sem):
    return pltpu.CompilerParams(dimension_semantics=sem, vmem_limit_bytes=VMEM_LIMIT)


def _silu(z):
    return z * (1.0 / (1.0 + jnp.exp(-z)))


def _sigmoid(z):
    return 1.0 / (1.0 + jnp.exp(-z))


def _rmsnorm_kernel(x_ref, g_ref, o_ref):
    x = x_ref[...]
    r = lax.rsqrt(jnp.mean(x * x, axis=-1, keepdims=True) + RMS_EPS)
    o_ref[...] = (x * r * g_ref[...]).astype(o_ref.dtype)


def _rmsnorm(x, g, out_dtype):
    m, d = x.shape
    tm = min(m, 512)
    return pl.pallas_call(
        _rmsnorm_kernel,
        grid=(pl.cdiv(m, tm),),
        in_specs=[pl.BlockSpec((tm, d), lambda i: (i, 0)),
                  pl.BlockSpec((1, d), lambda i: (0, 0))],
        out_specs=pl.BlockSpec((tm, d), lambda i: (i, 0)),
        out_shape=jax.ShapeDtypeStruct((m, d), out_dtype),
        compiler_params=_params("parallel"),
        name="rmsnorm",
    )(x, g.reshape(1, d))


def _proj_kernel(x_ref, w_ref, *refs, transposed, fill_layer):
    *o_refs, w_bf = refs[-(len(transposed) + 1):]

    @pl.when(pl.program_id(1) == 0)
    def _cast_weight_tile():
        w_bf[...] = w_ref[...].astype(BF16)

    acc = lax.dot_general(x_ref[...], w_bf[...], NT_DIMS, preferred_element_type=F32)
    for pos, (o_ref, tr) in enumerate(zip(o_refs, transposed)):
        if tr == "block_mean":
            for r in range(o_ref.shape[0]):
                rows = acc[r * MOBA_BLOCK:(r + 1) * MOBA_BLOCK, :]
                o_ref[r] = jnp.sum(rows, axis=0, keepdims=True) * (1.0 / MOBA_BLOCK)
        elif tr:
            for r in range(o_ref.shape[0]):
                rows = acc[r * MOBA_BLOCK:(r + 1) * MOBA_BLOCK, :]
                o_ref[r] = jnp.transpose(rows).astype(o_ref.dtype)
        elif pos == 0 and fill_layer is not None:
            for d in range(o_ref.shape[0]):
                o_ref[d] = acc.astype(o_ref.dtype) if d == fill_layer else jnp.zeros(acc.shape, o_ref.dtype)
        else:
            o_ref[...] = acc.astype(o_ref.dtype)


def _proj(xn, w_t, layer, col_block, n_blocks, outs, name, stack_depth=1, stacked=None):
    m, k = xn.shape
    tm = min(m, PROJ_TM)
    tn = PROJ_TN
    n = n_blocks * tn
    ni = pl.cdiv(m, tm)
    fill_layer = layer if stack_depth > 1 and stacked is None else None
    out_specs, out_shape = [], []
    for pos, (dt, tr) in enumerate(outs):
        if tr == "block_mean":
            out_specs.append(pl.BlockSpec((tm // MOBA_BLOCK, 1, tn), lambda j, i: (i, 0, j)))
            out_shape.append(jax.ShapeDtypeStruct((m // MOBA_BLOCK, 1, n), dt))
        elif tr:
            out_specs.append(pl.BlockSpec((tm // MOBA_BLOCK, tn, MOBA_BLOCK), lambda j, i: (i, j, 0)))
            out_shape.append(jax.ShapeDtypeStruct((m // MOBA_BLOCK, n, MOBA_BLOCK), dt))
        elif pos == 0 and stack_depth > 1:
            if stacked is None:
                out_specs.append(pl.BlockSpec((stack_depth, tm, tn), lambda j, i: (0, i, j)))
            else:
                out_specs.append(pl.BlockSpec((None, tm, tn), lambda j, i: (layer, i, j)))
            out_shape.append(jax.ShapeDtypeStruct((stack_depth, m, n), dt))
        else:
            out_specs.append(pl.BlockSpec((tm, tn), lambda j, i: (i, j)))
            out_shape.append(jax.ShapeDtypeStruct((m, n), dt))
    in_specs = [pl.BlockSpec((tm, k), lambda j, i: (i, 0)),
                pl.BlockSpec((None, tn, k), lambda j, i: (layer, col_block(j), 0))]
    operands = [xn, w_t]
    aliases = {}
    if stacked is not None:
        in_specs.append(pl.BlockSpec(memory_space=pl.ANY))
        operands.append(stacked)
        aliases = {2: 0}
    return pl.pallas_call(
        functools.partial(_proj_kernel, transposed=tuple(tr for _, tr in outs), fill_layer=fill_layer),
        grid=(n_blocks, ni),
        in_specs=in_specs,
        out_specs=out_specs,
        out_shape=out_shape,
        scratch_shapes=[pltpu.VMEM((tn, k), BF16)],
        input_output_aliases=aliases,
        compiler_params=_params("parallel", "arbitrary"),
        name=name,
    )(*operands)


def _gates_kernel(x_ref, wa_ref, wb_ref, o_ref, w_bf, *, shift):
    @pl.when(pl.program_id(1) == 0)
    def _cast_weight_tile():
        w = jnp.concatenate([wa_ref[shift:, :], wb_ref[...]], axis=0)
        w_bf[...] = w.astype(BF16)

    o_ref[...] = lax.dot_general(x_ref[...], w_bf[...], NT_DIMS, preferred_element_type=F32).astype(o_ref.dtype)


def _gates(xn, w_t, layer):
    m, k = xn.shape
    tm = min(m, PROJ_TM)
    tn = PROJ_TN
    base = W_GM // tn * tn
    shift = W_GM - base
    assert shift % SUBLANES == 0 and tn % shift == 0
    return pl.pallas_call(
        functools.partial(_gates_kernel, shift=shift),
        grid=(2 * D_MODEL // tn, pl.cdiv(m, tm)),
        in_specs=[pl.BlockSpec((tm, k), lambda j, i: (i, 0)),
                  pl.BlockSpec((None, tn, k), lambda j, i: (layer, base // tn + j, 0)),
                  pl.BlockSpec((None, shift, k), lambda j, i: (layer, (base + (j + 1) * tn) // shift, 0))],
        out_specs=pl.BlockSpec((tm, tn), lambda j, i: (i, j)),
        out_shape=jax.ShapeDtypeStruct((m, 2 * D_MODEL), BF16),
        scratch_shapes=[pltpu.VMEM((tn, k), BF16)],
        compiler_params=_params("parallel", "arbitrary"),
        name="proj_gates",
    )(xn, w_t, w_t)


def _gate_kernel(x_ref, w1_ref, w2_ref, b_ref, o_ref):
    a = lax.dot_general(x_ref[...], w1_ref[...].astype(BF16), NT_DIMS, preferred_element_type=F32).astype(BF16)
    y = jnp.dot(a, w2_ref[...], preferred_element_type=F32) + b_ref[...]
    log_sig = jnp.minimum(y, 0.0) - jnp.log(1.0 + jnp.exp(-jnp.abs(y)))
    o_ref[...] = log_sig * (1.0 / GLA_TAU)


def _gate(xn, w_t, layer, w2, b):
    m, k = xn.shape
    tm = min(m, 512)
    assert W_ALR % LANES == 0
    return pl.pallas_call(
        _gate_kernel,
        grid=(pl.cdiv(m, tm),),
        in_specs=[pl.BlockSpec((tm, k), lambda i: (i, 0)),
                  pl.BlockSpec((None, LANES, k), lambda i: (layer, W_ALR // LANES, 0)),
                  pl.BlockSpec((LANES, GLA_K_WIDTH), lambda i: (0, 0)),
                  pl.BlockSpec((1, GLA_K_WIDTH), lambda i: (0, 0))],
        out_specs=pl.BlockSpec((tm, GLA_K_WIDTH), lambda i: (i, 0)),
        out_shape=jax.ShapeDtypeStruct((m, GLA_K_WIDTH), F32),
        compiler_params=_params("parallel"),
        name="gla_gate",
    )(xn, w_t, w2, b.reshape(1, GLA_K_WIDTH))


def _merge_kernel(oa_ref, ob_ref, wa_ref, wb_ref, ga_ref, gb_ref, ba_ref, bb_ref, o_ref):
    ya = jnp.dot(oa_ref[...], wa_ref[...], preferred_element_type=F32)
    yb = jnp.dot(ob_ref[...], wb_ref[...], preferred_element_type=F32)
    ga = _sigmoid(ga_ref[...].astype(F32) + ba_ref[...])
    gb = _sigmoid(gb_ref[...].astype(F32) + bb_ref[...])
    o_ref[...] = (ga * ya + gb * yb).astype(o_ref.dtype)


def _merge(oa, ob, wa, wb, gates, b_merge):
    m = oa.shape[0]
    tm = min(m, 1024)
    tn = 1024
    nj = D_MODEL // tn
    bm = b_merge.reshape(1, 2 * D_MODEL)
    return pl.pallas_call(
        _merge_kernel,
        grid=(nj, pl.cdiv(m, tm)),
        in_specs=[pl.BlockSpec((tm, MOBA_WIDTH), lambda j, i: (i, 0)),
                  pl.BlockSpec((tm, GLA_V_WIDTH), lambda j, i: (i, 0)),
                  pl.BlockSpec((MOBA_WIDTH, tn), lambda j, i: (0, j)),
                  pl.BlockSpec((GLA_V_WIDTH, tn), lambda j, i: (0, j)),
                  pl.BlockSpec((tm, tn), lambda j, i: (i, j)),
                  pl.BlockSpec((tm, tn), lambda j, i: (i, nj + j)),
                  pl.BlockSpec((1, tn), lambda j, i: (0, j)),
                  pl.BlockSpec((1, tn), lambda j, i: (0, nj + j))],
        out_specs=pl.BlockSpec((tm, tn), lambda j, i: (i, j)),
        out_shape=jax.ShapeDtypeStruct((m, D_MODEL), BF16),
        compiler_params=_params("parallel", "parallel"),
        name="gated_merge",
    )(oa, ob, wa, wb, gates, gates, bm, bm)


def _out_kernel(m_ref, w_ref, x_ref, g_ref, *o_refs, emit_x):
    x_new = x_ref[...] + jnp.dot(m_ref[...], w_ref[...], preferred_element_type=F32)
    r = lax.rsqrt(jnp.mean(x_new * x_new, axis=-1, keepdims=True) + RMS_EPS)
    xn = x_new * r * g_ref[...]
    if emit_x:
        o_refs[0][...] = x_new
        o_refs[1][...] = xn.astype(o_refs[1].dtype)
    else:
        o_refs[0][...] = xn.astype(o_refs[0].dtype)


def _out_proj(merged, w_out, x, g_next, emit_x):
    m = x.shape[0]
    tm = min(m, 512)
    row = pl.BlockSpec((tm, D_MODEL), lambda i: (i, 0))
    if emit_x:
        out_specs = [row, row]
        out_shape = [jax.ShapeDtypeStruct((m, D_MODEL), F32),
                     jax.ShapeDtypeStruct((m, D_MODEL), BF16)]
    else:
        out_specs = [row]
        out_shape = [jax.ShapeDtypeStruct((m, D_MODEL), F32)]
    return pl.pallas_call(
        functools.partial(_out_kernel, emit_x=emit_x),
        grid=(pl.cdiv(m, tm),),
        in_specs=[row,
                  pl.BlockSpec((D_MODEL, D_MODEL), lambda i: (0, 0)),
                  row,
                  pl.BlockSpec((1, D_MODEL), lambda i: (0, 0))],
        out_specs=out_specs,
        out_shape=out_shape,
        compiler_params=_params("parallel"),
        name="out_proj",
    )(merged, w_out, x, g_next.reshape(1, D_MODEL))


def _top_k_mask(s, idx, n_cand, axis):
    sel = jnp.zeros(s.shape, F32)
    for _ in range(MOBA_TOPK):
        mx = jnp.max(s, axis=axis, keepdims=True)
        cand = jnp.where((s == mx) & (mx > -jnp.inf), idx, float(n_cand))
        first = jnp.min(cand, axis=axis, keepdims=True)
        pick = idx == first
        sel = jnp.where(pick, 1.0, sel)
        s = jnp.where(pick, -jnp.inf, s)
    return sel


def _moba_prompt_kernel(pt_ref, q_ref, z_ref, k_ref, vt_ref, mean_ref, ck_ref, o_ref, pm_ref,
                        sel_scr, acc_scr, lg_scr, page_buf, page_sem, *, nblk, heads, group, layer, pages_per_step):
    j = pl.program_id(2)

    step = (pl.program_id(0) * pl.num_programs(1) + pl.program_id(1)) * nblk + j
    n_steps = pl.num_programs(0) * pl.num_programs(1) * nblk
    pages_per_seq = pt_ref.shape[1]
    pages_per_block = MOBA_BLOCK // page_buf.shape[2]

    def page_copies(s):
        first = s * pages_per_step
        seq = first // pages_per_seq
        pg0 = first % pages_per_seq
        slot = s % 2
        return [pltpu.make_async_copy(ck_ref.at[layer, pt_ref[seq, pg0 + p]], page_buf.at[slot, p],
                                      page_sem.at[slot]) for p in range(pages_per_step)]

    @pl.when(step == 0)
    def _first_pages():
        for cp in page_copies(step):
            cp.start()

    @pl.when(step + 1 < n_steps)
    def _next_pages():
        for cp in page_copies(step + 1):
            cp.start()

    for cp in page_copies(step):
        cp.wait()
    first_page = step * pages_per_step
    for i in range(pages_per_step // pages_per_block):
        tot = None
        for pg in range(pages_per_block):
            part = jnp.sum(page_buf[step % 2, i * pages_per_block + pg], axis=0)
            tot = part if tot is None else tot + part
        pm_ref[first_page // pages_per_seq, (first_page % pages_per_seq) // pages_per_block + i] = (
            tot * (1.0 / MOBA_BLOCK))

    c = (HEAD_DIM ** -0.5) * LOG2E
    lanes = [slice(h * HEAD_DIM, (h + 1) * HEAD_DIM) for h in range(heads)]
    qs = [q_ref[:, ln] for ln in lanes]

    def logits(h, n):
        r = pl.multiple_of(n * MOBA_BLOCK, MOBA_BLOCK)
        return lax.dot_general(k_ref[pl.ds(r, MOBA_BLOCK), lanes[h]], qs[h], NT_DIMS, preferred_element_type=F32)

    def trip_blocks(i):
        return [jnp.minimum(group * i + g, nblk - 1) for g in range(group)]

    scores = [lax.dot_general(mean_ref[:, lanes[h]].astype(BF16), qs[h], NT_DIMS, preferred_element_type=F32)
              for h in range(heads)]
    blk = lax.broadcasted_iota(jnp.int32, scores[0].shape, 0)
    for h in range(heads):
        sel_scr[h] = _top_k_mask(jnp.where(blk < j, scores[h], -jnp.inf), blk.astype(F32), nblk, 0)

    own = [logits(h, j) for h in range(heads)]
    for h in range(heads):
        for g, n in enumerate(trip_blocks(0)):
            lg_scr[h * group + g] = logits(h, n)
    key = lax.broadcasted_iota(jnp.int32, own[0].shape, 0)
    qry = lax.broadcasted_iota(jnp.int32, own[0].shape, 1)
    m0, l0, p0 = [], [], []
    for h in range(heads):
        lg = jnp.where(key <= qry, own[h], NEG_INF)
        m = jnp.max(lg, axis=0, keepdims=True)
        p = jnp.exp2((lg - m) * c)
        m0.append(m)
        l0.append(jnp.sum(p, axis=0, keepdims=True))
        p0.append(p.astype(BF16))
    for h in range(heads):
        acc_scr[h] = jnp.dot(vt_ref[j, lanes[h], :], p0[h], preferred_element_type=F32)

    def past_blocks(i, carry):
        ms, ls = carry
        ns = trip_blocks(i)
        ns_next = trip_blocks(i + 1)
        new_m, new_l = [], []
        for h in range(heads):
            lgs = [lg_scr[h * group + g] for g in range(group)]
            for g, n in enumerate(ns_next):
                lg_scr[h * group + g] = logits(h, n)
            chosen = [(sel_scr[h, pl.ds(n, 1), :] > 0.0) & (group * i + g < j) for g, n in enumerate(ns)]
            m_new = ms[h]
            for lg, ch in zip(lgs, chosen):
                m_new = jnp.maximum(m_new, jnp.where(ch, jnp.max(lg, axis=0, keepdims=True), NEG_INF))
            alpha = jnp.exp2((ms[h] - m_new) * c)
            l_new = alpha * ls[h]
            pv = None
            for lg, ch, n in zip(lgs, chosen, ns):
                p = jnp.exp2((lg - jnp.where(ch, m_new, -NEG_INF)) * c)
                l_new = l_new + jnp.sum(p, axis=0, keepdims=True)
                part = jnp.dot(vt_ref[n, lanes[h], :], p.astype(BF16), preferred_element_type=F32)
                pv = part if pv is None else pv + part
            acc_scr[h] = alpha * acc_scr[h] + pv
            new_m.append(m_new)
            new_l.append(l_new)
        return tuple(new_m), tuple(new_l)

    _, l_fin = lax.fori_loop(0, (j + group - 1) // group, past_blocks, (tuple(m0), tuple(l0)))

    for h in range(heads):
        z = z_ref[:, lanes[h]].astype(F32)
        o_ref[:, lanes[h]] = (jnp.transpose(acc_scr[h] / l_fin[h]) * _silu(z)).astype(o_ref.dtype)


MOBA_HEADS_PER_STEP = 4
MOBA_BLOCKS_PER_TRIP = 2


def _moba_prompt(main, k16, vt16, means, cache_k, page_table, layer, batch, seq):
    nblk = seq // MOBA_BLOCK
    hp = MOBA_HEADS_PER_STEP
    width = hp * HEAD_DIM
    dec_b, n_pages = page_table.shape
    page = cache_k.shape[2]
    pages_per_block = MOBA_BLOCK // page
    n_steps = batch * (MOBA_HEADS // hp) * nblk
    pages_per_step = dec_b * n_pages // n_steps
    assert pages_per_step * n_steps == dec_b * n_pages, "pages must split evenly over the grid steps"
    assert n_pages % pages_per_step == 0 and pages_per_step % pages_per_block == 0
    n_past_blocks = n_pages // pages_per_block
    qmap = lambda b, h, j, pt: (b * nblk + j, COL_QA // width + h)
    zmap = lambda b, h, j, pt: (b * nblk + j, COL_ZA // width + h)
    return pl.pallas_call(
        functools.partial(_moba_prompt_kernel, nblk=nblk, heads=hp, group=MOBA_BLOCKS_PER_TRIP,
                          layer=layer, pages_per_step=pages_per_step),
        grid_spec=pltpu.PrefetchScalarGridSpec(
            num_scalar_prefetch=1,
            grid=(batch, MOBA_HEADS // hp, nblk),
            in_specs=[pl.BlockSpec((MOBA_BLOCK, width), qmap),
                      pl.BlockSpec((MOBA_BLOCK, width), zmap),
                      pl.BlockSpec((seq, width), lambda b, h, j, pt: (b, h)),
                      pl.BlockSpec((nblk, width, MOBA_BLOCK), lambda b, h, j, pt: (b, h, 0)),
                      pl.BlockSpec((None, nblk, width), lambda b, h, j, pt: (b, 0, h)),
                      pl.BlockSpec(memory_space=pl.ANY)],
            out_specs=[pl.BlockSpec((MOBA_BLOCK, width), lambda b, h, j, pt: (b * nblk + j, h)),
                       pl.BlockSpec((dec_b, n_past_blocks, MOBA_HEADS, HEAD_DIM), lambda b, h, j, pt: (0, 0, 0, 0))],
            scratch_shapes=[pltpu.VMEM((hp, nblk, MOBA_BLOCK), F32),
                            pltpu.VMEM((hp, HEAD_DIM, MOBA_BLOCK), F32),
                            pltpu.VMEM((hp * MOBA_BLOCKS_PER_TRIP, MOBA_BLOCK, MOBA_BLOCK), F32),
                            pltpu.VMEM((2, pages_per_step, page, MOBA_HEADS, HEAD_DIM), F32),
                            pltpu.SemaphoreType.DMA((2,))],
        ),
        out_shape=[jax.ShapeDtypeStruct((batch * seq, MOBA_WIDTH), BF16),
                   jax.ShapeDtypeStruct((dec_b, n_past_blocks, MOBA_HEADS, HEAD_DIM), F32)],
        compiler_params=_params("arbitrary", "arbitrary", "arbitrary"),
        name="moba_prompt",
    )(page_table, main, main, k16, vt16, means, cache_k)


def _gla_kernel(q_ref, k_ref, v_ref, g_ref, z_ref, s0_ref, gn_ref, o_ref, sout_ref, s_scr,
                *, chunk, sub, n_valid, heads):
    t = pl.program_id(2)

    @pl.when(t == 0)
    def _load_state():
        s_scr[...] = s0_ref[...]

    hs = range(heads)
    dk = [slice(h * GLA_DK, (h + 1) * GLA_DK) for h in hs]
    dv = [slice(h * GLA_DV, (h + 1) * GLA_DV) for h in hs]
    ri = lax.broadcasted_iota(jnp.int32, (chunk, chunk), 0)
    ci = lax.broadcasted_iota(jnp.int32, (chunk, chunk), 1)
    causal = ci <= ri
    tri = jnp.where(causal, 1.0, 0.0).astype(BF16)
    live = lax.broadcasted_iota(jnp.int32, (chunk, GLA_DK), 0) < n_valid
    row_sub = lax.broadcasted_iota(jnp.int32, (chunk, GLA_DK), 0) // sub
    nsub = chunk // sub

    ks, b3s = [], []
    for h in hs:
        k = k_ref[:, dk[h]].astype(F32)
        g = g_ref[:, dk[h]]
        if n_valid < chunk:
            g = jnp.where(live, g, 0.0)
            k = jnp.where(live, k, 0.0)
        g_hi = g.astype(BF16)
        g_r1 = g - g_hi.astype(F32)
        g_mid = g_r1.astype(BF16)
        g_lo = (g_r1 - g_mid.astype(F32)).astype(BF16)
        ks.append(k)
        b3s.append(jnp.dot(tri, jnp.concatenate([g_hi, g_mid, g_lo], axis=1), preferred_element_type=F32))

    bs, qs, scores = [], [], []
    for h in hs:
        b3 = b3s[h]
        b = (b3[:, :GLA_DK] + b3[:, GLA_DK:2 * GLA_DK]) + b3[:, 2 * GLA_DK:]
        q = q_ref[:, dk[h]].astype(F32) * (GLA_DK ** -0.5)
        ends = [b[(i + 1) * sub - 1:(i + 1) * sub] for i in range(nsub)]
        own_end = jnp.concatenate([jnp.broadcast_to(e, (sub, GLA_DK)) for e in ends], axis=0)
        k_rel = ks[h] * jnp.exp(own_end - b)
        q_parts, k_parts = [], []
        for i in range(nsub):
            q_parts.append(jnp.where(row_sub >= i, q * jnp.exp(b - ends[i]), 0.0).astype(BF16))
            k_parts.append(jnp.where(row_sub == i, k_rel, 0.0).astype(BF16))
        q_cat = jnp.concatenate(q_parts, axis=1) if nsub > 1 else q_parts[0]
        k_cat = jnp.concatenate(k_parts, axis=1) if nsub > 1 else k_parts[0]
        scores.append(lax.dot_general(q_cat, k_cat, NT_DIMS, preferred_element_type=F32))
        bs.append(b)
        qs.append(q)

    outs, new_states = [], []
    for h in hs:
        b = bs[h]
        b_last = b[chunk - 1:chunk]
        v = v_ref[:, dv[h]]
        state = s_scr[h]
        a = jnp.where(causal, scores[h], 0.0)
        outs.append(jnp.dot(a.astype(BF16), v, preferred_element_type=F32)
                    + jnp.dot((qs[h] * jnp.exp(b)).astype(BF16), state.astype(BF16), preferred_element_type=F32))
        k_dec = (ks[h] * jnp.exp(b_last - b)).astype(BF16)
        update = lax.dot_general(k_dec, v, TN_DIMS, preferred_element_type=F32)
        decay_t = jnp.transpose(jnp.broadcast_to(jnp.exp(b_last), (GLA_DK, GLA_DK)))
        new_states.append(jnp.concatenate([decay_t] * (GLA_DV // GLA_DK), axis=1) * state + update)

    for h in hs:
        s_scr[h] = new_states[h]
        o = outs[h]
        r = lax.rsqrt(jnp.mean(o * o, axis=-1, keepdims=True) + RMS_EPS)
        z = z_ref[:, dv[h]].astype(F32)
        o_ref[:, dv[h]] = (o * r * gn_ref[h] * _silu(z)).astype(o_ref.dtype)

    @pl.when(t == pl.num_programs(2) - 1)
    def _store_state():
        for h in hs:
            sout_ref[h] = new_states[h]


GLA_HEADS_PER_STEP = 4


def _gla(main, log_a, s0, gn, batch, rows_per_seq, chunk, sub, n_valid):
    nt = rows_per_seq // chunk
    m = main.shape[0]
    hp = GLA_HEADS_PER_STEP
    wk, wv = hp * GLA_DK, hp * GLA_DV
    rmap = lambda off: (lambda b, h, t: (b * nt + t, off + h))
    smap = lambda b, h, t: (b, h, 0, 0)
    return pl.pallas_call(
        functools.partial(_gla_kernel, chunk=chunk, sub=sub, n_valid=n_valid, heads=hp),
        grid=(batch, GLA_HEADS // hp, nt),
        in_specs=[pl.BlockSpec((chunk, wk), rmap(COL_QB // wk)),
                  pl.BlockSpec((chunk, wk), rmap(COL_KB // wk)),
                  pl.BlockSpec((chunk, wv), rmap(COL_VB // wv)),
                  pl.BlockSpec((chunk, wk), rmap(0)),
                  pl.BlockSpec((chunk, wv), rmap(COL_ZB // wv)),
                  pl.BlockSpec((None, hp, GLA_DK, GLA_DV), smap),
                  pl.BlockSpec((hp, 1, GLA_DV), lambda b, h, t: (h, 0, 0))],
        out_specs=[pl.BlockSpec((chunk, wv), rmap(0)),
                   pl.BlockSpec((None, hp, GLA_DK, GLA_DV), smap)],
        out_shape=[jax.ShapeDtypeStruct((m, GLA_V_WIDTH), BF16),
                   jax.ShapeDtypeStruct((batch, GLA_HEADS, GLA_DK, GLA_DV), F32)],
        scratch_shapes=[pltpu.VMEM((hp, GLA_DK, GLA_DV), F32)],
        compiler_params=_params("parallel", "parallel", "arbitrary"),
        name="gla",
    )(main, main, main, log_a, main, s0, gn.reshape(GLA_HEADS, 1, GLA_DV))


def _sample_select_kernel(q_ref, mean_ref, o_ref, *, n_blocks):
    q = q_ref[...]
    means = mean_ref[...].astype(BF16)
    rows = q.shape[0]
    ncol = n_blocks * MOBA_HEADS
    col = lax.broadcasted_iota(jnp.int32, (rows, ncol), 1)
    col_head = col % MOBA_HEADS
    col_f = col.astype(F32)
    s_all = jnp.zeros((rows, ncol), F32)
    for h in range(MOBA_HEADS):
        s_h = lax.dot_general(q[:, h * HEAD_DIM:(h + 1) * HEAD_DIM], means, NT_DIMS, preferred_element_type=F32)
        s_all = jnp.where(col_head == h, s_h, s_all)
    lane = lax.broadcasted_iota(jnp.int32, o_ref.shape, 1)
    out = jnp.zeros(o_ref.shape, jnp.int32)
    for h in range(MOBA_HEADS):
        s = jnp.where(col_head == h, s_all, -jnp.inf)
        for t in range(MOBA_TOPK):
            mx = jnp.max(s, axis=1, keepdims=True)
            cand = jnp.where((s == mx) & (mx > -jnp.inf), col_f, float(ncol))
            first = jnp.min(cand, axis=1, keepdims=True)
            s = jnp.where(col_f == first, -jnp.inf, s)
            blk = first.astype(jnp.int32) // MOBA_HEADS
            out = jnp.where(lane == h * MOBA_TOPK + t, blk, out)
    o_ref[...] = out


def _sample_select(main, means, dec_b):
    n_blocks = means.shape[1]
    means2 = means.reshape(dec_b, n_blocks * MOBA_HEADS, HEAD_DIM)
    return pl.pallas_call(
        functools.partial(_sample_select_kernel, n_blocks=n_blocks),
        grid=(dec_b,),
        in_specs=[pl.BlockSpec((SAMPLE_ROWS, MOBA_WIDTH), lambda b: (b, COL_QA // MOBA_WIDTH)),
                  pl.BlockSpec((None, n_blocks * MOBA_HEADS, HEAD_DIM), lambda b: (b, 0, 0))],
        out_specs=pl.BlockSpec((None, SAMPLE_ROWS, LANES), lambda b: (b, 0, 0)),
        out_shape=jax.ShapeDtypeStruct((dec_b, SAMPLE_ROWS, LANES), jnp.int32),
        compiler_params=_params("parallel"),
        name="moba_sample_select",
    )(main, means2)


def _moba_sample_kernel(pt_ref, sel_ref, q_ref, z_ref, ko_ref, vo_ref, ck_ref, cv_ref, o_ref,
                        kbuf, vbuf, sem, *, layer, n_new, page):
    b = pl.program_id(0)
    h = pl.program_id(1)
    pages_per_block = MOBA_BLOCK // page
    n_slots = n_new * MOBA_TOPK * pages_per_block

    def page_copies(slot):
        qi, rest = divmod(slot, MOBA_TOPK * pages_per_block)
        rank, pg = divmod(rest, pages_per_block)
        blk = sel_ref[b, qi * (MOBA_HEADS * MOBA_TOPK) + h * MOBA_TOPK + rank]
        pid = pt_ref[b, blk * pages_per_block + pg]
        dst = pl.ds(slot * page, page)
        return (pltpu.make_async_copy(ck_ref.at[layer, pid, :, h, :], kbuf.at[dst, :], sem.at[0]),
                pltpu.make_async_copy(cv_ref.at[layer, pid, :, h, :], vbuf.at[dst, :], sem.at[1]))

    for slot in range(n_slots):
        for c in page_copies(slot):
            c.start()
    for slot in range(n_slots):
        for c in page_copies(slot):
            c.wait()

    scale = HEAD_DIM ** -0.5
    q = q_ref[...]
    rows = q.shape[0]
    n_keys = n_slots * page
    lg = lax.dot_general(q, kbuf[...].astype(BF16), NT_DIMS, preferred_element_type=F32) * scale
    row = lax.broadcasted_iota(jnp.int32, (rows, n_keys), 0)
    key_owner = lax.broadcasted_iota(jnp.int32, (rows, n_keys), 1) // (MOBA_TOPK * MOBA_BLOCK)
    lg = jnp.where(key_owner == row, lg, NEG_INF)
    lg_own = lax.dot_general(q, ko_ref[...], NT_DIMS, preferred_element_type=F32) * scale
    r2 = lax.broadcasted_iota(jnp.int32, lg_own.shape, 0)
    c2 = lax.broadcasted_iota(jnp.int32, lg_own.shape, 1)
    lg_own = jnp.where((c2 <= r2) & (c2 < n_new), lg_own, NEG_INF)
    m = jnp.maximum(jnp.max(lg, axis=1, keepdims=True), jnp.max(lg_own, axis=1, keepdims=True))
    p = jnp.exp(lg - m)
    p_own = jnp.exp(lg_own - m)
    denom = jnp.sum(p, axis=1, keepdims=True) + jnp.sum(p_own, axis=1, keepdims=True)
    acc = (jnp.dot(p.astype(BF16), vbuf[...].astype(BF16), preferred_element_type=F32)
           + jnp.dot(p_own.astype(BF16), vo_ref[...], preferred_element_type=F32))
    z = z_ref[...].astype(F32)
    o_ref[...] = (acc / denom * _silu(z)).astype(o_ref.dtype)


def _moba_sample(main, k16, v16, cache_k, cache_v, page_table, sel, layer, n_new):
    dec_b = page_table.shape[0]
    page = cache_k.shape[2]
    n_keys = n_new * MOBA_TOPK * MOBA_BLOCK
    tile = lambda off: pl.BlockSpec((SAMPLE_ROWS, HEAD_DIM), lambda b, h, pt, sl: (b, off + h))
    return pl.pallas_call(
        functools.partial(_moba_sample_kernel, layer=layer, n_new=n_new, page=page),
        grid_spec=pltpu.PrefetchScalarGridSpec(
            num_scalar_prefetch=2,
            grid=(dec_b, MOBA_HEADS),
            in_specs=[tile(COL_QA // HEAD_DIM), tile(COL_ZA // HEAD_DIM), tile(0), tile(0),
                      pl.BlockSpec(memory_space=pl.ANY), pl.BlockSpec(memory_space=pl.ANY)],
            out_specs=tile(0),
            scratch_shapes=[pltpu.VMEM((n_keys, HEAD_DIM), F32),
                            pltpu.VMEM((n_keys, HEAD_DIM), F32),
                            pltpu.SemaphoreType.DMA((2,))],
        ),
        out_shape=jax.ShapeDtypeStruct((dec_b * SAMPLE_ROWS, MOBA_WIDTH), BF16),
        compiler_params=_params("arbitrary", "arbitrary"),
        name="moba_sample",
    )(page_table, sel, main, main, k16, v16, cache_k, cache_v)


def _main_col_block(j):
    return jnp.where(j == 0, W_QA // PROJ_TN, j + (W_ZA // PROJ_TN - 1))


def _token_proj(xn, w_t, layer, w_g2, b_gate_l, prompt, stack_depth=1, k_stacked=None, v_stacked=None):
    k_outs = ((F32, False), (BF16, False)) + (((F32, "block_mean"),) if prompt else ())
    k32, k16, *k_means = _proj(xn, w_t, layer, lambda j: W_KA // PROJ_TN, 1, k_outs, "proj_k",
                               stack_depth, k_stacked)
    v32, v16 = _proj(xn, w_t, layer, lambda j: W_VA // PROJ_TN, 1, ((F32, False), (BF16, prompt)), "proj_v",
                     stack_depth, v_stacked)
    (main,) = _proj(xn, w_t, layer, _main_col_block, MAIN_WIDTH // PROJ_TN, ((BF16, False),), "proj_main")
    gates = _gates(xn, w_t, layer)
    log_a = _gate(xn, w_t, layer, w_g2, b_gate_l)
    return k32, k16, v32, v16, main, gates, log_a, k_means


def kernel(x_prompt, x_sample, cache_k, cache_v, state_gla, page_table, norm_g, w_in, w_gate2, b_gate,
           gla_norm_g, w_branch_a, w_branch_b, b_merge, w_out, final_norm_g):
    depth = w_in.shape[0]
    bp, seq, _ = x_prompt.shape
    dec_b, n_new, _ = x_sample.shape
    assert n_new <= SAMPLE_ROWS and seq % MOBA_BLOCK == 0
    assert cache_k.shape[2] * page_table.shape[1] % MOBA_BLOCK == 0
    assert (W_ZA // PROJ_TN, W_ZB // PROJ_TN) == (3, 6) and W_QB % PROJ_TN == 0

    xp = x_prompt.reshape(bp * seq, D_MODEL)
    xs = jnp.pad(x_sample, ((0, 0), (0, SAMPLE_ROWS - n_new), (0, 0))).reshape(dec_b * SAMPLE_ROWS, D_MODEL)
    xnp = _rmsnorm(xp, norm_g[0], BF16)
    xns = _rmsnorm(xs, norm_g[0], BF16)
    zero_state = jnp.zeros((bp, GLA_HEADS, GLA_DK, GLA_DV), F32)
    w_t = jnp.swapaxes(w_in, 1, 2)

    sp_l, ks_l, vs_l, ss_l = [], [], [], []
    kp_all = vp_all = None
    for l in range(depth):
        w_g2 = jnp.pad(w_gate2[l].astype(BF16), ((0, LANES - GLA_GATE_RANK), (0, 0)))
        wa = w_branch_a[l].astype(BF16)
        wb = w_branch_b[l].astype(BF16)
        wo = w_out[l].astype(BF16)
        last = l == depth - 1
        g_next = final_norm_g if last else norm_g[l + 1]

        kp_all, k16, vp_all, vt16, main, gates, log_a, (k_means,) = _token_proj(
            xnp, w_t, l, w_g2, b_gate[l], True, depth, kp_all, vp_all)
        means = k_means.reshape(bp, seq // MOBA_BLOCK, MOBA_WIDTH)
        oa, pmeans = _moba_prompt(main, k16, vt16, means, cache_k, page_table, l, bp, seq)
        ob, sp = _gla(main, log_a, zero_state, gla_norm_g[l], bp, seq, MOBA_BLOCK, GLA_SUB, MOBA_BLOCK)
        merged = _merge(oa, ob, wa, wb, gates, b_merge[l])
        outs = _out_proj(merged, wo, xp, g_next, not last)
        if last:
            (yp,) = outs
        else:
            xp, xnp = outs
        sp_l.append(sp)

        k32, k16, v32, v16, main, gates, log_a, _ = _token_proj(xns, w_t, l, w_g2, b_gate[l], False)
        sel = _sample_select(main, pmeans, dec_b)
        sel = sel[:, :n_new, :MOBA_HEADS * MOBA_TOPK].reshape(dec_b, n_new * MOBA_HEADS * MOBA_TOPK)
        oa = _moba_sample(main, k16, v16, cache_k, cache_v, page_table, sel, l, n_new)
        ob, ss = _gla(main, log_a, state_gla[l], gla_norm_g[l], dec_b, SAMPLE_ROWS, SAMPLE_ROWS,
                      SAMPLE_ROWS, n_new)
        merged = _merge(oa, ob, wa, wb, gates, b_merge[l])
        outs = _out_proj(merged, wo, xs, g_next, not last)
        if last:
            (ys,) = outs
        else:
            xs, xns = outs
        rows = lambda a: a.reshape(dec_b, SAMPLE_ROWS, MOBA_HEADS, HEAD_DIM)[:, :n_new]
        ks_l.append(rows(k32))
        vs_l.append(rows(v32))
        ss_l.append(ss)

    y_prompt = yp.reshape(bp, seq, D_MODEL)
    y_sample = ys.reshape(dec_b, SAMPLE_ROWS, D_MODEL)[:, :n_new]
    k_prompt = kp_all.reshape(depth, bp, seq, MOBA_HEADS, HEAD_DIM)
    v_prompt = vp_all.reshape(depth, bp, seq, MOBA_HEADS, HEAD_DIM)
    return (y_prompt, y_sample, k_prompt, v_prompt, jnp.stack(sp_l),
            jnp.stack(ks_l), jnp.stack(vs_l), jnp.stack(ss_l))
```

```python
import functools
import math

import jax
import jax.numpy as jnp
from jax import lax
from jax.experimental import pallas as pl
from jax.experimental.pallas import tpu as pltpu

D_MODEL = 2048
MOBA_HEADS = 8
HEAD_DIM = 128
MOBA_BLOCK = 256
MOBA_TOPK = 3
GLA_HEADS = 4
GLA_DK = 128
GLA_DV = 256
GLA_GATE_RANK = 16
GLA_TAU = 16.0
GLA_SUB = 32
RMS_EPS = 1e-6
NEG_INF = -1e30
MOBA_WIDTH = MOBA_HEADS * HEAD_DIM
GLA_K_WIDTH = GLA_HEADS * GLA_DK
GLA_V_WIDTH = GLA_HEADS * GLA_DV

W_QA = 0
W_KA = W_QA + MOBA_WIDTH
W_VA = W_KA + MOBA_WIDTH
W_ZA = W_VA + MOBA_WIDTH
W_QB = W_ZA + MOBA_WIDTH
W_KB = W_QB + GLA_K_WIDTH
W_VB = W_KB + GLA_K_WIDTH
W_ZB = W_VB + GLA_V_WIDTH
W_ALR = W_ZB + GLA_V_WIDTH
W_GM = W_ALR + GLA_GATE_RANK

COL_QA = 0
COL_ZA = COL_QA + MOBA_WIDTH
COL_QB = COL_ZA + MOBA_WIDTH
COL_KB = COL_QB + GLA_K_WIDTH
COL_VB = COL_KB + GLA_K_WIDTH
COL_ZB = COL_VB + GLA_V_WIDTH
MAIN_WIDTH = COL_ZB + GLA_V_WIDTH

LANES = 128
SUBLANES = 8
SAMPLE_ROWS = SUBLANES
PROJ_TN = 1024
PROJ_TM = 1024
VMEM_LIMIT = 56 * 1024 * 1024

F32 = jnp.float32
BF16 = jnp.bfloat16
NT_DIMS = (((1,), (1,)), ((), ()))
TN_DIMS = (((0,), (0,)), ((), ()))
LOG2E = math.log2(math.e)


def _params(*sem):
    return pltpu.CompilerParams(dimension_semantics=sem, vmem_limit_bytes=VMEM_LIMIT)


def _silu(z):
    return z * (1.0 / (1.0 + jnp.exp(-z)))


def _sigmoid(z):
    return 1.0 / (1.0 + jnp.exp(-z))


def _rmsnorm_kernel(x_ref, g_ref, o_ref):
    x = x_ref[...]
    r = lax.rsqrt(jnp.mean(x * x, axis=-1, keepdims=True) + RMS_EPS)
    o_ref[...] = (x * r * g_ref[...]).astype(o_ref.dtype)


def _rmsnorm(x, g, out_dtype):
    m, d = x.shape
    tm = min(m, 512)
    return pl.pallas_call(
        _rmsnorm_kernel,
        grid=(pl.cdiv(m, tm),),
        in_specs=[pl.BlockSpec((tm, d), lambda i: (i, 0)),
                  pl.BlockSpec((1, d), lambda i: (0, 0))],
        out_specs=pl.BlockSpec((tm, d), lambda i: (i, 0)),
        out_shape=jax.ShapeDtypeStruct((m, d), out_dtype),
        compiler_params=_params("parallel"),
        name="rmsnorm",
    )(x, g.reshape(1, d))


def _proj_kernel(x_ref, w_ref, *refs, transposed, fill_layer, first_tile_scale):
    *o_refs, w_bf = refs[-(len(transposed) + 1):]

    @pl.when(pl.program_id(1) == 0)
    def _cast_weight_tile():
        w_bf[...] = w_ref[...].astype(BF16)

    acc = lax.dot_general(x_ref[...], w_bf[...], NT_DIMS, preferred_element_type=F32)
    if first_tile_scale is not None:
        acc = acc * jnp.where(pl.program_id(0) == 0, first_tile_scale, 1.0)
    for pos, (o_ref, tr) in enumerate(zip(o_refs, transposed)):
        if tr == "block_mean":
            for r in range(o_ref.shape[0]):
                rows = acc[r * MOBA_BLOCK:(r + 1) * MOBA_BLOCK, :]
                o_ref[r] = jnp.sum(rows, axis=0, keepdims=True) * (1.0 / MOBA_BLOCK)
        elif tr:
            for r in range(o_ref.shape[0]):
                rows = acc[r * MOBA_BLOCK:(r + 1) * MOBA_BLOCK, :]
                o_ref[r] = jnp.transpose(rows).astype(o_ref.dtype)
        elif pos == 0 and fill_layer is not None:
            for d in range(o_ref.shape[0]):
                o_ref[d] = acc.astype(o_ref.dtype) if d == fill_layer else jnp.zeros(acc.shape, o_ref.dtype)
        else:
            o_ref[...] = acc.astype(o_ref.dtype)


def _proj(xn, w_t, layer, col_block, n_blocks, outs, name, stack_depth=1, stacked=None, first_tile_scale=None):
    m, k = xn.shape
    tm = min(m, PROJ_TM)
    tn = PROJ_TN
    n = n_blocks * tn
    ni = pl.cdiv(m, tm)
    fill_layer = layer if stack_depth > 1 and stacked is None else None
    out_specs, out_shape = [], []
    for pos, (dt, tr) in enumerate(outs):
        if tr == "block_mean":
            out_specs.append(pl.BlockSpec((tm // MOBA_BLOCK, 1, tn), lambda j, i: (i, 0, j)))
            out_shape.append(jax.ShapeDtypeStruct((m // MOBA_BLOCK, 1, n), dt))
        elif tr:
            out_specs.append(pl.BlockSpec((tm // MOBA_BLOCK, tn, MOBA_BLOCK), lambda j, i: (i, j, 0)))
            out_shape.append(jax.ShapeDtypeStruct((m // MOBA_BLOCK, n, MOBA_BLOCK), dt))
        elif pos == 0 and stack_depth > 1:
            if stacked is None:
                out_specs.append(pl.BlockSpec((stack_depth, tm, tn), lambda j, i: (0, i, j)))
            else:
                out_specs.append(pl.BlockSpec((None, tm, tn), lambda j, i: (layer, i, j)))
            out_shape.append(jax.ShapeDtypeStruct((stack_depth, m, n), dt))
        else:
            out_specs.append(pl.BlockSpec((tm, tn), lambda j, i: (i, j)))
            out_shape.append(jax.ShapeDtypeStruct((m, n), dt))
    in_specs = [pl.BlockSpec((tm, k), lambda j, i: (i, 0)),
                pl.BlockSpec((None, tn, k), lambda j, i: (layer, col_block(j), 0))]
    operands = [xn, w_t]
    aliases = {}
    if stacked is not None:
        in_specs.append(pl.BlockSpec(memory_space=pl.ANY))
        operands.append(stacked)
        aliases = {2: 0}
    return pl.pallas_call(
        functools.partial(_proj_kernel, transposed=tuple(tr for _, tr in outs), fill_layer=fill_layer,
                          first_tile_scale=first_tile_scale),
        grid=(n_blocks, ni),
        in_specs=in_specs,
        out_specs=out_specs,
        out_shape=out_shape,
        scratch_shapes=[pltpu.VMEM((tn, k), BF16)],
        input_output_aliases=aliases,
        compiler_params=_params("parallel", "arbitrary"),
        name=name,
    )(*operands)


def _gates_kernel(x_ref, wa_ref, wb_ref, o_ref, w_bf, *, shift):
    @pl.when(pl.program_id(1) == 0)
    def _cast_weight_tile():
        w = jnp.concatenate([wa_ref[shift:, :], wb_ref[...]], axis=0)
        w_bf[...] = w.astype(BF16)

    o_ref[...] = lax.dot_general(x_ref[...], w_bf[...], NT_DIMS, preferred_element_type=F32).astype(o_ref.dtype)


def _gates(xn, w_t, layer):
    m, k = xn.shape
    tm = min(m, PROJ_TM)
    tn = PROJ_TN
    base = W_GM // tn * tn
    shift = W_GM - base
    assert shift % SUBLANES == 0 and tn % shift == 0
    return pl.pallas_call(
        functools.partial(_gates_kernel, shift=shift),
        grid=(2 * D_MODEL // tn, pl.cdiv(m, tm)),
        in_specs=[pl.BlockSpec((tm, k), lambda j, i: (i, 0)),
                  pl.BlockSpec((None, tn, k), lambda j, i: (layer, base // tn + j, 0)),
                  pl.BlockSpec((None, shift, k), lambda j, i: (layer, (base + (j + 1) * tn) // shift, 0))],
        out_specs=pl.BlockSpec((tm, tn), lambda j, i: (i, j)),
        out_shape=jax.ShapeDtypeStruct((m, 2 * D_MODEL), BF16),
        scratch_shapes=[pltpu.VMEM((tn, k), BF16)],
        compiler_params=_params("parallel", "arbitrary"),
        name="proj_gates",
    )(xn, w_t, w_t)


def _gate_kernel(x_ref, w1_ref, w2_ref, b_ref, o_ref):
    a = lax.dot_general(x_ref[...], w1_ref[...].astype(BF16), NT_DIMS, preferred_element_type=F32).astype(BF16)
    y = jnp.dot(a, w2_ref[...], preferred_element_type=F32) + b_ref[...]
    log_sig = jnp.minimum(y, 0.0) - jnp.log(1.0 + jnp.exp(-jnp.abs(y)))
    o_ref[...] = log_sig * (1.0 / GLA_TAU)


def _gate(xn, w_t, layer, w2, b):
    m, k = xn.shape
    tm = min(m, 512)
    assert W_ALR % LANES == 0
    return pl.pallas_call(
        _gate_kernel,
        grid=(pl.cdiv(m, tm),),
        in_specs=[pl.BlockSpec((tm, k), lambda i: (i, 0)),
                  pl.BlockSpec((None, LANES, k), lambda i: (layer, W_ALR // LANES, 0)),
                  pl.BlockSpec((LANES, GLA_K_WIDTH), lambda i: (0, 0)),
                  pl.BlockSpec((1, GLA_K_WIDTH), lambda i: (0, 0))],
        out_specs=pl.BlockSpec((tm, GLA_K_WIDTH), lambda i: (i, 0)),
        out_shape=jax.ShapeDtypeStruct((m, GLA_K_WIDTH), F32),
        compiler_params=_params("parallel"),
        name="gla_gate",
    )(xn, w_t, w2, b.reshape(1, GLA_K_WIDTH))


def _merge_kernel(oa_ref, ob_ref, wa_ref, wb_ref, ga_ref, gb_ref, ba_ref, bb_ref, o_ref):
    ya = jnp.dot(oa_ref[...], wa_ref[...], preferred_element_type=F32)
    yb = jnp.dot(ob_ref[...], wb_ref[...], preferred_element_type=F32)
    ga = _sigmoid(ga_ref[...].astype(F32) + ba_ref[...])
    gb = _sigmoid(gb_ref[...].astype(F32) + bb_ref[...])
    o_ref[...] = (ga * ya + gb * yb).astype(o_ref.dtype)


def _merge(oa, ob, wa, wb, gates, b_merge):
    m = oa.shape[0]
    tm = min(m, 1024)
    tn = 1024
    nj = D_MODEL // tn
    bm = b_merge.reshape(1, 2 * D_MODEL)
    return pl.pallas_call(
        _merge_kernel,
        grid=(nj, pl.cdiv(m, tm)),
        in_specs=[pl.BlockSpec((tm, MOBA_WIDTH), lambda j, i: (i, 0)),
                  pl.BlockSpec((tm, GLA_V_WIDTH), lambda j, i: (i, 0)),
                  pl.BlockSpec((MOBA_WIDTH, tn), lambda j, i: (0, j)),
                  pl.BlockSpec((GLA_V_WIDTH, tn), lambda j, i: (0, j)),
                  pl.BlockSpec((tm, tn), lambda j, i: (i, j)),
                  pl.BlockSpec((tm, tn), lambda j, i: (i, nj + j)),
                  pl.BlockSpec((1, tn), lambda j, i: (0, j)),
                  pl.BlockSpec((1, tn), lambda j, i: (0, nj + j))],
        out_specs=pl.BlockSpec((tm, tn), lambda j, i: (i, j)),
        out_shape=jax.ShapeDtypeStruct((m, D_MODEL), BF16),
        compiler_params=_params("parallel", "parallel"),
        name="gated_merge",
    )(oa, ob, wa, wb, gates, gates, bm, bm)


def _out_kernel(m_ref, w_ref, x_ref, g_ref, *o_refs, emit_x):
    x_new = x_ref[...] + jnp.dot(m_ref[...], w_ref[...], preferred_element_type=F32)
    r = lax.rsqrt(jnp.mean(x_new * x_new, axis=-1, keepdims=True) + RMS_EPS)
    xn = x_new * r * g_ref[...]
    if emit_x:
        o_refs[0][...] = x_new
        o_refs[1][...] = xn.astype(o_refs[1].dtype)
    else:
        o_refs[0][...] = xn.astype(o_refs[0].dtype)


def _out_proj(merged, w_out, x, g_next, emit_x):
    m = x.shape[0]
    tm = min(m, 512)
    row = pl.BlockSpec((tm, D_MODEL), lambda i: (i, 0))
    if emit_x:
        out_specs = [row, row]
        out_shape = [jax.ShapeDtypeStruct((m, D_MODEL), F32),
                     jax.ShapeDtypeStruct((m, D_MODEL), BF16)]
    else:
        out_specs = [row]
        out_shape = [jax.ShapeDtypeStruct((m, D_MODEL), F32)]
    return pl.pallas_call(
        functools.partial(_out_kernel, emit_x=emit_x),
        grid=(pl.cdiv(m, tm),),
        in_specs=[row,
                  pl.BlockSpec((D_MODEL, D_MODEL), lambda i: (0, 0)),
                  row,
                  pl.BlockSpec((1, D_MODEL), lambda i: (0, 0))],
        out_specs=out_specs,
        out_shape=out_shape,
        compiler_params=_params("parallel"),
        name="out_proj",
    )(merged, w_out, x, g_next.reshape(1, D_MODEL))


def _top_k_mask(s, idx, n_cand, axis):
    sel = jnp.zeros(s.shape, F32)
    for _ in range(MOBA_TOPK):
        mx = jnp.max(s, axis=axis, keepdims=True)
        cand = jnp.where((s == mx) & (mx > -jnp.inf), idx, float(n_cand))
        first = jnp.min(cand, axis=axis, keepdims=True)
        pick = idx == first
        sel = jnp.where(pick, 1.0, sel)
        s = jnp.where(pick, -jnp.inf, s)
    return sel


def _moba_prompt_kernel(pt_ref, q_ref, z_ref, k_ref, vt_ref, mean_ref, ck_ref, o_ref, pm_ref,
                        sel_scr, acc_scr, lg_scr, page_buf, page_sem, *, nblk, heads, group, layer, pages_per_step):
    j = pl.program_id(2)

    step = (pl.program_id(0) * pl.num_programs(1) + pl.program_id(1)) * nblk + j
    n_steps = pl.num_programs(0) * pl.num_programs(1) * nblk
    pages_per_seq = pt_ref.shape[1]
    pages_per_block = MOBA_BLOCK // page_buf.shape[2]

    def page_copies(s):
        first = s * pages_per_step
        seq = first // pages_per_seq
        pg0 = first % pages_per_seq
        slot = s % 2
        return [pltpu.make_async_copy(ck_ref.at[layer, pt_ref[seq, pg0 + p]], page_buf.at[slot, p],
                                      page_sem.at[slot]) for p in range(pages_per_step)]

    @pl.when(step == 0)
    def _first_pages():
        for cp in page_copies(step):
            cp.start()

    @pl.when(step + 1 < n_steps)
    def _next_pages():
        for cp in page_copies(step + 1):
            cp.start()

    for cp in page_copies(step):
        cp.wait()
    first_page = step * pages_per_step
    for i in range(pages_per_step // pages_per_block):
        tot = None
        for pg in range(pages_per_block):
            part = jnp.sum(page_buf[step % 2, i * pages_per_block + pg], axis=0)
            tot = part if tot is None else tot + part
        pm_ref[first_page // pages_per_seq, (first_page % pages_per_seq) // pages_per_block + i] = (
            tot * (1.0 / MOBA_BLOCK))

    ones_rows = jnp.ones((MOBA_DENOM_ROWS, MOBA_BLOCK), BF16)
    lanes = [slice(h * HEAD_DIM, (h + 1) * HEAD_DIM) for h in range(heads)]
    qs = [q_ref[:, ln] for ln in lanes]

    def logits(h, n):
        r = pl.multiple_of(n * MOBA_BLOCK, MOBA_BLOCK)
        return lax.dot_general(k_ref[pl.ds(r, MOBA_BLOCK), lanes[h]], qs[h], NT_DIMS, preferred_element_type=F32)

    def weighted_values(h, n, p):
        lhs = jnp.concatenate([vt_ref[n, lanes[h], :], ones_rows], axis=0)
        return jnp.dot(lhs, p, preferred_element_type=F32)

    def trip_blocks(i):
        return [jnp.minimum(group * i + g, nblk - 1) for g in range(group)]

    scores = [lax.dot_general(mean_ref[:, lanes[h]].astype(BF16), qs[h], NT_DIMS, preferred_element_type=F32)
              for h in range(heads)]
    blk = lax.broadcasted_iota(jnp.int32, scores[0].shape, 0)
    for h in range(heads):
        sel_scr[h] = _top_k_mask(jnp.where(blk < j, scores[h], -jnp.inf), blk.astype(F32), nblk, 0)

    own = [logits(h, j) for h in range(heads)]
    for h in range(heads):
        for g, n in enumerate(trip_blocks(0)):
            lg_scr[h * group + g] = logits(h, n)
    key = lax.broadcasted_iota(jnp.int32, own[0].shape, 0)
    qry = lax.broadcasted_iota(jnp.int32, own[0].shape, 1)
    m0, p0 = [], []
    for h in range(heads):
        lg = jnp.where(key <= qry, own[h], NEG_INF)
        m = jnp.max(lg, axis=0, keepdims=True)
        m0.append(m)
        p0.append(jnp.exp2(lg - m).astype(BF16))
    for h in range(heads):
        acc_scr[h] = weighted_values(h, j, p0[h])

    def past_blocks(i, carry):
        ms = carry
        ns = trip_blocks(i)
        ns_next = trip_blocks(i + 1)
        new_m = []
        for h in range(heads):
            lgs = [lg_scr[h * group + g] for g in range(group)]
            for g, n in enumerate(ns_next):
                lg_scr[h * group + g] = logits(h, n)
            chosen = [(sel_scr[h, pl.ds(n, 1), :] > 0.0) & (group * i + g < j) for g, n in enumerate(ns)]
            m_new = ms[h]
            for lg, ch in zip(lgs, chosen):
                m_new = jnp.maximum(m_new, jnp.where(ch, jnp.max(lg, axis=0, keepdims=True), NEG_INF))
            alpha = jnp.exp2(ms[h] - m_new)
            pv = None
            for lg, ch, n in zip(lgs, chosen, ns):
                p = jnp.exp2(lg - jnp.where(ch, m_new, -NEG_INF))
                part = weighted_values(h, n, p.astype(BF16))
                pv = part if pv is None else pv + part
            acc_scr[h] = alpha * acc_scr[h] + pv
            new_m.append(m_new)
        return tuple(new_m)

    lax.fori_loop(0, (j + group - 1) // group, past_blocks, tuple(m0))

    for h in range(heads):
        z = z_ref[:, lanes[h]].astype(F32)
        out_t = acc_scr[h, :HEAD_DIM, :] / acc_scr[h, HEAD_DIM:HEAD_DIM + 1, :]
        o_ref[:, lanes[h]] = (jnp.transpose(out_t) * _silu(z)).astype(o_ref.dtype)


MOBA_HEADS_PER_STEP = 4
MOBA_DENOM_ROWS = 16
MOBA_BLOCKS_PER_TRIP = 2


def _moba_prompt(main, k16, vt16, means, cache_k, page_table, layer, batch, seq):
    nblk = seq // MOBA_BLOCK
    hp = MOBA_HEADS_PER_STEP
    width = hp * HEAD_DIM
    dec_b, n_pages = page_table.shape
    page = cache_k.shape[2]
    pages_per_block = MOBA_BLOCK // page
    n_steps = batch * (MOBA_HEADS // hp) * nblk
    pages_per_step = dec_b * n_pages // n_steps
    assert pages_per_step * n_steps == dec_b * n_pages, "pages must split evenly over the grid steps"
    assert n_pages % pages_per_step == 0 and pages_per_step % pages_per_block == 0
    n_past_blocks = n_pages // pages_per_block
    qmap = lambda b, h, j, pt: (b * nblk + j, COL_QA // width + h)
    zmap = lambda b, h, j, pt: (b * nblk + j, COL_ZA // width + h)
    return pl.pallas_call(
        functools.partial(_moba_prompt_kernel, nblk=nblk, heads=hp, group=MOBA_BLOCKS_PER_TRIP,
                          layer=layer, pages_per_step=pages_per_step),
        grid_spec=pltpu.PrefetchScalarGridSpec(
            num_scalar_prefetch=1,
            grid=(batch, MOBA_HEADS // hp, nblk),
            in_specs=[pl.BlockSpec((MOBA_BLOCK, width), qmap),
                      pl.BlockSpec((MOBA_BLOCK, width), zmap),
                      pl.BlockSpec((seq, width), lambda b, h, j, pt: (b, h)),
                      pl.BlockSpec((nblk, width, MOBA_BLOCK), lambda b, h, j, pt: (b, h, 0)),
                      pl.BlockSpec((None, nblk, width), lambda b, h, j, pt: (b, 0, h)),
                      pl.BlockSpec(memory_space=pl.ANY)],
            out_specs=[pl.BlockSpec((MOBA_BLOCK, width), lambda b, h, j, pt: (b * nblk + j, h)),
                       pl.BlockSpec((dec_b, n_past_blocks, MOBA_HEADS, HEAD_DIM), lambda b, h, j, pt: (0, 0, 0, 0))],
            scratch_shapes=[pltpu.VMEM((hp, nblk, MOBA_BLOCK), F32),
                            pltpu.VMEM((hp, HEAD_DIM + MOBA_DENOM_ROWS, MOBA_BLOCK), F32),
                            pltpu.VMEM((hp * MOBA_BLOCKS_PER_TRIP, MOBA_BLOCK, MOBA_BLOCK), F32),
                            pltpu.VMEM((2, pages_per_step, page, MOBA_HEADS, HEAD_DIM), F32),
                            pltpu.SemaphoreType.DMA((2,))],
        ),
        out_shape=[jax.ShapeDtypeStruct((batch * seq, MOBA_WIDTH), BF16),
                   jax.ShapeDtypeStruct((dec_b, n_past_blocks, MOBA_HEADS, HEAD_DIM), F32)],
        compiler_params=_params("arbitrary", "arbitrary", "arbitrary"),
        name="moba_prompt",
    )(page_table, main, main, k16, vt16, means, cache_k)


def _gla_kernel(q_ref, k_ref, v_ref, g_ref, z_ref, s0_ref, gn_ref, o_ref, sout_ref, s_scr,
                *, chunk, sub, n_valid, heads):
    t = pl.program_id(2)

    @pl.when(t == 0)
    def _load_state():
        s_scr[...] = s0_ref[...]

    hs = range(heads)
    dk = [slice(h * GLA_DK, (h + 1) * GLA_DK) for h in hs]
    dv = [slice(h * GLA_DV, (h + 1) * GLA_DV) for h in hs]
    ri = lax.broadcasted_iota(jnp.int32, (chunk, chunk), 0)
    ci = lax.broadcasted_iota(jnp.int32, (chunk, chunk), 1)
    causal = ci <= ri
    tri = jnp.where(causal, 1.0, 0.0).astype(BF16)
    live = lax.broadcasted_iota(jnp.int32, (chunk, GLA_DK), 0) < n_valid
    row_sub = lax.broadcasted_iota(jnp.int32, (chunk, GLA_DK), 0) // sub
    nsub = chunk // sub

    ks, b3s = [], []
    for h in hs:
        k = k_ref[:, dk[h]].astype(F32)
        g = g_ref[:, dk[h]]
        if n_valid < chunk:
            g = jnp.where(live, g, 0.0)
            k = jnp.where(live, k, 0.0)
        g_hi = g.astype(BF16)
        g_r1 = g - g_hi.astype(F32)
        g_mid = g_r1.astype(BF16)
        g_lo = (g_r1 - g_mid.astype(F32)).astype(BF16)
        ks.append(k)
        b3s.append(jnp.dot(tri, jnp.concatenate([g_hi, g_mid, g_lo], axis=1), preferred_element_type=F32))

    bs, qs, scores = [], [], []
    for h in hs:
        b3 = b3s[h]
        b = (b3[:, :GLA_DK] + b3[:, GLA_DK:2 * GLA_DK]) + b3[:, 2 * GLA_DK:]
        q = q_ref[:, dk[h]].astype(F32) * (GLA_DK ** -0.5)
        ends = [b[(i + 1) * sub - 1:(i + 1) * sub] for i in range(nsub)]
        own_end = jnp.concatenate([jnp.broadcast_to(e, (sub, GLA_DK)) for e in ends], axis=0)
        k_rel = ks[h] * jnp.exp(own_end - b)
        q_parts, k_parts = [], []
        for i in range(nsub):
            q_parts.append(jnp.where(row_sub >= i, q * jnp.exp(b - ends[i]), 0.0).astype(BF16))
            k_parts.append(jnp.where(row_sub == i, k_rel, 0.0).astype(BF16))
        q_cat = jnp.concatenate(q_parts, axis=1) if nsub > 1 else q_parts[0]
        k_cat = jnp.concatenate(k_parts, axis=1) if nsub > 1 else k_parts[0]
        scores.append(lax.dot_general(q_cat, k_cat, NT_DIMS, preferred_element_type=F32))
        bs.append(b)
        qs.append(q)

    outs, new_states = [], []
    for h in hs:
        b = bs[h]
        b_last = b[chunk - 1:chunk]
        v = v_ref[:, dv[h]]
        state = s_scr[h]
        a = jnp.where(causal, scores[h], 0.0)
        outs.append(jnp.dot(a.astype(BF16), v, preferred_element_type=F32)
                    + jnp.dot((qs[h] * jnp.exp(b)).astype(BF16), state.astype(BF16), preferred_element_type=F32))
        k_dec = (ks[h] * jnp.exp(b_last - b)).astype(BF16)
        update = lax.dot_general(k_dec, v, TN_DIMS, preferred_element_type=F32)
        decay_t = jnp.transpose(jnp.broadcast_to(jnp.exp(b_last), (GLA_DK, GLA_DK)))
        new_states.append(jnp.concatenate([decay_t] * (GLA_DV // GLA_DK), axis=1) * state + update)

    for h in hs:
        s_scr[h] = new_states[h]
        o = outs[h]
        r = lax.rsqrt(jnp.mean(o * o, axis=-1, keepdims=True) + RMS_EPS)
        z = z_ref[:, dv[h]].astype(F32)
        o_ref[:, dv[h]] = (o * r * gn_ref[h] * _silu(z)).astype(o_ref.dtype)

    @pl.when(t == pl.num_programs(2) - 1)
    def _store_state():
        for h in hs:
            sout_ref[h] = new_states[h]


GLA_HEADS_PER_STEP = 4


def _gla(main, log_a, s0, gn, batch, rows_per_seq, chunk, sub, n_valid):
    nt = rows_per_seq // chunk
    m = main.shape[0]
    hp = GLA_HEADS_PER_STEP
    wk, wv = hp * GLA_DK, hp * GLA_DV
    rmap = lambda off: (lambda b, h, t: (b * nt + t, off + h))
    smap = lambda b, h, t: (b, h, 0, 0)
    return pl.pallas_call(
        functools.partial(_gla_kernel, chunk=chunk, sub=sub, n_valid=n_valid, heads=hp),
        grid=(batch, GLA_HEADS // hp, nt),
        in_specs=[pl.BlockSpec((chunk, wk), rmap(COL_QB // wk)),
                  pl.BlockSpec((chunk, wk), rmap(COL_KB // wk)),
                  pl.BlockSpec((chunk, wv), rmap(COL_VB // wv)),
                  pl.BlockSpec((chunk, wk), rmap(0)),
                  pl.BlockSpec((chunk, wv), rmap(COL_ZB // wv)),
                  pl.BlockSpec((None, hp, GLA_DK, GLA_DV), smap),
                  pl.BlockSpec((hp, 1, GLA_DV), lambda b, h, t: (h, 0, 0))],
        out_specs=[pl.BlockSpec((chunk, wv), rmap(0)),
                   pl.BlockSpec((None, hp, GLA_DK, GLA_DV), smap)],
        out_shape=[jax.ShapeDtypeStruct((m, GLA_V_WIDTH), BF16),
                   jax.ShapeDtypeStruct((batch, GLA_HEADS, GLA_DK, GLA_DV), F32)],
        scratch_shapes=[pltpu.VMEM((hp, GLA_DK, GLA_DV), F32)],
        compiler_params=_params("parallel", "parallel", "arbitrary"),
        name="gla",
    )(main, main, main, log_a, main, s0, gn.reshape(GLA_HEADS, 1, GLA_DV))


def _sample_select_kernel(q_ref, mean_ref, o_ref, *, n_blocks):
    q = q_ref[...]
    means = mean_ref[...].astype(BF16)
    rows = q.shape[0]
    ncol = n_blocks * MOBA_HEADS
    col = lax.broadcasted_iota(jnp.int32, (rows, ncol), 1)
    col_head = col % MOBA_HEADS
    col_f = col.astype(F32)
    s_all = jnp.zeros((rows, ncol), F32)
    for h in range(MOBA_HEADS):
        s_h = lax.dot_general(q[:, h * HEAD_DIM:(h + 1) * HEAD_DIM], means, NT_DIMS, preferred_element_type=F32)
        s_all = jnp.where(col_head == h, s_h, s_all)
    lane = lax.broadcasted_iota(jnp.int32, o_ref.shape, 1)
    out = jnp.zeros(o_ref.shape, jnp.int32)
    for h in range(MOBA_HEADS):
        s = jnp.where(col_head == h, s_all, -jnp.inf)
        for t in range(MOBA_TOPK):
            mx = jnp.max(s, axis=1, keepdims=True)
            cand = jnp.where((s == mx) & (mx > -jnp.inf), col_f, float(ncol))
            first = jnp.min(cand, axis=1, keepdims=True)
            s = jnp.where(col_f == first, -jnp.inf, s)
            blk = first.astype(jnp.int32) // MOBA_HEADS
            out = jnp.where(lane == h * MOBA_TOPK + t, blk, out)
    o_ref[...] = out


def _sample_select(main, means, dec_b):
    n_blocks = means.shape[1]
    means2 = means.reshape(dec_b, n_blocks * MOBA_HEADS, HEAD_DIM)
    return pl.pallas_call(
        functools.partial(_sample_select_kernel, n_blocks=n_blocks),
        grid=(dec_b,),
        in_specs=[pl.BlockSpec((SAMPLE_ROWS, MOBA_WIDTH), lambda b: (b, COL_QA // MOBA_WIDTH)),
                  pl.BlockSpec((None, n_blocks * MOBA_HEADS, HEAD_DIM), lambda b: (b, 0, 0))],
        out_specs=pl.BlockSpec((None, SAMPLE_ROWS, LANES), lambda b: (b, 0, 0)),
        out_shape=jax.ShapeDtypeStruct((dec_b, SAMPLE_ROWS, LANES), jnp.int32),
        compiler_params=_params("parallel"),
        name="moba_sample_select",
    )(main, means2)


def _moba_sample_kernel(pt_ref, sel_ref, q_ref, z_ref, ko_ref, vo_ref, ck_ref, cv_ref, o_ref,
                        kbuf, vbuf, sem, *, layer, n_new, page):
    n_heads = pl.num_programs(1)
    step = pl.program_id(0) * n_heads + pl.program_id(1)
    n_steps = pl.num_programs(0) * n_heads
    pages_per_block = MOBA_BLOCK // page
    n_slots = n_new * MOBA_TOPK * pages_per_block

    def page_copies(s):
        b, h = s // n_heads, s % n_heads
        buf = s % 2
        copies = []
        for slot in range(n_slots):
            qi, rest = divmod(slot, MOBA_TOPK * pages_per_block)
            rank, pg = divmod(rest, pages_per_block)
            blk = sel_ref[b, qi * (MOBA_HEADS * MOBA_TOPK) + h * MOBA_TOPK + rank]
            pid = pt_ref[b, blk * pages_per_block + pg]
            dst = pl.ds(slot * page, page)
            copies.append(pltpu.make_async_copy(ck_ref.at[layer, pid, :, h, :], kbuf.at[buf, dst, :],
                                                sem.at[buf, 0]))
            copies.append(pltpu.make_async_copy(cv_ref.at[layer, pid, :, h, :], vbuf.at[buf, dst, :],
                                                sem.at[buf, 1]))
        return copies

    @pl.when(step == 0)
    def _first_gather():
        for cp in page_copies(step):
            cp.start()

    @pl.when(step + 1 < n_steps)
    def _next_gather():
        for cp in page_copies(step + 1):
            cp.start()

    for cp in page_copies(step):
        cp.wait()

    scale = HEAD_DIM ** -0.5
    q = q_ref[...]
    rows = q.shape[0]
    n_keys = n_slots * page
    k_sel = kbuf[step % 2]
    v_sel = vbuf[step % 2]
    lg = lax.dot_general(q, k_sel.astype(BF16), NT_DIMS, preferred_element_type=F32) * scale
    row = lax.broadcasted_iota(jnp.int32, (rows, n_keys), 0)
    key_owner = lax.broadcasted_iota(jnp.int32, (rows, n_keys), 1) // (MOBA_TOPK * MOBA_BLOCK)
    lg = jnp.where(key_owner == row, lg, NEG_INF)
    lg_own = lax.dot_general(q, ko_ref[...], NT_DIMS, preferred_element_type=F32) * scale
    r2 = lax.broadcasted_iota(jnp.int32, lg_own.shape, 0)
    c2 = lax.broadcasted_iota(jnp.int32, lg_own.shape, 1)
    lg_own = jnp.where((c2 <= r2) & (c2 < n_new), lg_own, NEG_INF)
    m = jnp.maximum(jnp.max(lg, axis=1, keepdims=True), jnp.max(lg_own, axis=1, keepdims=True))
    p = jnp.exp(lg - m)
    p_own = jnp.exp(lg_own - m)
    denom = jnp.sum(p, axis=1, keepdims=True) + jnp.sum(p_own, axis=1, keepdims=True)
    acc = (jnp.dot(p.astype(BF16), v_sel.astype(BF16), preferred_element_type=F32)
           + jnp.dot(p_own.astype(BF16), vo_ref[...], preferred_element_type=F32))
    z = z_ref[...].astype(F32)
    o_ref[...] = (acc / denom * _silu(z)).astype(o_ref.dtype)


def _moba_sample(main, k16, v16, cache_k, cache_v, page_table, sel, layer, n_new):
    dec_b = page_table.shape[0]
    page = cache_k.shape[2]
    n_keys = n_new * MOBA_TOPK * MOBA_BLOCK
    tile = lambda off: pl.BlockSpec((SAMPLE_ROWS, HEAD_DIM), lambda b, h, pt, sl: (b, off + h))
    return pl.pallas_call(
        functools.partial(_moba_sample_kernel, layer=layer, n_new=n_new, page=page),
        grid_spec=pltpu.PrefetchScalarGridSpec(
            num_scalar_prefetch=2,
            grid=(dec_b, MOBA_HEADS),
            in_specs=[tile(COL_QA // HEAD_DIM), tile(COL_ZA // HEAD_DIM), tile(0), tile(0),
                      pl.BlockSpec(memory_space=pl.ANY), pl.BlockSpec(memory_space=pl.ANY)],
            out_specs=tile(0),
            scratch_shapes=[pltpu.VMEM((2, n_keys, HEAD_DIM), F32),
                            pltpu.VMEM((2, n_keys, HEAD_DIM), F32),
                            pltpu.SemaphoreType.DMA((2, 2))],
        ),
        out_shape=jax.ShapeDtypeStruct((dec_b * SAMPLE_ROWS, MOBA_WIDTH), BF16),
        compiler_params=_params("arbitrary", "arbitrary"),
        name="moba_sample",
    )(page_table, sel, main, main, k16, v16, cache_k, cache_v)


def _main_col_block(j):
    return jnp.where(j == 0, W_QA // PROJ_TN, j + (W_ZA // PROJ_TN - 1))


def _token_proj(xn, w_t, layer, w_g2, b_gate_l, prompt, stack_depth=1, k_stacked=None, v_stacked=None):
    k_outs = ((F32, False), (BF16, False)) + (((F32, "block_mean"),) if prompt else ())
    k32, k16, *k_means = _proj(xn, w_t, layer, lambda j: W_KA // PROJ_TN, 1, k_outs, "proj_k",
                               stack_depth, k_stacked)
    v32, v16 = _proj(xn, w_t, layer, lambda j: W_VA // PROJ_TN, 1, ((F32, False), (BF16, prompt)), "proj_v",
                     stack_depth, v_stacked)
    q_scale = (HEAD_DIM ** -0.5) * LOG2E if prompt else None
    (main,) = _proj(xn, w_t, layer, _main_col_block, MAIN_WIDTH // PROJ_TN, ((BF16, False),), "proj_main",
                    first_tile_scale=q_scale)
    gates = _gates(xn, w_t, layer)
    log_a = _gate(xn, w_t, layer, w_g2, b_gate_l)
    return k32, k16, v32, v16, main, gates, log_a, k_means


def kernel(x_prompt, x_sample, cache_k, cache_v, state_gla, page_table, norm_g, w_in, w_gate2, b_gate,
           gla_norm_g, w_branch_a, w_branch_b, b_merge, w_out, final_norm_g):
    depth = w_in.shape[0]
    bp, seq, _ = x_prompt.shape
    dec_b, n_new, _ = x_sample.shape
    assert n_new <= SAMPLE_ROWS and seq % MOBA_BLOCK == 0
    assert cache_k.shape[2] * page_table.shape[1] % MOBA_BLOCK == 0
    assert (W_ZA // PROJ_TN, W_ZB // PROJ_TN) == (3, 6) and W_QB % PROJ_TN == 0

    xp = x_prompt.reshape(bp * seq, D_MODEL)
    xs = jnp.pad(x_sample, ((0, 0), (0, SAMPLE_ROWS - n_new), (0, 0))).reshape(dec_b * SAMPLE_ROWS, D_MODEL)
    xnp = _rmsnorm(xp, norm_g[0], BF16)
    xns = _rmsnorm(xs, norm_g[0], BF16)
    zero_state = jnp.zeros((bp, GLA_HEADS, GLA_DK, GLA_DV), F32)
    w_t = jnp.swapaxes(w_in, 1, 2)

    sp_l, ks_l, vs_l, ss_l = [], [], [], []
    kp_all = vp_all = None
    for l in range(depth):
        w_g2 = jnp.pad(w_gate2[l].astype(BF16), ((0, LANES - GLA_GATE_RANK), (0, 0)))
        wa = w_branch_a[l].astype(BF16)
        wb = w_branch_b[l].astype(BF16)
        wo = w_out[l].astype(BF16)
        last = l == depth - 1
        g_next = final_norm_g if last else norm_g[l + 1]

        kp_all, k16, vp_all, vt16, main, gates, log_a, (k_means,) = _token_proj(
            xnp, w_t, l, w_g2, b_gate[l], True, depth, kp_all, vp_all)
        means = k_means.reshape(bp, seq // MOBA_BLOCK, MOBA_WIDTH)
        oa, pmeans = _moba_prompt(main, k16, vt16, means, cache_k, page_table, l, bp, seq)
        ob, sp = _gla(main, log_a, zero_state, gla_norm_g[l], bp, seq, MOBA_BLOCK, GLA_SUB, MOBA_BLOCK)
        merged = _merge(oa, ob, wa, wb, gates, b_merge[l])
        outs = _out_proj(merged, wo, xp, g_next, not last)
        if last:
            (yp,) = outs
        else:
            xp, xnp = outs
        sp_l.append(sp)

        k32, k16, v32, v16, main, gates, log_a, _ = _token_proj(xns, w_t, l, w_g2, b_gate[l], False)
        sel = _sample_select(main, pmeans, dec_b)
        sel = sel[:, :n_new, :MOBA_HEADS * MOBA_TOPK].reshape(dec_b, n_new * MOBA_HEADS * MOBA_TOPK)
        oa = _moba_sample(main, k16, v16, cache_k, cache_v, page_table, sel, l, n_new)
        ob, ss = _gla(main, log_a, state_gla[l], gla_norm_g[l], dec_b, SAMPLE_ROWS, SAMPLE_ROWS,
                      SAMPLE_ROWS, n_new)
        merged = _merge(oa, ob, wa, wb, gates, b_merge[l])
        outs = _out_proj(merged, wo, xs, g_next, not last)
        if last:
            (ys,) = outs
        else:
            xs, xns = outs
        rows = lambda a: a.reshape(dec_b, SAMPLE_ROWS, MOBA_HEADS, HEAD_DIM)[:, :n_new]
        ks_l.append(rows(k32))
        vs_l.append(rows(v32))
        ss_l.append(ss)

    y_prompt = yp.reshape(bp, seq, D_MODEL)
    y_sample = ys.reshape(dec_b, SAMPLE_ROWS, D_MODEL)[:, :n_new]
    k_prompt = kp_all.reshape(depth, bp, seq, MOBA_HEADS, HEAD_DIM)
    v_prompt = vp_all.reshape(depth, bp, seq, MOBA_HEADS, HEAD_DIM)
    return (y_prompt, y_sample, k_prompt, v_prompt, jnp.stack(sp_l),
            jnp.stack(ks_l), jnp.stack(vs_l), jnp.stack(ss_l))
```

```python
import functools
import math

import jax
import jax.numpy as jnp
from jax import lax
from jax.experimental import pallas as pl
from jax.experimental.pallas import tpu as pltpu

D_MODEL = 2048
MOBA_HEADS = 8
HEAD_DIM = 128
MOBA_BLOCK = 256
MOBA_TOPK = 3
GLA_HEADS = 4
GLA_DK = 128
GLA_DV = 256
GLA_GATE_RANK = 16
GLA_TAU = 16.0
GLA_SUB = 32
RMS_EPS = 1e-6
NEG_INF = -1e30
MOBA_WIDTH = MOBA_HEADS * HEAD_DIM
GLA_K_WIDTH = GLA_HEADS * GLA_DK
GLA_V_WIDTH = GLA_HEADS * GLA_DV

W_QA = 0
W_KA = W_QA + MOBA_WIDTH
W_VA = W_KA + MOBA_WIDTH
W_ZA = W_VA + MOBA_WIDTH
W_QB = W_ZA + MOBA_WIDTH
W_KB = W_QB + GLA_K_WIDTH
W_VB = W_KB + GLA_K_WIDTH
W_ZB = W_VB + GLA_V_WIDTH
W_ALR = W_ZB + GLA_V_WIDTH
W_GM = W_ALR + GLA_GATE_RANK

COL_QA = 0
COL_ZA = COL_QA + MOBA_WIDTH
COL_QB = COL_ZA + MOBA_WIDTH
COL_KB = COL_QB + GLA_K_WIDTH
COL_VB = COL_KB + GLA_K_WIDTH
COL_ZB = COL_VB + GLA_V_WIDTH
MAIN_WIDTH = COL_ZB + GLA_V_WIDTH

LANES = 128
SUBLANES = 8
SAMPLE_ROWS = SUBLANES
PROJ_TN = 1024
PROJ_TM = 1024
VMEM_LIMIT = 56 * 1024 * 1024

F32 = jnp.float32
BF16 = jnp.bfloat16
NT_DIMS = (((1,), (1,)), ((), ()))
TN_DIMS = (((0,), (0,)), ((), ()))
LOG2E = math.log2(math.e)


def _params(*sem):
    return pltpu.CompilerParams(dimension_semantics=sem, vmem_limit_bytes=VMEM_LIMIT)


def _silu(z):
    return z * (1.0 / (1.0 + jnp.exp(-z)))


def _sigmoid(z):
    return 1.0 / (1.0 + jnp.exp(-z))


def _log_forget_gate(xn, w1_ref, w2_ref, b_ref):
    a = lax.dot_general(xn, w1_ref[...].astype(BF16), NT_DIMS, preferred_element_type=F32).astype(BF16)
    y = jnp.dot(a, w2_ref[...], preferred_element_type=F32) + b_ref[...]
    log_sig = jnp.minimum(y, 0.0) - jnp.log(1.0 + jnp.exp(-jnp.abs(y)))
    return log_sig * (1.0 / GLA_TAU)


def _gate_operands(w_t, layer, w2, b):
    assert W_ALR % LANES == 0
    specs = [pl.BlockSpec((None, LANES, w_t.shape[2]), lambda i: (layer, W_ALR // LANES, 0)),
             pl.BlockSpec((LANES, GLA_K_WIDTH), lambda i: (0, 0)),
             pl.BlockSpec((1, GLA_K_WIDTH), lambda i: (0, 0))]
    return specs, [w_t, w2, b.reshape(1, GLA_K_WIDTH)]


def _rmsnorm_kernel(x_ref, g_ref, w1_ref, w2_ref, b_ref, o_ref, la_ref):
    x = x_ref[...]
    r = lax.rsqrt(jnp.mean(x * x, axis=-1, keepdims=True) + RMS_EPS)
    xn = (x * r * g_ref[...]).astype(o_ref.dtype)
    o_ref[...] = xn
    la_ref[...] = _log_forget_gate(xn, w1_ref, w2_ref, b_ref)


def _rmsnorm(x, g, gate):
    m, d = x.shape
    tm = min(m, 512)
    gate_specs, gate_operands = _gate_operands(*gate)
    return pl.pallas_call(
        _rmsnorm_kernel,
        grid=(pl.cdiv(m, tm),),
        in_specs=[pl.BlockSpec((tm, d), lambda i: (i, 0)),
                  pl.BlockSpec((1, d), lambda i: (0, 0))] + gate_specs,
        out_specs=[pl.BlockSpec((tm, d), lambda i: (i, 0)),
                   pl.BlockSpec((tm, GLA_K_WIDTH), lambda i: (i, 0))],
        out_shape=[jax.ShapeDtypeStruct((m, d), BF16),
                   jax.ShapeDtypeStruct((m, GLA_K_WIDTH), F32)],
        compiler_params=_params("parallel"),
        name="rmsnorm",
    )(x, g.reshape(1, d), *gate_operands)


def _proj_kernel(x_ref, w_ref, *refs, transposed, fill_layer, first_tile_scale):
    *o_refs, w_bf = refs[-(len(transposed) + 1):]

    @pl.when(pl.program_id(1) == 0)
    def _cast_weight_tile():
        w_bf[...] = w_ref[...].astype(BF16)

    acc = lax.dot_general(x_ref[...], w_bf[...], NT_DIMS, preferred_element_type=F32)
    if first_tile_scale is not None:
        acc = acc * jnp.where(pl.program_id(0) == 0, first_tile_scale, 1.0)
    for pos, (o_ref, tr) in enumerate(zip(o_refs, transposed)):
        if tr == "block_mean":
            for r in range(o_ref.shape[0]):
                rows = acc[r * MOBA_BLOCK:(r + 1) * MOBA_BLOCK, :]
                o_ref[r] = jnp.sum(rows, axis=0, keepdims=True) * (1.0 / MOBA_BLOCK)
        elif tr:
            for r in range(o_ref.shape[0]):
                rows = acc[r * MOBA_BLOCK:(r + 1) * MOBA_BLOCK, :]
                o_ref[r] = jnp.transpose(rows).astype(o_ref.dtype)
        elif pos == 0 and fill_layer is not None:
            for d in range(o_ref.shape[0]):
                o_ref[d] = acc.astype(o_ref.dtype) if d == fill_layer else jnp.zeros(acc.shape, o_ref.dtype)
        else:
            o_ref[...] = acc.astype(o_ref.dtype)


def _proj(xn, w_t, layer, col_block, n_blocks, outs, name, stack_depth=1, stacked=None, first_tile_scale=None):
    m, k = xn.shape
    tm = min(m, PROJ_TM)
    tn = PROJ_TN
    n = n_blocks * tn
    ni = pl.cdiv(m, tm)
    fill_layer = layer if stack_depth > 1 and stacked is None else None
    out_specs, out_shape = [], []
    for pos, (dt, tr) in enumerate(outs):
        if tr == "block_mean":
            out_specs.append(pl.BlockSpec((tm // MOBA_BLOCK, 1, tn), lambda j, i: (i, 0, j)))
            out_shape.append(jax.ShapeDtypeStruct((m // MOBA_BLOCK, 1, n), dt))
        elif tr:
            out_specs.append(pl.BlockSpec((tm // MOBA_BLOCK, tn, MOBA_BLOCK), lambda j, i: (i, j, 0)))
            out_shape.append(jax.ShapeDtypeStruct((m // MOBA_BLOCK, n, MOBA_BLOCK), dt))
        elif pos == 0 and stack_depth > 1:
            if stacked is None:
                out_specs.append(pl.BlockSpec((stack_depth, tm, tn), lambda j, i: (0, i, j)))
            else:
                out_specs.append(pl.BlockSpec((None, tm, tn), lambda j, i: (layer, i, j)))
            out_shape.append(jax.ShapeDtypeStruct((stack_depth, m, n), dt))
        else:
            out_specs.append(pl.BlockSpec((tm, tn), lambda j, i: (i, j)))
            out_shape.append(jax.ShapeDtypeStruct((m, n), dt))
    in_specs = [pl.BlockSpec((tm, k), lambda j, i: (i, 0)),
                pl.BlockSpec((None, tn, k), lambda j, i: (layer, col_block(j), 0))]
    operands = [xn, w_t]
    aliases = {}
    if stacked is not None:
        in_specs.append(pl.BlockSpec(memory_space=pl.ANY))
        operands.append(stacked)
        aliases = {2: 0}
    return pl.pallas_call(
        functools.partial(_proj_kernel, transposed=tuple(tr for _, tr in outs), fill_layer=fill_layer,
                          first_tile_scale=first_tile_scale),
        grid=(n_blocks, ni),
        in_specs=in_specs,
        out_specs=out_specs,
        out_shape=out_shape,
        scratch_shapes=[pltpu.VMEM((tn, k), BF16)],
        input_output_aliases=aliases,
        compiler_params=_params("parallel", "arbitrary"),
        name=name,
    )(*operands)


def _gates_kernel(x_ref, wa_ref, wb_ref, o_ref, w_bf, *, shift):
    @pl.when(pl.program_id(1) == 0)
    def _cast_weight_tile():
        w = jnp.concatenate([wa_ref[shift:, :], wb_ref[...]], axis=0)
        w_bf[...] = w.astype(BF16)

    o_ref[...] = lax.dot_general(x_ref[...], w_bf[...], NT_DIMS, preferred_element_type=F32).astype(o_ref.dtype)


def _gates(xn, w_t, layer):
    m, k = xn.shape
    tm = min(m, PROJ_TM)
    tn = PROJ_TN
    base = W_GM // tn * tn
    shift = W_GM - base
    assert shift % SUBLANES == 0 and tn % shift == 0
    return pl.pallas_call(
        functools.partial(_gates_kernel, shift=shift),
        grid=(2 * D_MODEL // tn, pl.cdiv(m, tm)),
        in_specs=[pl.BlockSpec((tm, k), lambda j, i: (i, 0)),
                  pl.BlockSpec((None, tn, k), lambda j, i: (layer, base // tn + j, 0)),
                  pl.BlockSpec((None, shift, k), lambda j, i: (layer, (base + (j + 1) * tn) // shift, 0))],
        out_specs=pl.BlockSpec((tm, tn), lambda j, i: (i, j)),
        out_shape=jax.ShapeDtypeStruct((m, 2 * D_MODEL), BF16),
        scratch_shapes=[pltpu.VMEM((tn, k), BF16)],
        compiler_params=_params("parallel", "arbitrary"),
        name="proj_gates",
    )(xn, w_t, w_t)


def _merge_kernel(oa_ref, ob_ref, wa_ref, wb_ref, ga_ref, gb_ref, ba_ref, bb_ref, o_ref, wa_bf, wb_bf):
    @pl.when(pl.program_id(1) == 0)
    def _cast_weight_tiles():
        wa_bf[...] = wa_ref[...].astype(BF16)
        wb_bf[...] = wb_ref[...].astype(BF16)

    ya = jnp.dot(oa_ref[...], wa_bf[...], preferred_element_type=F32)
    yb = jnp.dot(ob_ref[...], wb_bf[...], preferred_element_type=F32)
    ga = _sigmoid(ga_ref[...].astype(F32) + ba_ref[...])
    gb = _sigmoid(gb_ref[...].astype(F32) + bb_ref[...])
    o_ref[...] = (ga * ya + gb * yb).astype(o_ref.dtype)


def _merge(oa, ob, w_a, w_b, layer, gates, b_merge):
    m = oa.shape[0]
    tm = min(m, 1024)
    tn = 1024
    nj = D_MODEL // tn
    bm = b_merge.reshape(1, 2 * D_MODEL)
    return pl.pallas_call(
        _merge_kernel,
        grid=(nj, pl.cdiv(m, tm)),
        in_specs=[pl.BlockSpec((tm, MOBA_WIDTH), lambda j, i: (i, 0)),
                  pl.BlockSpec((tm, GLA_V_WIDTH), lambda j, i: (i, 0)),
                  pl.BlockSpec((None, MOBA_WIDTH, tn), lambda j, i: (layer, 0, j)),
                  pl.BlockSpec((None, GLA_V_WIDTH, tn), lambda j, i: (layer, 0, j)),
                  pl.BlockSpec((tm, tn), lambda j, i: (i, j)),
                  pl.BlockSpec((tm, tn), lambda j, i: (i, nj + j)),
                  pl.BlockSpec((1, tn), lambda j, i: (0, j)),
                  pl.BlockSpec((1, tn), lambda j, i: (0, nj + j))],
        out_specs=pl.BlockSpec((tm, tn), lambda j, i: (i, j)),
        out_shape=jax.ShapeDtypeStruct((m, D_MODEL), BF16),
        scratch_shapes=[pltpu.VMEM((MOBA_WIDTH, tn), BF16), pltpu.VMEM((GLA_V_WIDTH, tn), BF16)],
        compiler_params=_params("parallel", "arbitrary"),
        name="gated_merge",
    )(oa, ob, w_a, w_b, gates, gates, bm, bm)


def _out_kernel(m_ref, w_ref, x_ref, g_ref, *refs, emit_x):
    w_bf = refs[-1]

    @pl.when(pl.program_id(0) == 0)
    def _cast_weight():
        w_bf[...] = w_ref[...].astype(BF16)

    x_new = x_ref[...] + jnp.dot(m_ref[...], w_bf[...], preferred_element_type=F32)
    r = lax.rsqrt(jnp.mean(x_new * x_new, axis=-1, keepdims=True) + RMS_EPS)
    xn = x_new * r * g_ref[...]
    if emit_x:
        w1_ref, w2_ref, b_ref, x_out, xn_out, la_out = refs[:-1]
        x_out[...] = x_new
        xn_bf = xn.astype(xn_out.dtype)
        xn_out[...] = xn_bf
        la_out[...] = _log_forget_gate(xn_bf, w1_ref, w2_ref, b_ref)
    else:
        refs[0][...] = xn.astype(refs[0].dtype)


def _out_proj(merged, w_out, layer, x, g_next, gate_next):
    m = x.shape[0]
    tm = min(m, 512)
    row = pl.BlockSpec((tm, D_MODEL), lambda i: (i, 0))
    in_specs = [row,
                pl.BlockSpec((None, D_MODEL, D_MODEL), lambda i: (layer, 0, 0), pipeline_mode=pl.Buffered(1)),
                row,
                pl.BlockSpec((1, D_MODEL), lambda i: (0, 0))]
    operands = [merged, w_out, x, g_next.reshape(1, D_MODEL)]
    if gate_next is not None:
        gate_specs, gate_operands = _gate_operands(*gate_next)
        in_specs += gate_specs
        operands += gate_operands
        out_specs = [row, row, pl.BlockSpec((tm, GLA_K_WIDTH), lambda i: (i, 0))]
        out_shape = [jax.ShapeDtypeStruct((m, D_MODEL), F32),
                     jax.ShapeDtypeStruct((m, D_MODEL), BF16),
                     jax.ShapeDtypeStruct((m, GLA_K_WIDTH), F32)]
    else:
        out_specs = [row]
        out_shape = [jax.ShapeDtypeStruct((m, D_MODEL), F32)]
    return pl.pallas_call(
        functools.partial(_out_kernel, emit_x=gate_next is not None),
        grid=(pl.cdiv(m, tm),),
        in_specs=in_specs,
        out_specs=out_specs,
        out_shape=out_shape,
        scratch_shapes=[pltpu.VMEM((D_MODEL, D_MODEL), BF16)],
        compiler_params=_params("arbitrary"),
        name="out_proj",
    )(*operands)


def _top_k_mask(s, idx, n_cand, axis):
    sel = jnp.zeros(s.shape, F32)
    for _ in range(MOBA_TOPK):
        mx = jnp.max(s, axis=axis, keepdims=True)
        cand = jnp.where((s == mx) & (mx > -jnp.inf), idx, float(n_cand))
        first = jnp.min(cand, axis=axis, keepdims=True)
        pick = idx == first
        sel = jnp.where(pick, 1.0, sel)
        s = jnp.where(pick, -jnp.inf, s)
    return sel


def _moba_prompt_kernel(pt_ref, q_ref, z_ref, k_ref, vt_ref, mean_ref, ck_ref, o_ref, pm_ref,
                        sel_scr, acc_scr, lg_scr, page_buf, page_sem, *, nblk, heads, group, layer, pages_per_step):
    j = pl.program_id(2)

    step = (pl.program_id(0) * pl.num_programs(1) + pl.program_id(1)) * nblk + j
    n_steps = pl.num_programs(0) * pl.num_programs(1) * nblk
    pages_per_seq = pt_ref.shape[1]
    pages_per_block = MOBA_BLOCK // page_buf.shape[2]

    def page_copies(s, slot):
        first = s * pages_per_step
        seq = first // pages_per_seq
        pg0 = first % pages_per_seq
        return [pltpu.make_async_copy(ck_ref.at[layer, pt_ref[seq, pg0 + p]], page_buf.at[slot, p],
                                      page_sem.at[slot]) for p in range(pages_per_step)]

    @pl.when(step == 0)
    def _first_pages():
        for cp in page_copies(step, 0):
            cp.start()

    @pl.when(step + 1 < n_steps)
    def _next_pages():
        for cp in page_copies(step + 1, (step + 1) % 2):
            cp.start()

    for cp in page_copies(step, step % 2):
        cp.wait()
    first_page = step * pages_per_step
    for i in range(pages_per_step // pages_per_block):
        tot = None
        for pg in range(pages_per_block):
            part = jnp.sum(page_buf[step % 2, i * pages_per_block + pg], axis=0)
            tot = part if tot is None else tot + part
        pm_ref[first_page // pages_per_seq, (first_page % pages_per_seq) // pages_per_block + i] = (
            tot * (1.0 / MOBA_BLOCK))

    ones_rows = jnp.ones((MOBA_DENOM_ROWS, MOBA_BLOCK), BF16)
    lanes = [slice(h * HEAD_DIM, (h + 1) * HEAD_DIM) for h in range(heads)]
    qs = [q_ref[:, ln] for ln in lanes]

    def logits(h, n):
        r = pl.multiple_of(n * MOBA_BLOCK, MOBA_BLOCK)
        return lax.dot_general(k_ref[pl.ds(r, MOBA_BLOCK), lanes[h]], qs[h], NT_DIMS, preferred_element_type=F32)

    def weighted_values(h, n, p):
        lhs = jnp.concatenate([vt_ref[n, lanes[h], :], ones_rows], axis=0)
        return jnp.dot(lhs, p, preferred_element_type=F32)

    def trip_blocks(i):
        return [jnp.minimum(group * i + g, nblk - 1) for g in range(group)]

    scores = [lax.dot_general(mean_ref[:, lanes[h]].astype(BF16), qs[h], NT_DIMS, preferred_element_type=F32)
              for h in range(heads)]
    blk = lax.broadcasted_iota(jnp.int32, scores[0].shape, 0)
    for h in range(heads):
        sel_scr[h] = _top_k_mask(jnp.where(blk < j, scores[h], -jnp.inf), blk.astype(F32), nblk, 0)

    own = [logits(h, j) for h in range(heads)]
    for h in range(heads):
        for g, n in enumerate(trip_blocks(0)):
            lg_scr[h * group + g] = logits(h, n)
    key = lax.broadcasted_iota(jnp.int32, own[0].shape, 0)
    qry = lax.broadcasted_iota(jnp.int32, own[0].shape, 1)
    m0, p0 = [], []
    for h in range(heads):
        lg = jnp.where(key <= qry, own[h], NEG_INF)
        m = jnp.max(lg, axis=0, keepdims=True)
        m0.append(m)
        p0.append(jnp.exp2(lg - m).astype(BF16))
    for h in range(heads):
        acc_scr[h] = weighted_values(h, j, p0[h])

    def past_blocks(i, carry):
        ms = carry
        ns = trip_blocks(i)
        ns_next = trip_blocks(i + 1)
        new_m = []
        for h in range(heads):
            lgs = [lg_scr[h * group + g] for g in range(group)]
            for g, n in enumerate(ns_next):
                lg_scr[h * group + g] = logits(h, n)
            chosen = [(sel_scr[h, pl.ds(n, 1), :] > 0.0) & (group * i + g < j) for g, n in enumerate(ns)]
            m_new = ms[h]
            for lg, ch in zip(lgs, chosen):
                m_new = jnp.maximum(m_new, jnp.where(ch, jnp.max(lg, axis=0, keepdims=True), NEG_INF))
            alpha = jnp.exp2(ms[h] - m_new)
            pv = None
            for lg, ch, n in zip(lgs, chosen, ns):
                p = jnp.exp2(lg - jnp.where(ch, m_new, -NEG_INF))
                part = weighted_values(h, n, p.astype(BF16))
                pv = part if pv is None else pv + part
            acc_scr[h] = alpha * acc_scr[h] + pv
            new_m.append(m_new)
        return tuple(new_m)

    lax.fori_loop(0, (j + group - 1) // group, past_blocks, tuple(m0))

    for h in range(heads):
        z = z_ref[:, lanes[h]].astype(F32)
        out_t = acc_scr[h, :HEAD_DIM, :] / acc_scr[h, HEAD_DIM:HEAD_DIM + 1, :]
        o_ref[:, lanes[h]] = (jnp.transpose(out_t) * _silu(z)).astype(o_ref.dtype)


MOBA_HEADS_PER_STEP = 4
MOBA_DENOM_ROWS = 16
MOBA_BLOCKS_PER_TRIP = 2


def _moba_prompt(main, k16, vt16, means, cache_k, page_table, layer, batch, seq):
    nblk = seq // MOBA_BLOCK
    hp = MOBA_HEADS_PER_STEP
    width = hp * HEAD_DIM
    dec_b, n_pages = page_table.shape
    page = cache_k.shape[2]
    pages_per_block = MOBA_BLOCK // page
    n_steps = batch * (MOBA_HEADS // hp) * nblk
    pages_per_step = dec_b * n_pages // n_steps
    assert pages_per_step * n_steps == dec_b * n_pages, "pages must split evenly over the grid steps"
    assert n_pages % pages_per_step == 0 and pages_per_step % pages_per_block == 0
    n_past_blocks = n_pages // pages_per_block
    qmap = lambda b, h, j, pt: (b * nblk + j, COL_QA // width + h)
    zmap = lambda b, h, j, pt: (b * nblk + j, COL_ZA // width + h)
    return pl.pallas_call(
        functools.partial(_moba_prompt_kernel, nblk=nblk, heads=hp, group=MOBA_BLOCKS_PER_TRIP,
                          layer=layer, pages_per_step=pages_per_step),
        grid_spec=pltpu.PrefetchScalarGridSpec(
            num_scalar_prefetch=1,
            grid=(batch, MOBA_HEADS // hp, nblk),
            in_specs=[pl.BlockSpec((MOBA_BLOCK, width), qmap),
                      pl.BlockSpec((MOBA_BLOCK, width), zmap),
                      pl.BlockSpec((seq, width), lambda b, h, j, pt: (b, h)),
                      pl.BlockSpec((nblk, width, MOBA_BLOCK), lambda b, h, j, pt: (b, h, 0)),
                      pl.BlockSpec((None, nblk, width), lambda b, h, j, pt: (b, 0, h)),
                      pl.BlockSpec(memory_space=pl.ANY)],
            out_specs=[pl.BlockSpec((MOBA_BLOCK, width), lambda b, h, j, pt: (b * nblk + j, h)),
                       pl.BlockSpec((dec_b, n_past_blocks, MOBA_HEADS, HEAD_DIM), lambda b, h, j, pt: (0, 0, 0, 0))],
            scratch_shapes=[pltpu.VMEM((hp, nblk, MOBA_BLOCK), F32),
                            pltpu.VMEM((hp, HEAD_DIM + MOBA_DENOM_ROWS, MOBA_BLOCK), F32),
                            pltpu.VMEM((hp * MOBA_BLOCKS_PER_TRIP, MOBA_BLOCK, MOBA_BLOCK), F32),
                            pltpu.VMEM((2, pages_per_step, page, MOBA_HEADS, HEAD_DIM), F32),
                            pltpu.SemaphoreType.DMA((2,))],
        ),
        out_shape=[jax.ShapeDtypeStruct((batch * seq, MOBA_WIDTH), BF16),
                   jax.ShapeDtypeStruct((dec_b, n_past_blocks, MOBA_HEADS, HEAD_DIM), F32)],
        compiler_params=_params("arbitrary", "arbitrary", "arbitrary"),
        name="moba_prompt",
    )(page_table, main, main, k16, vt16, means, cache_k)


def _gla_kernel(q_ref, k_ref, v_ref, g_ref, z_ref, s0_ref, gn_ref, o_ref, sout_ref, s_scr,
                *, chunk, sub, n_valid, heads):
    t = pl.program_id(2)

    @pl.when(t == 0)
    def _load_state():
        s_scr[...] = s0_ref[...]

    hs = range(heads)
    dk = [slice(h * GLA_DK, (h + 1) * GLA_DK) for h in hs]
    dv = [slice(h * GLA_DV, (h + 1) * GLA_DV) for h in hs]
    ri = lax.broadcasted_iota(jnp.int32, (chunk, chunk), 0)
    ci = lax.broadcasted_iota(jnp.int32, (chunk, chunk), 1)
    causal = ci <= ri
    tri = jnp.where(causal, 1.0, 0.0).astype(BF16)
    live = lax.broadcasted_iota(jnp.int32, (chunk, GLA_DK), 0) < n_valid
    row_sub = lax.broadcasted_iota(jnp.int32, (chunk, GLA_DK), 0) // sub
    nsub = chunk // sub

    ks, b3s = [], []
    for h in hs:
        k = k_ref[:, dk[h]].astype(F32)
        g = g_ref[:, dk[h]]
        if n_valid < chunk:
            g = jnp.where(live, g, 0.0)
            k = jnp.where(live, k, 0.0)
        g_hi = g.astype(BF16)
        g_r1 = g - g_hi.astype(F32)
        g_mid = g_r1.astype(BF16)
        g_lo = (g_r1 - g_mid.astype(F32)).astype(BF16)
        ks.append(k)
        b3s.append(jnp.dot(tri, jnp.concatenate([g_hi, g_mid, g_lo], axis=1), preferred_element_type=F32))

    bs, qs, scores = [], [], []
    for h in hs:
        b3 = b3s[h]
        b = (b3[:, :GLA_DK] + b3[:, GLA_DK:2 * GLA_DK]) + b3[:, 2 * GLA_DK:]
        q = q_ref[:, dk[h]].astype(F32) * (GLA_DK ** -0.5)
        ends = [b[(i + 1) * sub - 1:(i + 1) * sub] for i in range(nsub)]
        own_end = jnp.concatenate([jnp.broadcast_to(e, (sub, GLA_DK)) for e in ends], axis=0)
        k_rel = ks[h] * jnp.exp(own_end - b)
        q_parts, k_parts = [], []
        for i in range(nsub):
            q_parts.append(jnp.where(row_sub >= i, q * jnp.exp(b - ends[i]), 0.0).astype(BF16))
            k_parts.append(jnp.where(row_sub == i, k_rel, 0.0).astype(BF16))
        q_cat = jnp.concatenate(q_parts, axis=1) if nsub > 1 else q_parts[0]
        k_cat = jnp.concatenate(k_parts, axis=1) if nsub > 1 else k_parts[0]
        scores.append(lax.dot_general(q_cat, k_cat, NT_DIMS, preferred_element_type=F32))
        bs.append(b)
        qs.append(q)

    outs, new_states = [], []
    for h in hs:
        b = bs[h]
        b_last = b[chunk - 1:chunk]
        v = v_ref[:, dv[h]]
        state = s_scr[h]
        a = jnp.where(causal, scores[h], 0.0)
        outs.append(jnp.dot(a.astype(BF16), v, preferred_element_type=F32)
                    + jnp.dot((qs[h] * jnp.exp(b)).astype(BF16), state.astype(BF16), preferred_element_type=F32))
        k_dec = (ks[h] * jnp.exp(b_last - b)).astype(BF16)
        update = lax.dot_general(k_dec, v, TN_DIMS, preferred_element_type=F32)
        decay_t = jnp.transpose(jnp.broadcast_to(jnp.exp(b_last), (GLA_DK, GLA_DK)))
        new_states.append(jnp.concatenate([decay_t] * (GLA_DV // GLA_DK), axis=1) * state + update)

    for h in hs:
        s_scr[h] = new_states[h]
        o = outs[h]
        r = lax.rsqrt(jnp.mean(o * o, axis=-1, keepdims=True) + RMS_EPS)
        z = z_ref[:, dv[h]].astype(F32)
        o_ref[:, dv[h]] = (o * r * gn_ref[h] * _silu(z)).astype(o_ref.dtype)

    @pl.when(t == pl.num_programs(2) - 1)
    def _store_state():
        for h in hs:
            sout_ref[h] = new_states[h]


GLA_HEADS_PER_STEP = 4


def _gla(main, log_a, s0, gn, batch, rows_per_seq, chunk, sub, n_valid):
    nt = rows_per_seq // chunk
    m = main.shape[0]
    hp = GLA_HEADS_PER_STEP
    wk, wv = hp * GLA_DK, hp * GLA_DV
    rmap = lambda off: (lambda b, h, t: (b * nt + t, off + h))
    smap = lambda b, h, t: (b, h, 0, 0)
    return pl.pallas_call(
        functools.partial(_gla_kernel, chunk=chunk, sub=sub, n_valid=n_valid, heads=hp),
        grid=(batch, GLA_HEADS // hp, nt),
        in_specs=[pl.BlockSpec((chunk, wk), rmap(COL_QB // wk)),
                  pl.BlockSpec((chunk, wk), rmap(COL_KB // wk)),
                  pl.BlockSpec((chunk, wv), rmap(COL_VB // wv)),
                  pl.BlockSpec((chunk, wk), rmap(0)),
                  pl.BlockSpec((chunk, wv), rmap(COL_ZB // wv)),
                  pl.BlockSpec((None, hp, GLA_DK, GLA_DV), smap),
                  pl.BlockSpec((hp, 1, GLA_DV), lambda b, h, t: (h, 0, 0))],
        out_specs=[pl.BlockSpec((chunk, wv), rmap(0)),
                   pl.BlockSpec((None, hp, GLA_DK, GLA_DV), smap)],
        out_shape=[jax.ShapeDtypeStruct((m, GLA_V_WIDTH), BF16),
                   jax.ShapeDtypeStruct((batch, GLA_HEADS, GLA_DK, GLA_DV), F32)],
        scratch_shapes=[pltpu.VMEM((hp, GLA_DK, GLA_DV), F32)],
        compiler_params=_params("parallel", "parallel", "arbitrary"),
        name="gla",
    )(main, main, main, log_a, main, s0, gn.reshape(GLA_HEADS, 1, GLA_DV))


def _sample_select_kernel(q_ref, mean_ref, o_ref, *, n_blocks):
    q = q_ref[...]
    means = mean_ref[...].astype(BF16)
    rows = q.shape[0]
    ncol = n_blocks * MOBA_HEADS
    col = lax.broadcasted_iota(jnp.int32, (rows, ncol), 1)
    col_head = col % MOBA_HEADS
    col_f = col.astype(F32)
    s_all = jnp.zeros((rows, ncol), F32)
    for h in range(MOBA_HEADS):
        s_h = lax.dot_general(q[:, h * HEAD_DIM:(h + 1) * HEAD_DIM], means, NT_DIMS, preferred_element_type=F32)
        s_all = jnp.where(col_head == h, s_h, s_all)
    lane = lax.broadcasted_iota(jnp.int32, o_ref.shape, 1)
    out = jnp.zeros(o_ref.shape, jnp.int32)
    for h in range(MOBA_HEADS):
        s = jnp.where(col_head == h, s_all, -jnp.inf)
        for t in range(MOBA_TOPK):
            mx = jnp.max(s, axis=1, keepdims=True)
            cand = jnp.where((s == mx) & (mx > -jnp.inf), col_f, float(ncol))
            first = jnp.min(cand, axis=1, keepdims=True)
            s = jnp.where(col_f == first, -jnp.inf, s)
            blk = first.astype(jnp.int32) // MOBA_HEADS
            out = jnp.where(lane == h * MOBA_TOPK + t, blk, out)
    o_ref[...] = out


def _sample_select(main, means, dec_b):
    n_blocks = means.shape[1]
    means2 = means.reshape(dec_b, n_blocks * MOBA_HEADS, HEAD_DIM)
    return pl.pallas_call(
        functools.partial(_sample_select_kernel, n_blocks=n_blocks),
        grid=(dec_b,),
        in_specs=[pl.BlockSpec((SAMPLE_ROWS, MOBA_WIDTH), lambda b: (b, COL_QA // MOBA_WIDTH)),
                  pl.BlockSpec((None, n_blocks * MOBA_HEADS, HEAD_DIM), lambda b: (b, 0, 0))],
        out_specs=pl.BlockSpec((None, SAMPLE_ROWS, LANES), lambda b: (b, 0, 0)),
        out_shape=jax.ShapeDtypeStruct((dec_b, SAMPLE_ROWS, LANES), jnp.int32),
        compiler_params=_params("parallel"),
        name="moba_sample_select",
    )(main, means2)


def _moba_sample_kernel(pt_ref, sel_ref, q_ref, z_ref, ko_ref, vo_ref, ck_ref, cv_ref, o_ref,
                        kbuf, vbuf, sem, *, layer, n_new, page):
    n_heads = pl.num_programs(1)
    step = pl.program_id(0) * n_heads + pl.program_id(1)
    n_steps = pl.num_programs(0) * n_heads
    pages_per_block = MOBA_BLOCK // page
    n_slots = n_new * MOBA_TOPK * pages_per_block

    def page_copies(s):
        b, h = s // n_heads, s % n_heads
        buf = s % 2
        copies = []
        for slot in range(n_slots):
            qi, rest = divmod(slot, MOBA_TOPK * pages_per_block)
            rank, pg = divmod(rest, pages_per_block)
            blk = sel_ref[b, qi * (MOBA_HEADS * MOBA_TOPK) + h * MOBA_TOPK + rank]
            pid = pt_ref[b, blk * pages_per_block + pg]
            dst = pl.ds(slot * page, page)
            copies.append(pltpu.make_async_copy(ck_ref.at[layer, pid, :, h, :], kbuf.at[buf, dst, :],
                                                sem.at[buf, 0]))
            copies.append(pltpu.make_async_copy(cv_ref.at[layer, pid, :, h, :], vbuf.at[buf, dst, :],
                                                sem.at[buf, 1]))
        return copies

    @pl.when(step == 0)
    def _first_gather():
        for cp in page_copies(step):
            cp.start()

    @pl.when(step + 1 < n_steps)
    def _next_gather():
        for cp in page_copies(step + 1):
            cp.start()

    for cp in page_copies(step):
        cp.wait()

    scale = HEAD_DIM ** -0.5
    q = q_ref[...]
    rows = q.shape[0]
    n_keys = n_slots * page
    k_sel = kbuf[step % 2]
    v_sel = vbuf[step % 2]
    lg = lax.dot_general(q, k_sel.astype(BF16), NT_DIMS, preferred_element_type=F32) * scale
    row = lax.broadcasted_iota(jnp.int32, (rows, n_keys), 0)
    key_owner = lax.broadcasted_iota(jnp.int32, (rows, n_keys), 1) // (MOBA_TOPK * MOBA_BLOCK)
    lg = jnp.where(key_owner == row, lg, NEG_INF)
    lg_own = lax.dot_general(q, ko_ref[...], NT_DIMS, preferred_element_type=F32) * scale
    r2 = lax.broadcasted_iota(jnp.int32, lg_own.shape, 0)
    c2 = lax.broadcasted_iota(jnp.int32, lg_own.shape, 1)
    lg_own = jnp.where((c2 <= r2) & (c2 < n_new), lg_own, NEG_INF)
    m = jnp.maximum(jnp.max(lg, axis=1, keepdims=True), jnp.max(lg_own, axis=1, keepdims=True))
    p = jnp.exp(lg - m)
    p_own = jnp.exp(lg_own - m)
    denom = jnp.sum(p, axis=1, keepdims=True) + jnp.sum(p_own, axis=1, keepdims=True)
    acc = (jnp.dot(p.astype(BF16), v_sel.astype(BF16), preferred_element_type=F32)
           + jnp.dot(p_own.astype(BF16), vo_ref[...], preferred_element_type=F32))
    z = z_ref[...].astype(F32)
    o_ref[...] = (acc / denom * _silu(z)).astype(o_ref.dtype)


def _moba_sample(main, k16, v16, cache_k, cache_v, page_table, sel, layer, n_new):
    dec_b = page_table.shape[0]
    page = cache_k.shape[2]
    n_keys = n_new * MOBA_TOPK * MOBA_BLOCK
    tile = lambda off: pl.BlockSpec((SAMPLE_ROWS, HEAD_DIM), lambda b, h, pt, sl: (b, off + h))
    return pl.pallas_call(
        functools.partial(_moba_sample_kernel, layer=layer, n_new=n_new, page=page),
        grid_spec=pltpu.PrefetchScalarGridSpec(
            num_scalar_prefetch=2,
            grid=(dec_b, MOBA_HEADS),
            in_specs=[tile(COL_QA // HEAD_DIM), tile(COL_ZA // HEAD_DIM), tile(0), tile(0),
                      pl.BlockSpec(memory_space=pl.ANY), pl.BlockSpec(memory_space=pl.ANY)],
            out_specs=tile(0),
            scratch_shapes=[pltpu.VMEM((2, n_keys, HEAD_DIM), F32),
                            pltpu.VMEM((2, n_keys, HEAD_DIM), F32),
                            pltpu.SemaphoreType.DMA((2, 2))],
        ),
        out_shape=jax.ShapeDtypeStruct((dec_b * SAMPLE_ROWS, MOBA_WIDTH), BF16),
        compiler_params=_params("arbitrary", "arbitrary"),
        name="moba_sample",
    )(page_table, sel, main, main, k16, v16, cache_k, cache_v)


def _main_col_block(j):
    return jnp.where(j == 0, W_QA // PROJ_TN, j + (W_ZA // PROJ_TN - 1))


def _token_proj(xn, w_t, layer, prompt, stack_depth=1, k_stacked=None, v_stacked=None):
    k_outs = ((F32, False), (BF16, False)) + (((F32, "block_mean"),) if prompt else ())
    k32, k16, *k_means = _proj(xn, w_t, layer, lambda j: W_KA // PROJ_TN, 1, k_outs, "proj_k",
                               stack_depth, k_stacked)
    v32, v16 = _proj(xn, w_t, layer, lambda j: W_VA // PROJ_TN, 1, ((F32, False), (BF16, prompt)), "proj_v",
                     stack_depth, v_stacked)
    q_scale = (HEAD_DIM ** -0.5) * LOG2E if prompt else None
    (main,) = _proj(xn, w_t, layer, _main_col_block, MAIN_WIDTH // PROJ_TN, ((BF16, False),), "proj_main",
                    first_tile_scale=q_scale)
    gates = _gates(xn, w_t, layer)
    return k32, k16, v32, v16, main, gates, k_means


def kernel(x_prompt, x_sample, cache_k, cache_v, state_gla, page_table, norm_g, w_in, w_gate2, b_gate,
           gla_norm_g, w_branch_a, w_branch_b, b_merge, w_out, final_norm_g):
    depth = w_in.shape[0]
    bp, seq, _ = x_prompt.shape
    dec_b, n_new, _ = x_sample.shape
    assert n_new <= SAMPLE_ROWS and seq % MOBA_BLOCK == 0
    assert cache_k.shape[2] * page_table.shape[1] % MOBA_BLOCK == 0
    assert (W_ZA // PROJ_TN, W_ZB // PROJ_TN) == (3, 6) and W_QB % PROJ_TN == 0

    xp = x_prompt.reshape(bp * seq, D_MODEL)
    xs = jnp.pad(x_sample, ((0, 0), (0, SAMPLE_ROWS - n_new), (0, 0))).reshape(dec_b * SAMPLE_ROWS, D_MODEL)
    zero_state = jnp.zeros((bp, GLA_HEADS, GLA_DK, GLA_DV), F32)
    w_t = jnp.swapaxes(w_in, 1, 2)
    w_g2 = jnp.pad(w_gate2.astype(BF16), ((0, 0), (0, LANES - GLA_GATE_RANK), (0, 0)))
    gate = lambda l: (w_t, l, w_g2[l], b_gate[l])
    xnp, log_a_p = _rmsnorm(xp, norm_g[0], gate(0))
    xns, log_a_s = _rmsnorm(xs, norm_g[0], gate(0))

    sp_l, ks_l, vs_l, ss_l = [], [], [], []
    kp_all = vp_all = None
    for l in range(depth):
        last = l == depth - 1
        g_next = final_norm_g if last else norm_g[l + 1]
        gate_next = None if last else gate(l + 1)

        kp_all, k16, vp_all, vt16, main, gates, (k_means,) = _token_proj(
            xnp, w_t, l, True, depth, kp_all, vp_all)
        means = k_means.reshape(bp, seq // MOBA_BLOCK, MOBA_WIDTH)
        oa, pmeans = _moba_prompt(main, k16, vt16, means, cache_k, page_table, l, bp, seq)
        ob, sp = _gla(main, log_a_p, zero_state, gla_norm_g[l], bp, seq, MOBA_BLOCK, GLA_SUB, MOBA_BLOCK)
        merged = _merge(oa, ob, w_branch_a, w_branch_b, l, gates, b_merge[l])
        outs = _out_proj(merged, w_out, l, xp, g_next, gate_next)
        if last:
            (yp,) = outs
        else:
            xp, xnp, log_a_p = outs
        sp_l.append(sp)

        k32, k16, v32, v16, main, gates, _ = _token_proj(xns, w_t, l, False)
        sel = _sample_select(main, pmeans, dec_b)
        sel = sel[:, :n_new, :MOBA_HEADS * MOBA_TOPK].reshape(dec_b, n_new * MOBA_HEADS * MOBA_TOPK)
        oa = _moba_sample(main, k16, v16, cache_k, cache_v, page_table, sel, l, n_new)
        ob, ss = _gla(main, log_a_s, state_gla[l], gla_norm_g[l], dec_b, SAMPLE_ROWS, SAMPLE_ROWS,
                      SAMPLE_ROWS, n_new)
        merged = _merge(oa, ob, w_branch_a, w_branch_b, l, gates, b_merge[l])
        outs = _out_proj(merged, w_out, l, xs, g_next, gate_next)
        if last:
            (ys,) = outs
        else:
            xs, xns, log_a_s = outs
        rows = lambda a: a.reshape(dec_b, SAMPLE_ROWS, MOBA_HEADS, HEAD_DIM)[:, :n_new]
        ks_l.append(rows(k32))
        vs_l.append(rows(v32))
        ss_l.append(ss)

    y_prompt = yp.reshape(bp, seq, D_MODEL)
    y_sample = ys.reshape(dec_b, SAMPLE_ROWS, D_MODEL)[:, :n_new]
    k_prompt = kp_all.reshape(depth, bp, seq, MOBA_HEADS, HEAD_DIM)
    v_prompt = vp_all.reshape(depth, bp, seq, MOBA_HEADS, HEAD_DIM)
    return (y_prompt, y_sample, k_prompt, v_prompt, jnp.stack(sp_l),
            jnp.stack(ks_l), jnp.stack(vs_l), jnp.stack(ss_l))
```

```python
import functools
import math

import jax
import jax.numpy as jnp
from jax import lax
from jax.experimental import pallas as pl
from jax.experimental.pallas import tpu as pltpu

D_MODEL = 2048
MOBA_HEADS = 8
HEAD_DIM = 128
MOBA_BLOCK = 256
MOBA_TOPK = 3
GLA_HEADS = 4
GLA_DK = 128
GLA_DV = 256
GLA_GATE_RANK = 16
GLA_TAU = 16.0
GLA_SUB = 32
RMS_EPS = 1e-6
NEG_INF = -1e30
MOBA_WIDTH = MOBA_HEADS * HEAD_DIM
GLA_K_WIDTH = GLA_HEADS * GLA_DK
GLA_V_WIDTH = GLA_HEADS * GLA_DV

W_QA = 0
W_KA = W_QA + MOBA_WIDTH
W_VA = W_KA + MOBA_WIDTH
W_ZA = W_VA + MOBA_WIDTH
W_QB = W_ZA + MOBA_WIDTH
W_KB = W_QB + GLA_K_WIDTH
W_VB = W_KB + GLA_K_WIDTH
W_ZB = W_VB + GLA_V_WIDTH
W_ALR = W_ZB + GLA_V_WIDTH
W_GM = W_ALR + GLA_GATE_RANK

COL_QA = 0
COL_ZA = COL_QA + MOBA_WIDTH
COL_QB = COL_ZA + MOBA_WIDTH
COL_KB = COL_QB + GLA_K_WIDTH
COL_VB = COL_KB + GLA_K_WIDTH
COL_ZB = COL_VB + GLA_V_WIDTH
MAIN_WIDTH = COL_ZB + GLA_V_WIDTH

LANES = 128
SUBLANES = 8
SAMPLE_ROWS = SUBLANES
PROJ_TN = 1024
PROJ_TM = 1024
VMEM_LIMIT = 56 * 1024 * 1024

F32 = jnp.float32
BF16 = jnp.bfloat16
NT_DIMS = (((1,), (1,)), ((), ()))
TN_DIMS = (((0,), (0,)), ((), ()))
LOG2E = math.log2(math.e)


def _params(*sem):
    return pltpu.CompilerParams(dimension_semantics=sem, vmem_limit_bytes=VMEM_LIMIT)


def _silu(z):
    return z * (1.0 / (1.0 + jnp.exp(-z)))


def _sigmoid(z):
    return 1.0 / (1.0 + jnp.exp(-z))


def _log_forget_gate(xn, w1_ref, w2_ref, b_ref):
    a = lax.dot_general(xn, w1_ref[...].astype(BF16), NT_DIMS, preferred_element_type=F32).astype(BF16)
    y = jnp.dot(a, w2_ref[...], preferred_element_type=F32) + b_ref[...]
    log_sig = jnp.minimum(y, 0.0) - jnp.log(1.0 + jnp.exp(-jnp.abs(y)))
    return log_sig * (1.0 / GLA_TAU)


def _gate_operands(w_t, layer, w2, b):
    assert W_ALR % LANES == 0
    specs = [pl.BlockSpec((None, LANES, w_t.shape[2]), lambda i: (layer, W_ALR // LANES, 0)),
             pl.BlockSpec((LANES, GLA_K_WIDTH), lambda i: (0, 0)),
             pl.BlockSpec((1, GLA_K_WIDTH), lambda i: (0, 0))]
    return specs, [w_t, w2, b.reshape(1, GLA_K_WIDTH)]


def _rmsnorm_kernel(x_ref, g_ref, w1_ref, w2_ref, b_ref, o_ref, la_ref):
    x = x_ref[...]
    r = lax.rsqrt(jnp.mean(x * x, axis=-1, keepdims=True) + RMS_EPS)
    xn = (x * r * g_ref[...]).astype(o_ref.dtype)
    o_ref[...] = xn
    la_ref[...] = _log_forget_gate(xn, w1_ref, w2_ref, b_ref)


def _rmsnorm(x, g, gate):
    m, d = x.shape
    tm = min(m, 512)
    gate_specs, gate_operands = _gate_operands(*gate)
    return pl.pallas_call(
        _rmsnorm_kernel,
        grid=(pl.cdiv(m, tm),),
        in_specs=[pl.BlockSpec((tm, d), lambda i: (i, 0)),
                  pl.BlockSpec((1, d), lambda i: (0, 0))] + gate_specs,
        out_specs=[pl.BlockSpec((tm, d), lambda i: (i, 0)),
                   pl.BlockSpec((tm, GLA_K_WIDTH), lambda i: (i, 0))],
        out_shape=[jax.ShapeDtypeStruct((m, d), BF16),
                   jax.ShapeDtypeStruct((m, GLA_K_WIDTH), F32)],
        compiler_params=_params("parallel"),
        name="rmsnorm",
    )(x, g.reshape(1, d), *gate_operands)


def _proj_kernel(x_ref, w_ref, *refs, transposed, fill_layer, first_tile_scale):
    *o_refs, w_bf = refs[-(len(transposed) + 1):]

    @pl.when(pl.program_id(1) == 0)
    def _cast_weight_tile():
        w_bf[...] = w_ref[...].astype(BF16)

    acc = lax.dot_general(x_ref[...], w_bf[...], NT_DIMS, preferred_element_type=F32)
    if first_tile_scale is not None:
        acc = acc * jnp.where(pl.program_id(0) == 0, first_tile_scale, 1.0)
    for pos, (o_ref, tr) in enumerate(zip(o_refs, transposed)):
        if tr == "block_mean":
            for r in range(o_ref.shape[0]):
                rows = acc[r * MOBA_BLOCK:(r + 1) * MOBA_BLOCK, :]
                o_ref[r] = jnp.sum(rows, axis=0, keepdims=True) * (1.0 / MOBA_BLOCK)
        elif tr:
            for r in range(o_ref.shape[0]):
                rows = acc[r * MOBA_BLOCK:(r + 1) * MOBA_BLOCK, :]
                o_ref[r] = jnp.transpose(rows).astype(o_ref.dtype)
        elif pos == 0 and fill_layer is not None:
            for d in range(o_ref.shape[0]):
                o_ref[d] = acc.astype(o_ref.dtype) if d == fill_layer else jnp.zeros(acc.shape, o_ref.dtype)
        else:
            o_ref[...] = acc.astype(o_ref.dtype)


def _proj(xn, w_t, layer, col_block, n_blocks, outs, name, stack_depth=1, stacked=None, first_tile_scale=None):
    m, k = xn.shape
    tm = min(m, PROJ_TM)
    tn = PROJ_TN
    n = n_blocks * tn
    ni = pl.cdiv(m, tm)
    fill_layer = layer if stack_depth > 1 and stacked is None else None
    out_specs, out_shape = [], []
    for pos, (dt, tr) in enumerate(outs):
        if tr == "block_mean":
            out_specs.append(pl.BlockSpec((tm // MOBA_BLOCK, 1, tn), lambda j, i: (i, 0, j)))
            out_shape.append(jax.ShapeDtypeStruct((m // MOBA_BLOCK, 1, n), dt))
        elif tr:
            out_specs.append(pl.BlockSpec((tm // MOBA_BLOCK, tn, MOBA_BLOCK), lambda j, i: (i, j, 0)))
            out_shape.append(jax.ShapeDtypeStruct((m // MOBA_BLOCK, n, MOBA_BLOCK), dt))
        elif pos == 0 and stack_depth > 1:
            if stacked is None:
                out_specs.append(pl.BlockSpec((stack_depth, tm, tn), lambda j, i: (0, i, j)))
            else:
                out_specs.append(pl.BlockSpec((None, tm, tn), lambda j, i: (layer, i, j)))
            out_shape.append(jax.ShapeDtypeStruct((stack_depth, m, n), dt))
        else:
            out_specs.append(pl.BlockSpec((tm, tn), lambda j, i: (i, j)))
            out_shape.append(jax.ShapeDtypeStruct((m, n), dt))
    in_specs = [pl.BlockSpec((tm, k), lambda j, i: (i, 0)),
                pl.BlockSpec((None, tn, k), lambda j, i: (layer, col_block(j), 0))]
    operands = [xn, w_t]
    aliases = {}
    if stacked is not None:
        in_specs.append(pl.BlockSpec(memory_space=pl.ANY))
        operands.append(stacked)
        aliases = {2: 0}
    return pl.pallas_call(
        functools.partial(_proj_kernel, transposed=tuple(tr for _, tr in outs), fill_layer=fill_layer,
                          first_tile_scale=first_tile_scale),
        grid=(n_blocks, ni),
        in_specs=in_specs,
        out_specs=out_specs,
        out_shape=out_shape,
        scratch_shapes=[pltpu.VMEM((tn, k), BF16)],
        input_output_aliases=aliases,
        compiler_params=_params("parallel", "arbitrary"),
        name=name,
    )(*operands)


def _gates_kernel(x_ref, wa_ref, wb_ref, o_ref, w_bf, *, shift):
    @pl.when(pl.program_id(1) == 0)
    def _cast_weight_tile():
        w = jnp.concatenate([wa_ref[shift:, :], wb_ref[...]], axis=0)
        w_bf[...] = w.astype(BF16)

    o_ref[...] = lax.dot_general(x_ref[...], w_bf[...], NT_DIMS, preferred_element_type=F32).astype(o_ref.dtype)


def _gates(xn, w_t, layer):
    m, k = xn.shape
    tm = min(m, PROJ_TM)
    tn = PROJ_TN
    base = W_GM // tn * tn
    shift = W_GM - base
    assert shift % SUBLANES == 0 and tn % shift == 0
    return pl.pallas_call(
        functools.partial(_gates_kernel, shift=shift),
        grid=(2 * D_MODEL // tn, pl.cdiv(m, tm)),
        in_specs=[pl.BlockSpec((tm, k), lambda j, i: (i, 0)),
                  pl.BlockSpec((None, tn, k), lambda j, i: (layer, base // tn + j, 0)),
                  pl.BlockSpec((None, shift, k), lambda j, i: (layer, (base + (j + 1) * tn) // shift, 0))],
        out_specs=pl.BlockSpec((tm, tn), lambda j, i: (i, j)),
        out_shape=jax.ShapeDtypeStruct((m, 2 * D_MODEL), BF16),
        scratch_shapes=[pltpu.VMEM((tn, k), BF16)],
        compiler_params=_params("parallel", "arbitrary"),
        name="proj_gates",
    )(xn, w_t, w_t)


def _merge_kernel(oa_ref, ob_ref, wa_ref, wb_ref, ga_ref, gb_ref, ba_ref, bb_ref, o_ref, wa_bf, wb_bf):
    @pl.when(pl.program_id(1) == 0)
    def _cast_weight_tiles():
        wa_bf[...] = wa_ref[...].astype(BF16)
        wb_bf[...] = wb_ref[...].astype(BF16)

    ya = jnp.dot(oa_ref[...], wa_bf[...], preferred_element_type=F32)
    yb = jnp.dot(ob_ref[...], wb_bf[...], preferred_element_type=F32)
    ga = _sigmoid(ga_ref[...].astype(F32) + ba_ref[...])
    gb = _sigmoid(gb_ref[...].astype(F32) + bb_ref[...])
    o_ref[...] = (ga * ya + gb * yb).astype(o_ref.dtype)


def _merge(oa, ob, w_a, w_b, layer, gates, b_merge):
    m = oa.shape[0]
    tm = min(m, 1024)
    tn = 1024
    nj = D_MODEL // tn
    bm = b_merge.reshape(1, 2 * D_MODEL)
    return pl.pallas_call(
        _merge_kernel,
        grid=(nj, pl.cdiv(m, tm)),
        in_specs=[pl.BlockSpec((tm, MOBA_WIDTH), lambda j, i: (i, 0)),
                  pl.BlockSpec((tm, GLA_V_WIDTH), lambda j, i: (i, 0)),
                  pl.BlockSpec((None, MOBA_WIDTH, tn), lambda j, i: (layer, 0, j)),
                  pl.BlockSpec((None, GLA_V_WIDTH, tn), lambda j, i: (layer, 0, j)),
                  pl.BlockSpec((tm, tn), lambda j, i: (i, j)),
                  pl.BlockSpec((tm, tn), lambda j, i: (i, nj + j)),
                  pl.BlockSpec((1, tn), lambda j, i: (0, j)),
                  pl.BlockSpec((1, tn), lambda j, i: (0, nj + j))],
        out_specs=pl.BlockSpec((tm, tn), lambda j, i: (i, j)),
        out_shape=jax.ShapeDtypeStruct((m, D_MODEL), BF16),
        scratch_shapes=[pltpu.VMEM((MOBA_WIDTH, tn), BF16), pltpu.VMEM((GLA_V_WIDTH, tn), BF16)],
        compiler_params=_params("parallel", "arbitrary"),
        name="gated_merge",
    )(oa, ob, w_a, w_b, gates, gates, bm, bm)


def _out_kernel(m_ref, w_ref, x_ref, g_ref, *refs, emit_x):
    w_bf = refs[-1]

    @pl.when(pl.program_id(0) == 0)
    def _cast_weight():
        w_bf[...] = w_ref[...].astype(BF16)

    x_new = x_ref[...] + jnp.dot(m_ref[...], w_bf[...], preferred_element_type=F32)
    r = lax.rsqrt(jnp.mean(x_new * x_new, axis=-1, keepdims=True) + RMS_EPS)
    xn = x_new * r * g_ref[...]
    if emit_x:
        w1_ref, w2_ref, b_ref, x_out, xn_out, la_out = refs[:-1]
        x_out[...] = x_new
        xn_bf = xn.astype(xn_out.dtype)
        xn_out[...] = xn_bf
        la_out[...] = _log_forget_gate(xn_bf, w1_ref, w2_ref, b_ref)
    else:
        refs[0][...] = xn.astype(refs[0].dtype)


def _out_proj(merged, w_out, layer, x, g_next, gate_next):
    m = x.shape[0]
    tm = min(m, 512)
    row = pl.BlockSpec((tm, D_MODEL), lambda i: (i, 0))
    in_specs = [row,
                pl.BlockSpec((None, D_MODEL, D_MODEL), lambda i: (layer, 0, 0), pipeline_mode=pl.Buffered(1)),
                row,
                pl.BlockSpec((1, D_MODEL), lambda i: (0, 0))]
    operands = [merged, w_out, x, g_next.reshape(1, D_MODEL)]
    if gate_next is not None:
        gate_specs, gate_operands = _gate_operands(*gate_next)
        in_specs += gate_specs
        operands += gate_operands
        out_specs = [row, row, pl.BlockSpec((tm, GLA_K_WIDTH), lambda i: (i, 0))]
        out_shape = [jax.ShapeDtypeStruct((m, D_MODEL), F32),
                     jax.ShapeDtypeStruct((m, D_MODEL), BF16),
                     jax.ShapeDtypeStruct((m, GLA_K_WIDTH), F32)]
    else:
        out_specs = [row]
        out_shape = [jax.ShapeDtypeStruct((m, D_MODEL), F32)]
    return pl.pallas_call(
        functools.partial(_out_kernel, emit_x=gate_next is not None),
        grid=(pl.cdiv(m, tm),),
        in_specs=in_specs,
        out_specs=out_specs,
        out_shape=out_shape,
        scratch_shapes=[pltpu.VMEM((D_MODEL, D_MODEL), BF16)],
        compiler_params=_params("arbitrary"),
        name="out_proj",
    )(*operands)


def _top_k_mask(s, idx, n_cand, axis):
    sel = jnp.zeros(s.shape, F32)
    for _ in range(MOBA_TOPK):
        mx = jnp.max(s, axis=axis, keepdims=True)
        cand = jnp.where((s == mx) & (mx > -jnp.inf), idx, float(n_cand))
        first = jnp.min(cand, axis=axis, keepdims=True)
        pick = idx == first
        sel = jnp.where(pick, 1.0, sel)
        s = jnp.where(pick, -jnp.inf, s)
    return sel


def _moba_prompt_kernel(pt_ref, q_ref, z_ref, k_ref, vt_ref, mean_ref, ck_ref, o_ref, pm_ref,
                        sel_scr, acc_scr, lg_scr, page_buf, page_sem, *, nblk, heads, group, layer, pages_per_step):
    j = pl.program_id(2)

    step = (pl.program_id(0) * pl.num_programs(1) + pl.program_id(1)) * nblk + j
    n_steps = pl.num_programs(0) * pl.num_programs(1) * nblk
    pages_per_seq = pt_ref.shape[1]
    pages_per_block = MOBA_BLOCK // page_buf.shape[2]
    n_bufs = page_buf.shape[0]

    def page_copies(s):
        first = s * pages_per_step
        seq = first // pages_per_seq
        pg0 = first % pages_per_seq
        slot = s % n_bufs
        return [pltpu.make_async_copy(ck_ref.at[layer, pt_ref[seq, pg0 + p]], page_buf.at[slot, p],
                                      page_sem.at[slot]) for p in range(pages_per_step)]

    @pl.when(step == 0)
    def _first_pages():
        for ahead in range(n_bufs - 1):
            for cp in page_copies(step + ahead):
                cp.start()

    @pl.when(step + (n_bufs - 1) < n_steps)
    def _next_pages():
        for cp in page_copies(step + (n_bufs - 1)):
            cp.start()

    for cp in page_copies(step):
        cp.wait()
    first_page = step * pages_per_step
    for i in range(pages_per_step // pages_per_block):
        tot = None
        for pg in range(pages_per_block):
            part = jnp.sum(page_buf[step % n_bufs, i * pages_per_block + pg], axis=0)
            tot = part if tot is None else tot + part
        pm_ref[first_page // pages_per_seq, (first_page % pages_per_seq) // pages_per_block + i] = (
            tot * (1.0 / MOBA_BLOCK))

    ones_rows = jnp.ones((MOBA_DENOM_ROWS, MOBA_BLOCK), BF16)
    lanes = [slice(h * HEAD_DIM, (h + 1) * HEAD_DIM) for h in range(heads)]
    qs = [q_ref[:, ln] for ln in lanes]

    def logits(h, n):
        r = pl.multiple_of(n * MOBA_BLOCK, MOBA_BLOCK)
        return lax.dot_general(k_ref[pl.ds(r, MOBA_BLOCK), lanes[h]], qs[h], NT_DIMS, preferred_element_type=F32)

    def weighted_values(h, n, p):
        lhs = jnp.concatenate([vt_ref[n, lanes[h], :], ones_rows], axis=0)
        return jnp.dot(lhs, p, preferred_element_type=F32)

    def trip_blocks(i):
        return [jnp.minimum(group * i + g, nblk - 1) for g in range(group)]

    scores = [lax.dot_general(mean_ref[:, lanes[h]].astype(BF16), qs[h], NT_DIMS, preferred_element_type=F32)
              for h in range(heads)]
    blk = lax.broadcasted_iota(jnp.int32, scores[0].shape, 0)
    for h in range(heads):
        sel_scr[h] = _top_k_mask(jnp.where(blk < j, scores[h], -jnp.inf), blk.astype(F32), nblk, 0)

    own = [logits(h, j) for h in range(heads)]
    for h in range(heads):
        for g, n in enumerate(trip_blocks(0)):
            lg_scr[h * group + g] = logits(h, n)
    key = lax.broadcasted_iota(jnp.int32, own[0].shape, 0)
    qry = lax.broadcasted_iota(jnp.int32, own[0].shape, 1)
    m0, p0 = [], []
    for h in range(heads):
        lg = jnp.where(key <= qry, own[h], NEG_INF)
        m = jnp.max(lg, axis=0, keepdims=True)
        m0.append(m)
        p0.append(jnp.exp2(lg - m).astype(BF16))
    for h in range(heads):
        acc_scr[h] = weighted_values(h, j, p0[h])

    def past_blocks(i, carry):
        ms = carry
        ns = trip_blocks(i)
        ns_next = trip_blocks(i + 1)
        new_m = []
        for h in range(heads):
            lgs = [lg_scr[h * group + g] for g in range(group)]
            for g, n in enumerate(ns_next):
                lg_scr[h * group + g] = logits(h, n)
            chosen = [(sel_scr[h, pl.ds(n, 1), :] > 0.0) & (group * i + g < j) for g, n in enumerate(ns)]
            m_new = ms[h]
            for lg, ch in zip(lgs, chosen):
                m_new = jnp.maximum(m_new, jnp.where(ch, jnp.max(lg, axis=0, keepdims=True), NEG_INF))
            alpha = jnp.exp2(ms[h] - m_new)
            pv = None
            for lg, ch, n in zip(lgs, chosen, ns):
                p = jnp.exp2(lg - jnp.where(ch, m_new, -NEG_INF))
                part = weighted_values(h, n, p.astype(BF16))
                pv = part if pv is None else pv + part
            acc_scr[h] = alpha * acc_scr[h] + pv
            new_m.append(m_new)
        return tuple(new_m)

    lax.fori_loop(0, (j + group - 1) // group, past_blocks, tuple(m0))

    for h in range(heads):
        z = z_ref[:, lanes[h]].astype(F32)
        out_t = acc_scr[h, :HEAD_DIM, :] / acc_scr[h, HEAD_DIM:HEAD_DIM + 1, :]
        o_ref[:, lanes[h]] = (jnp.transpose(out_t) * _silu(z)).astype(o_ref.dtype)


MOBA_HEADS_PER_STEP = 4
PAGE_LOOKAHEAD = 2
MOBA_DENOM_ROWS = 16
MOBA_BLOCKS_PER_TRIP = 2


def _moba_prompt(main, k16, vt16, means, cache_k, page_table, layer, batch, seq):
    nblk = seq // MOBA_BLOCK
    hp = MOBA_HEADS_PER_STEP
    width = hp * HEAD_DIM
    dec_b, n_pages = page_table.shape
    page = cache_k.shape[2]
    pages_per_block = MOBA_BLOCK // page
    n_steps = batch * (MOBA_HEADS // hp) * nblk
    pages_per_step = dec_b * n_pages // n_steps
    assert pages_per_step * n_steps == dec_b * n_pages, "pages must split evenly over the grid steps"
    assert n_pages % pages_per_step == 0 and pages_per_step % pages_per_block == 0
    n_past_blocks = n_pages // pages_per_block
    qmap = lambda b, h, j, pt: (b * nblk + j, COL_QA // width + h)
    zmap = lambda b, h, j, pt: (b * nblk + j, COL_ZA // width + h)
    return pl.pallas_call(
        functools.partial(_moba_prompt_kernel, nblk=nblk, heads=hp, group=MOBA_BLOCKS_PER_TRIP,
                          layer=layer, pages_per_step=pages_per_step),
        grid_spec=pltpu.PrefetchScalarGridSpec(
            num_scalar_prefetch=1,
            grid=(batch, MOBA_HEADS // hp, nblk),
            in_specs=[pl.BlockSpec((MOBA_BLOCK, width), qmap),
                      pl.BlockSpec((MOBA_BLOCK, width), zmap),
                      pl.BlockSpec((seq, width), lambda b, h, j, pt: (b, h)),
                      pl.BlockSpec((nblk, width, MOBA_BLOCK), lambda b, h, j, pt: (b, h, 0)),
                      pl.BlockSpec((None, nblk, width), lambda b, h, j, pt: (b, 0, h)),
                      pl.BlockSpec(memory_space=pl.ANY)],
            out_specs=[pl.BlockSpec((MOBA_BLOCK, width), lambda b, h, j, pt: (b * nblk + j, h)),
                       pl.BlockSpec((dec_b, n_past_blocks, MOBA_HEADS, HEAD_DIM), lambda b, h, j, pt: (0, 0, 0, 0))],
            scratch_shapes=[pltpu.VMEM((hp, nblk, MOBA_BLOCK), F32),
                            pltpu.VMEM((hp, HEAD_DIM + MOBA_DENOM_ROWS, MOBA_BLOCK), F32),
                            pltpu.VMEM((hp * MOBA_BLOCKS_PER_TRIP, MOBA_BLOCK, MOBA_BLOCK), F32),
                            pltpu.VMEM((PAGE_LOOKAHEAD + 1, pages_per_step, page, MOBA_HEADS, HEAD_DIM), F32),
                            pltpu.SemaphoreType.DMA((PAGE_LOOKAHEAD + 1,))],
        ),
        out_shape=[jax.ShapeDtypeStruct((batch * seq, MOBA_WIDTH), BF16),
                   jax.ShapeDtypeStruct((dec_b, n_past_blocks, MOBA_HEADS, HEAD_DIM), F32)],
        compiler_params=_params("arbitrary", "arbitrary", "arbitrary"),
        name="moba_prompt",
    )(page_table, main, main, k16, vt16, means, cache_k)


def _gla_kernel(q_ref, k_ref, v_ref, g_ref, z_ref, s0_ref, gn_ref, o_ref, sout_ref, s_scr,
                *, chunk, sub, n_valid, heads):
    t = pl.program_id(2)

    @pl.when(t == 0)
    def _load_state():
        s_scr[...] = s0_ref[...]

    hs = range(heads)
    dk = [slice(h * GLA_DK, (h + 1) * GLA_DK) for h in hs]
    dv = [slice(h * GLA_DV, (h + 1) * GLA_DV) for h in hs]
    ri = lax.broadcasted_iota(jnp.int32, (chunk, chunk), 0)
    ci = lax.broadcasted_iota(jnp.int32, (chunk, chunk), 1)
    causal = ci <= ri
    tri = jnp.where(causal, 1.0, 0.0).astype(BF16)
    live = lax.broadcasted_iota(jnp.int32, (chunk, GLA_DK), 0) < n_valid
    nsub = chunk // sub

    ks, b3s = [], []
    for h in hs:
        k = k_ref[:, dk[h]].astype(F32)
        g = g_ref[:, dk[h]]
        if n_valid < chunk:
            g = jnp.where(live, g, 0.0)
            k = jnp.where(live, k, 0.0)
        g_hi = g.astype(BF16)
        g_r1 = g - g_hi.astype(F32)
        g_mid = g_r1.astype(BF16)
        g_lo = (g_r1 - g_mid.astype(F32)).astype(BF16)
        ks.append(k)
        b3s.append(jnp.dot(tri, jnp.concatenate([g_hi, g_mid, g_lo], axis=1), preferred_element_type=F32))

    bs, qs, scores = [], [], []
    for h in hs:
        b3 = b3s[h]
        b = (b3[:, :GLA_DK] + b3[:, GLA_DK:2 * GLA_DK]) + b3[:, 2 * GLA_DK:]
        q = q_ref[:, dk[h]].astype(F32) * (GLA_DK ** -0.5)
        ends = [b[(i + 1) * sub - 1:(i + 1) * sub] for i in range(nsub)]
        own_end = jnp.concatenate([jnp.broadcast_to(e, (sub, GLA_DK)) for e in ends], axis=0)
        k_rel = (ks[h] * jnp.exp(own_end - b)).astype(BF16)
        q_parts, k_rows = [], []
        for i in range(nsub):
            lo = i * sub
            q_rel = (q[lo:] * jnp.exp(b[lo:] - ends[i])).astype(BF16)
            q_parts.append(q_rel if lo == 0 else
                           jnp.concatenate([jnp.zeros((lo, GLA_DK), BF16), q_rel], axis=0))
            k_rows.append(jnp.concatenate(
                [k_rel[lo:lo + sub] if c == i else jnp.zeros((sub, GLA_DK), BF16) for c in range(nsub)], axis=1)
                if nsub > 1 else k_rel)
        q_cat = jnp.concatenate(q_parts, axis=1) if nsub > 1 else q_parts[0]
        k_cat = jnp.concatenate(k_rows, axis=0) if nsub > 1 else k_rows[0]
        scores.append(lax.dot_general(q_cat, k_cat, NT_DIMS, preferred_element_type=F32))
        bs.append(b)
        qs.append(q)

    outs, new_states = [], []
    for h in hs:
        b = bs[h]
        b_last = b[chunk - 1:chunk]
        v = v_ref[:, dv[h]]
        state = s_scr[h]
        a = jnp.where(causal, scores[h], 0.0)
        outs.append(jnp.dot(a.astype(BF16), v, preferred_element_type=F32)
                    + jnp.dot((qs[h] * jnp.exp(b)).astype(BF16), state.astype(BF16), preferred_element_type=F32))
        k_dec = (ks[h] * jnp.exp(b_last - b)).astype(BF16)
        update = lax.dot_general(k_dec, v, TN_DIMS, preferred_element_type=F32)
        decay_t = jnp.transpose(jnp.broadcast_to(jnp.exp(b_last), (GLA_DK, GLA_DK)))
        new_states.append(jnp.concatenate([decay_t] * (GLA_DV // GLA_DK), axis=1) * state + update)

    for h in hs:
        s_scr[h] = new_states[h]
        o = outs[h]
        r = lax.rsqrt(jnp.mean(o * o, axis=-1, keepdims=True) + RMS_EPS)
        z = z_ref[:, dv[h]].astype(F32)
        o_ref[:, dv[h]] = (o * r * gn_ref[h] * _silu(z)).astype(o_ref.dtype)

    @pl.when(t == pl.num_programs(2) - 1)
    def _store_state():
        for h in hs:
            sout_ref[h] = new_states[h]


GLA_HEADS_PER_STEP = 4


def _gla(main, log_a, s0, gn, batch, rows_per_seq, chunk, sub, n_valid):
    nt = rows_per_seq // chunk
    m = main.shape[0]
    hp = GLA_HEADS_PER_STEP
    wk, wv = hp * GLA_DK, hp * GLA_DV
    rmap = lambda off: (lambda b, h, t: (b * nt + t, off + h))
    smap = lambda b, h, t: (b, h, 0, 0)
    return pl.pallas_call(
        functools.partial(_gla_kernel, chunk=chunk, sub=sub, n_valid=n_valid, heads=hp),
        grid=(batch, GLA_HEADS // hp, nt),
        in_specs=[pl.BlockSpec((chunk, wk), rmap(COL_QB // wk)),
                  pl.BlockSpec((chunk, wk), rmap(COL_KB // wk)),
                  pl.BlockSpec((chunk, wv), rmap(COL_VB // wv)),
                  pl.BlockSpec((chunk, wk), rmap(0)),
                  pl.BlockSpec((chunk, wv), rmap(COL_ZB // wv)),
                  pl.BlockSpec((None, hp, GLA_DK, GLA_DV), smap),
                  pl.BlockSpec((hp, 1, GLA_DV), lambda b, h, t: (h, 0, 0))],
        out_specs=[pl.BlockSpec((chunk, wv), rmap(0)),
                   pl.BlockSpec((None, hp, GLA_DK, GLA_DV), smap)],
        out_shape=[jax.ShapeDtypeStruct((m, GLA_V_WIDTH), BF16),
                   jax.ShapeDtypeStruct((batch, GLA_HEADS, GLA_DK, GLA_DV), F32)],
        scratch_shapes=[pltpu.VMEM((hp, GLA_DK, GLA_DV), F32)],
        compiler_params=_params("parallel", "parallel", "arbitrary"),
        name="gla",
    )(main, main, main, log_a, main, s0, gn.reshape(GLA_HEADS, 1, GLA_DV))


def _sample_select_kernel(q_ref, mean_ref, o_ref, *, n_blocks):
    q = q_ref[...]
    means = mean_ref[...].astype(BF16)
    rows = q.shape[0]
    ncol = n_blocks * MOBA_HEADS
    col = lax.broadcasted_iota(jnp.int32, (rows, ncol), 1)
    col_head = col % MOBA_HEADS
    col_f = col.astype(F32)
    s_all = jnp.zeros((rows, ncol), F32)
    for h in range(MOBA_HEADS):
        s_h = lax.dot_general(q[:, h * HEAD_DIM:(h + 1) * HEAD_DIM], means, NT_DIMS, preferred_element_type=F32)
        s_all = jnp.where(col_head == h, s_h, s_all)
    lane = lax.broadcasted_iota(jnp.int32, o_ref.shape, 1)
    out = jnp.zeros(o_ref.shape, jnp.int32)
    for h in range(MOBA_HEADS):
        s = jnp.where(col_head == h, s_all, -jnp.inf)
        for t in range(MOBA_TOPK):
            mx = jnp.max(s, axis=1, keepdims=True)
            cand = jnp.where((s == mx) & (mx > -jnp.inf), col_f, float(ncol))
            first = jnp.min(cand, axis=1, keepdims=True)
            s = jnp.where(col_f == first, -jnp.inf, s)
            blk = first.astype(jnp.int32) // MOBA_HEADS
            out = jnp.where(lane == h * MOBA_TOPK + t, blk, out)
    o_ref[...] = out


def _sample_select(main, means, dec_b):
    n_blocks = means.shape[1]
    means2 = means.reshape(dec_b, n_blocks * MOBA_HEADS, HEAD_DIM)
    return pl.pallas_call(
        functools.partial(_sample_select_kernel, n_blocks=n_blocks),
        grid=(dec_b,),
        in_specs=[pl.BlockSpec((SAMPLE_ROWS, MOBA_WIDTH), lambda b: (b, COL_QA // MOBA_WIDTH)),
                  pl.BlockSpec((None, n_blocks * MOBA_HEADS, HEAD_DIM), lambda b: (b, 0, 0))],
        out_specs=pl.BlockSpec((None, SAMPLE_ROWS, LANES), lambda b: (b, 0, 0)),
        out_shape=jax.ShapeDtypeStruct((dec_b, SAMPLE_ROWS, LANES), jnp.int32),
        compiler_params=_params("parallel"),
        name="moba_sample_select",
    )(main, means2)


def _moba_sample_kernel(pt_ref, sel_ref, q_ref, z_ref, ko_ref, vo_ref, ck_ref, cv_ref, o_ref,
                        kbuf, vbuf, sem, *, layer, n_new, page):
    n_heads = pl.num_programs(1)
    step = pl.program_id(0) * n_heads + pl.program_id(1)
    n_steps = pl.num_programs(0) * n_heads
    pages_per_block = MOBA_BLOCK // page
    n_slots = n_new * MOBA_TOPK * pages_per_block

    def page_copies(s):
        b, h = s // n_heads, s % n_heads
        buf = s % 2
        copies = []
        for slot in range(n_slots):
            qi, rest = divmod(slot, MOBA_TOPK * pages_per_block)
            rank, pg = divmod(rest, pages_per_block)
            blk = sel_ref[b, qi * (MOBA_HEADS * MOBA_TOPK) + h * MOBA_TOPK + rank]
            pid = pt_ref[b, blk * pages_per_block + pg]
            dst = pl.ds(slot * page, page)
            copies.append(pltpu.make_async_copy(ck_ref.at[layer, pid, :, h, :], kbuf.at[buf, dst, :],
                                                sem.at[buf, 0]))
            copies.append(pltpu.make_async_copy(cv_ref.at[layer, pid, :, h, :], vbuf.at[buf, dst, :],
                                                sem.at[buf, 1]))
        return copies

    @pl.when(step == 0)
    def _first_gather():
        for cp in page_copies(step):
            cp.start()

    @pl.when(step + 1 < n_steps)
    def _next_gather():
        for cp in page_copies(step + 1):
            cp.start()

    for cp in page_copies(step):
        cp.wait()

    scale = HEAD_DIM ** -0.5
    q = q_ref[...]
    rows = q.shape[0]
    n_keys = n_slots * page
    k_sel = kbuf[step % 2]
    v_sel = vbuf[step % 2]
    lg = lax.dot_general(q, k_sel.astype(BF16), NT_DIMS, preferred_element_type=F32) * scale
    row = lax.broadcasted_iota(jnp.int32, (rows, n_keys), 0)
    key_owner = lax.broadcasted_iota(jnp.int32, (rows, n_keys), 1) // (MOBA_TOPK * MOBA_BLOCK)
    lg = jnp.where(key_owner == row, lg, NEG_INF)
    lg_own = lax.dot_general(q, ko_ref[...], NT_DIMS, preferred_element_type=F32) * scale
    r2 = lax.broadcasted_iota(jnp.int32, lg_own.shape, 0)
    c2 = lax.broadcasted_iota(jnp.int32, lg_own.shape, 1)
    lg_own = jnp.where((c2 <= r2) & (c2 < n_new), lg_own, NEG_INF)
    m = jnp.maximum(jnp.max(lg, axis=1, keepdims=True), jnp.max(lg_own, axis=1, keepdims=True))
    p = jnp.exp(lg - m)
    p_own = jnp.exp(lg_own - m)
    denom = jnp.sum(p, axis=1, keepdims=True) + jnp.sum(p_own, axis=1, keepdims=True)
    acc = (jnp.dot(p.astype(BF16), v_sel.astype(BF16), preferred_element_type=F32)
           + jnp.dot(p_own.astype(BF16), vo_ref[...], preferred_element_type=F32))
    z = z_ref[...].astype(F32)
    o_ref[...] = (acc / denom * _silu(z)).astype(o_ref.dtype)


def _moba_sample(main, k16, v16, cache_k, cache_v, page_table, sel, layer, n_new):
    dec_b = page_table.shape[0]
    page = cache_k.shape[2]
    n_keys = n_new * MOBA_TOPK * MOBA_BLOCK
    tile = lambda off: pl.BlockSpec((SAMPLE_ROWS, HEAD_DIM), lambda b, h, pt, sl: (b, off + h))
    return pl.pallas_call(
        functools.partial(_moba_sample_kernel, layer=layer, n_new=n_new, page=page),
        grid_spec=pltpu.PrefetchScalarGridSpec(
            num_scalar_prefetch=2,
            grid=(dec_b, MOBA_HEADS),
            in_specs=[tile(COL_QA // HEAD_DIM), tile(COL_ZA // HEAD_DIM), tile(0), tile(0),
                      pl.BlockSpec(memory_space=pl.ANY), pl.BlockSpec(memory_space=pl.ANY)],
            out_specs=tile(0),
            scratch_shapes=[pltpu.VMEM((2, n_keys, HEAD_DIM), F32),
                            pltpu.VMEM((2, n_keys, HEAD_DIM), F32),
                            pltpu.SemaphoreType.DMA((2, 2))],
        ),
        out_shape=jax.ShapeDtypeStruct((dec_b * SAMPLE_ROWS, MOBA_WIDTH), BF16),
        compiler_params=_params("arbitrary", "arbitrary"),
        name="moba_sample",
    )(page_table, sel, main, main, k16, v16, cache_k, cache_v)


def _main_col_block(j):
    return jnp.where(j == 0, W_QA // PROJ_TN, j + (W_ZA // PROJ_TN - 1))


def _token_proj(xn, w_t, layer, prompt, stack_depth=1, k_stacked=None, v_stacked=None):
    k_outs = ((F32, False), (BF16, False)) + (((F32, "block_mean"),) if prompt else ())
    k32, k16, *k_means = _proj(xn, w_t, layer, lambda j: W_KA // PROJ_TN, 1, k_outs, "proj_k",
                               stack_depth, k_stacked)
    v32, v16 = _proj(xn, w_t, layer, lambda j: W_VA // PROJ_TN, 1, ((F32, False), (BF16, prompt)), "proj_v",
                     stack_depth, v_stacked)
    q_scale = (HEAD_DIM ** -0.5) * LOG2E if prompt else None
    (main,) = _proj(xn, w_t, layer, _main_col_block, MAIN_WIDTH // PROJ_TN, ((BF16, False),), "proj_main",
                    first_tile_scale=q_scale)
    gates = _gates(xn, w_t, layer)
    return k32, k16, v32, v16, main, gates, k_means


def kernel(x_prompt, x_sample, cache_k, cache_v, state_gla, page_table, norm_g, w_in, w_gate2, b_gate,
           gla_norm_g, w_branch_a, w_branch_b, b_merge, w_out, final_norm_g):
    depth = w_in.shape[0]
    bp, seq, _ = x_prompt.shape
    dec_b, n_new, _ = x_sample.shape
    assert n_new <= SAMPLE_ROWS and seq % MOBA_BLOCK == 0
    assert cache_k.shape[2] * page_table.shape[1] % MOBA_BLOCK == 0
    assert (W_ZA // PROJ_TN, W_ZB // PROJ_TN) == (3, 6) and W_QB % PROJ_TN == 0

    xp = x_prompt.reshape(bp * seq, D_MODEL)
    xs = jnp.pad(x_sample, ((0, 0), (0, SAMPLE_ROWS - n_new), (0, 0))).reshape(dec_b * SAMPLE_ROWS, D_MODEL)
    zero_state = jnp.zeros((bp, GLA_HEADS, GLA_DK, GLA_DV), F32)
    w_t = jnp.swapaxes(w_in, 1, 2)
    w_g2 = jnp.pad(w_gate2.astype(BF16), ((0, 0), (0, LANES - GLA_GATE_RANK), (0, 0)))
    gate = lambda l: (w_t, l, w_g2[l], b_gate[l])
    xnp, log_a_p = _rmsnorm(xp, norm_g[0], gate(0))
    xns, log_a_s = _rmsnorm(xs, norm_g[0], gate(0))

    sp_l, ks_l, vs_l, ss_l = [], [], [], []
    kp_all = vp_all = None
    for l in range(depth):
        last = l == depth - 1
        g_next = final_norm_g if last else norm_g[l + 1]
        gate_next = None if last else gate(l + 1)

        kp_all, k16, vp_all, vt16, main, gates, (k_means,) = _token_proj(
            xnp, w_t, l, True, depth, kp_all, vp_all)
        means = k_means.reshape(bp, seq // MOBA_BLOCK, MOBA_WIDTH)
        oa, pmeans = _moba_prompt(main, k16, vt16, means, cache_k, page_table, l, bp, seq)
        ob, sp = _gla(main, log_a_p, zero_state, gla_norm_g[l], bp, seq, MOBA_BLOCK, GLA_SUB, MOBA_BLOCK)
        merged = _merge(oa, ob, w_branch_a, w_branch_b, l, gates, b_merge[l])
        outs = _out_proj(merged, w_out, l, xp, g_next, gate_next)
        if last:
            (yp,) = outs
        else:
            xp, xnp, log_a_p = outs
        sp_l.append(sp)

        k32, k16, v32, v16, main, gates, _ = _token_proj(xns, w_t, l, False)
        sel = _sample_select(main, pmeans, dec_b)
        sel = sel[:, :n_new, :MOBA_HEADS * MOBA_TOPK].reshape(dec_b, n_new * MOBA_HEADS * MOBA_TOPK)
        oa = _moba_sample(main, k16, v16, cache_k, cache_v, page_table, sel, l, n_new)
        ob, ss = _gla(main, log_a_s, state_gla[l], gla_norm_g[l], dec_b, SAMPLE_ROWS, SAMPLE_ROWS,
                      SAMPLE_ROWS, n_new)
        merged = _merge(oa, ob, w_branch_a, w_branch_b, l, gates, b_merge[l])
        outs = _out_proj(merged, w_out, l, xs, g_next, gate_next)
        if last:
            (ys,) = outs
        else:
            xs, xns, log_a_s = outs
        rows = lambda a: a.reshape(dec_b, SAMPLE_ROWS, MOBA_HEADS, HEAD_DIM)[:, :n_new]
        ks_l.append(rows(k32))
        vs_l.append(rows(v32))
        ss_l.append(ss)

    y_prompt = yp.reshape(bp, seq, D_MODEL)
    y_sample = ys.reshape(dec_b, SAMPLE_ROWS, D_MODEL)[:, :n_new]
    k_prompt = kp_all.reshape(depth, bp, seq, MOBA_HEADS, HEAD_DIM)
    v_prompt = vp_all.reshape(depth, bp, seq, MOBA_HEADS, HEAD_DIM)
    return (y_prompt, y_sample, k_prompt, v_prompt, jnp.stack(sp_l),
            jnp.stack(ks_l), jnp.stack(vs_l), jnp.stack(ss_l))
```

```python
import functools
import math

import jax
import jax.numpy as jnp
from jax import lax
from jax.experimental import pallas as pl
from jax.experimental.pallas import tpu as pltpu

D_MODEL = 2048
MOBA_HEADS = 8
HEAD_DIM = 128
MOBA_BLOCK = 256
MOBA_TOPK = 3
GLA_HEADS = 4
GLA_DK = 128
GLA_DV = 256
GLA_GATE_RANK = 16
GLA_TAU = 16.0
GLA_SUB = 32
RMS_EPS = 1e-6
NEG_INF = -1e30
MOBA_WIDTH = MOBA_HEADS * HEAD_DIM
GLA_K_WIDTH = GLA_HEADS * GLA_DK
GLA_V_WIDTH = GLA_HEADS * GLA_DV

W_QA = 0
W_KA = W_QA + MOBA_WIDTH
W_VA = W_KA + MOBA_WIDTH
W_ZA = W_VA + MOBA_WIDTH
W_QB = W_ZA + MOBA_WIDTH
W_KB = W_QB + GLA_K_WIDTH
W_VB = W_KB + GLA_K_WIDTH
W_ZB = W_VB + GLA_V_WIDTH
W_ALR = W_ZB + GLA_V_WIDTH
W_GM = W_ALR + GLA_GATE_RANK

COL_QA = 0
COL_ZA = COL_QA + MOBA_WIDTH
COL_QB = COL_ZA + MOBA_WIDTH
COL_KB = COL_QB + GLA_K_WIDTH
COL_VB = COL_KB + GLA_K_WIDTH
COL_ZB = COL_VB + GLA_V_WIDTH
MAIN_WIDTH = COL_ZB + GLA_V_WIDTH

LANES = 128
SUBLANES = 8
SAMPLE_ROWS = SUBLANES
PROJ_TN = 1024
PROJ_TM = 1024
PROJ_TM_BF16 = 2048
VMEM_LIMIT = 56 * 1024 * 1024

F32 = jnp.float32
BF16 = jnp.bfloat16
NT_DIMS = (((1,), (1,)), ((), ()))
TN_DIMS = (((0,), (0,)), ((), ()))
LOG2E = math.log2(math.e)


def _params(*sem):
    return pltpu.CompilerParams(dimension_semantics=sem, vmem_limit_bytes=VMEM_LIMIT)


def _silu(z):
    return z * (1.0 / (1.0 + jnp.exp(-z)))


def _sigmoid(z):
    return 1.0 / (1.0 + jnp.exp(-z))


def _log_forget_gate(xn, w1_ref, w2_ref, b_ref):
    a = lax.dot_general(xn, w1_ref[...].astype(BF16), NT_DIMS, preferred_element_type=F32).astype(BF16)
    y = jnp.dot(a, w2_ref[...], preferred_element_type=F32) + b_ref[...]
    log_sig = jnp.minimum(y, 0.0) - jnp.log(1.0 + jnp.exp(-jnp.abs(y)))
    return log_sig * (1.0 / GLA_TAU)


def _gate_operands(w_t, layer, w2, b):
    assert W_ALR % LANES == 0
    specs = [pl.BlockSpec((None, LANES, w_t.shape[2]), lambda i: (layer, W_ALR // LANES, 0)),
             pl.BlockSpec((LANES, GLA_K_WIDTH), lambda i: (0, 0)),
             pl.BlockSpec((1, GLA_K_WIDTH), lambda i: (0, 0))]
    return specs, [w_t, w2, b.reshape(1, GLA_K_WIDTH)]


def _rmsnorm_kernel(x_ref, g_ref, w1_ref, w2_ref, b_ref, o_ref, la_ref):
    x = x_ref[...]
    r = lax.rsqrt(jnp.mean(x * x, axis=-1, keepdims=True) + RMS_EPS)
    xn = (x * r * g_ref[...]).astype(o_ref.dtype)
    o_ref[...] = xn
    la_ref[...] = _log_forget_gate(xn, w1_ref, w2_ref, b_ref)


def _rmsnorm(x, g, gate):
    m, d = x.shape
    tm = min(m, 512)
    gate_specs, gate_operands = _gate_operands(*gate)
    return pl.pallas_call(
        _rmsnorm_kernel,
        grid=(pl.cdiv(m, tm),),
        in_specs=[pl.BlockSpec((tm, d), lambda i: (i, 0)),
                  pl.BlockSpec((1, d), lambda i: (0, 0))] + gate_specs,
        out_specs=[pl.BlockSpec((tm, d), lambda i: (i, 0)),
                   pl.BlockSpec((tm, GLA_K_WIDTH), lambda i: (i, 0))],
        out_shape=[jax.ShapeDtypeStruct((m, d), BF16),
                   jax.ShapeDtypeStruct((m, GLA_K_WIDTH), F32)],
        compiler_params=_params("parallel"),
        name="rmsnorm",
    )(x, g.reshape(1, d), *gate_operands)


def _proj_kernel(x_ref, w_ref, *refs, transposed, fill_layer, first_tile_scale):
    *o_refs, w_bf = refs[-(len(transposed) + 1):]

    @pl.when(pl.program_id(1) == 0)
    def _cast_weight_tile():
        w_bf[...] = w_ref[...].astype(BF16)

    acc = lax.dot_general(x_ref[...], w_bf[...], NT_DIMS, preferred_element_type=F32)
    if first_tile_scale is not None:
        acc = acc * jnp.where(pl.program_id(0) == 0, first_tile_scale, 1.0)
    for pos, (o_ref, tr) in enumerate(zip(o_refs, transposed)):
        if tr == "block_mean":
            for r in range(o_ref.shape[0]):
                rows = acc[r * MOBA_BLOCK:(r + 1) * MOBA_BLOCK, :]
                o_ref[r] = jnp.sum(rows, axis=0, keepdims=True) * (1.0 / MOBA_BLOCK)
        elif tr:
            for r in range(o_ref.shape[0]):
                rows = acc[r * MOBA_BLOCK:(r + 1) * MOBA_BLOCK, :]
                o_ref[r] = jnp.transpose(rows).astype(o_ref.dtype)
        elif pos == 0 and fill_layer is not None:
            for d in range(o_ref.shape[0]):
                o_ref[d] = acc.astype(o_ref.dtype) if d == fill_layer else jnp.zeros(acc.shape, o_ref.dtype)
        else:
            o_ref[...] = acc.astype(o_ref.dtype)


def _proj(xn, w_t, layer, col_block, n_blocks, outs, name, stack_depth=1, stacked=None, first_tile_scale=None):
    m, k = xn.shape
    tm = min(m, PROJ_TM_BF16 if all(dt == BF16 and not tr for dt, tr in outs) else PROJ_TM)
    tn = PROJ_TN
    n = n_blocks * tn
    ni = pl.cdiv(m, tm)
    fill_layer = layer if stack_depth > 1 and stacked is None else None
    out_specs, out_shape = [], []
    for pos, (dt, tr) in enumerate(outs):
        if tr == "block_mean":
            out_specs.append(pl.BlockSpec((tm // MOBA_BLOCK, 1, tn), lambda j, i: (i, 0, j)))
            out_shape.append(jax.ShapeDtypeStruct((m // MOBA_BLOCK, 1, n), dt))
        elif tr:
            out_specs.append(pl.BlockSpec((tm // MOBA_BLOCK, tn, MOBA_BLOCK), lambda j, i: (i, j, 0)))
            out_shape.append(jax.ShapeDtypeStruct((m // MOBA_BLOCK, n, MOBA_BLOCK), dt))
        elif pos == 0 and stack_depth > 1:
            if stacked is None:
                out_specs.append(pl.BlockSpec((stack_depth, tm, tn), lambda j, i: (0, i, j)))
            else:
                out_specs.append(pl.BlockSpec((None, tm, tn), lambda j, i: (layer, i, j)))
            out_shape.append(jax.ShapeDtypeStruct((stack_depth, m, n), dt))
        else:
            out_specs.append(pl.BlockSpec((tm, tn), lambda j, i: (i, j)))
            out_shape.append(jax.ShapeDtypeStruct((m, n), dt))
    in_specs = [pl.BlockSpec((tm, k), lambda j, i: (i, 0)),
                pl.BlockSpec((None, tn, k), lambda j, i: (layer, col_block(j), 0))]
    operands = [xn, w_t]
    aliases = {}
    if stacked is not None:
        in_specs.append(pl.BlockSpec(memory_space=pl.ANY))
        operands.append(stacked)
        aliases = {2: 0}
    return pl.pallas_call(
        functools.partial(_proj_kernel, transposed=tuple(tr for _, tr in outs), fill_layer=fill_layer,
                          first_tile_scale=first_tile_scale),
        grid=(n_blocks, ni),
        in_specs=in_specs,
        out_specs=out_specs,
        out_shape=out_shape,
        scratch_shapes=[pltpu.VMEM((tn, k), BF16)],
        input_output_aliases=aliases,
        compiler_params=_params("parallel", "arbitrary"),
        name=name,
    )(*operands)


def _gates_kernel(x_ref, wa_ref, wb_ref, o_ref, w_bf, *, shift):
    @pl.when(pl.program_id(1) == 0)
    def _cast_weight_tile():
        w = jnp.concatenate([wa_ref[shift:, :], wb_ref[...]], axis=0)
        w_bf[...] = w.astype(BF16)

    o_ref[...] = lax.dot_general(x_ref[...], w_bf[...], NT_DIMS, preferred_element_type=F32).astype(o_ref.dtype)


def _gates(xn, w_t, layer):
    m, k = xn.shape
    tm = min(m, PROJ_TM_BF16)
    tn = PROJ_TN
    base = W_GM // tn * tn
    shift = W_GM - base
    assert shift % SUBLANES == 0 and tn % shift == 0
    return pl.pallas_call(
        functools.partial(_gates_kernel, shift=shift),
        grid=(2 * D_MODEL // tn, pl.cdiv(m, tm)),
        in_specs=[pl.BlockSpec((tm, k), lambda j, i: (i, 0)),
                  pl.BlockSpec((None, tn, k), lambda j, i: (layer, base // tn + j, 0)),
                  pl.BlockSpec((None, shift, k), lambda j, i: (layer, (base + (j + 1) * tn) // shift, 0))],
        out_specs=pl.BlockSpec((tm, tn), lambda j, i: (i, j)),
        out_shape=jax.ShapeDtypeStruct((m, 2 * D_MODEL), BF16),
        scratch_shapes=[pltpu.VMEM((tn, k), BF16)],
        compiler_params=_params("parallel", "arbitrary"),
        name="proj_gates",
    )(xn, w_t, w_t)


def _merge_kernel(oa_ref, ob_ref, wa_ref, wb_ref, ga_ref, gb_ref, ba_ref, bb_ref, o_ref, wa_bf, wb_bf):
    @pl.when(pl.program_id(1) == 0)
    def _cast_weight_tiles():
        wa_bf[...] = wa_ref[...].astype(BF16)
        wb_bf[...] = wb_ref[...].astype(BF16)

    ya = jnp.dot(oa_ref[...], wa_bf[...], preferred_element_type=F32)
    yb = jnp.dot(ob_ref[...], wb_bf[...], preferred_element_type=F32)
    ga = _sigmoid(ga_ref[...].astype(F32) + ba_ref[...])
    gb = _sigmoid(gb_ref[...].astype(F32) + bb_ref[...])
    o_ref[...] = (ga * ya + gb * yb).astype(o_ref.dtype)


def _merge(oa, ob, w_a, w_b, layer, gates, b_merge):
    m = oa.shape[0]
    tm = min(m, 1024)
    tn = 1024
    nj = D_MODEL // tn
    bm = b_merge.reshape(1, 2 * D_MODEL)
    return pl.pallas_call(
        _merge_kernel,
        grid=(nj, pl.cdiv(m, tm)),
        in_specs=[pl.BlockSpec((tm, MOBA_WIDTH), lambda j, i: (i, 0)),
                  pl.BlockSpec((tm, GLA_V_WIDTH), lambda j, i: (i, 0)),
                  pl.BlockSpec((None, MOBA_WIDTH, tn), lambda j, i: (layer, 0, j)),
                  pl.BlockSpec((None, GLA_V_WIDTH, tn), lambda j, i: (layer, 0, j)),
                  pl.BlockSpec((tm, tn), lambda j, i: (i, j)),
                  pl.BlockSpec((tm, tn), lambda j, i: (i, nj + j)),
                  pl.BlockSpec((1, tn), lambda j, i: (0, j)),
                  pl.BlockSpec((1, tn), lambda j, i: (0, nj + j))],
        out_specs=pl.BlockSpec((tm, tn), lambda j, i: (i, j)),
        out_shape=jax.ShapeDtypeStruct((m, D_MODEL), BF16),
        scratch_shapes=[pltpu.VMEM((MOBA_WIDTH, tn), BF16), pltpu.VMEM((GLA_V_WIDTH, tn), BF16)],
        compiler_params=_params("parallel", "arbitrary"),
        name="gated_merge",
    )(oa, ob, w_a, w_b, gates, gates, bm, bm)


def _out_kernel(m_ref, w_ref, x_ref, g_ref, *refs, emit_x):
    w_bf = refs[-1]

    @pl.when(pl.program_id(0) == 0)
    def _cast_weight():
        w_bf[...] = w_ref[...].astype(BF16)

    x_new = x_ref[...] + jnp.dot(m_ref[...], w_bf[...], preferred_element_type=F32)
    r = lax.rsqrt(jnp.mean(x_new * x_new, axis=-1, keepdims=True) + RMS_EPS)
    xn = x_new * r * g_ref[...]
    if emit_x:
        w1_ref, w2_ref, b_ref, x_out, xn_out, la_out = refs[:-1]
        x_out[...] = x_new
        xn_bf = xn.astype(xn_out.dtype)
        xn_out[...] = xn_bf
        la_out[...] = _log_forget_gate(xn_bf, w1_ref, w2_ref, b_ref)
    else:
        refs[0][...] = xn.astype(refs[0].dtype)


def _out_proj(merged, w_out, layer, x, g_next, gate_next):
    m = x.shape[0]
    tm = min(m, 512)
    row = pl.BlockSpec((tm, D_MODEL), lambda i: (i, 0))
    in_specs = [row,
                pl.BlockSpec((None, D_MODEL, D_MODEL), lambda i: (layer, 0, 0), pipeline_mode=pl.Buffered(1)),
                row,
                pl.BlockSpec((1, D_MODEL), lambda i: (0, 0))]
    operands = [merged, w_out, x, g_next.reshape(1, D_MODEL)]
    if gate_next is not None:
        gate_specs, gate_operands = _gate_operands(*gate_next)
        in_specs += gate_specs
        operands += gate_operands
        out_specs = [row, row, pl.BlockSpec((tm, GLA_K_WIDTH), lambda i: (i, 0))]
        out_shape = [jax.ShapeDtypeStruct((m, D_MODEL), F32),
                     jax.ShapeDtypeStruct((m, D_MODEL), BF16),
                     jax.ShapeDtypeStruct((m, GLA_K_WIDTH), F32)]
    else:
        out_specs = [row]
        out_shape = [jax.ShapeDtypeStruct((m, D_MODEL), F32)]
    return pl.pallas_call(
        functools.partial(_out_kernel, emit_x=gate_next is not None),
        grid=(pl.cdiv(m, tm),),
        in_specs=in_specs,
        out_specs=out_specs,
        out_shape=out_shape,
        scratch_shapes=[pltpu.VMEM((D_MODEL, D_MODEL), BF16)],
        compiler_params=_params("arbitrary"),
        name="out_proj",
    )(*operands)


def _top_k_mask(s, idx, n_cand, axis):
    sel = jnp.zeros(s.shape, F32)
    for _ in range(MOBA_TOPK):
        mx = jnp.max(s, axis=axis, keepdims=True)
        cand = jnp.where((s == mx) & (mx > -jnp.inf), idx, float(n_cand))
        first = jnp.min(cand, axis=axis, keepdims=True)
        pick = idx == first
        sel = jnp.where(pick, 1.0, sel)
        s = jnp.where(pick, -jnp.inf, s)
    return sel


def _moba_prompt_kernel(pt_ref, q_ref, z_ref, k_ref, vt_ref, mean_ref, ck_ref, o_ref, pm_ref,
                        sel_scr, acc_scr, lg_scr, page_buf, page_sem, *, nblk, heads, group, layer, pages_per_step):
    j = pl.program_id(2)

    step = (pl.program_id(0) * pl.num_programs(1) + pl.program_id(1)) * nblk + j
    n_steps = pl.num_programs(0) * pl.num_programs(1) * nblk
    pages_per_seq = pt_ref.shape[1]
    pages_per_block = MOBA_BLOCK // page_buf.shape[2]
    n_bufs = page_buf.shape[0]

    def page_copies(s):
        first = s * pages_per_step
        seq = first // pages_per_seq
        pg0 = first % pages_per_seq
        slot = s % n_bufs
        return [pltpu.make_async_copy(ck_ref.at[layer, pt_ref[seq, pg0 + p]], page_buf.at[slot, p],
                                      page_sem.at[slot]) for p in range(pages_per_step)]

    @pl.when(step == 0)
    def _first_pages():
        for ahead in range(n_bufs - 1):
            for cp in page_copies(step + ahead):
                cp.start()

    @pl.when(step + (n_bufs - 1) < n_steps)
    def _next_pages():
        for cp in page_copies(step + (n_bufs - 1)):
            cp.start()

    for cp in page_copies(step):
        cp.wait()
    first_page = step * pages_per_step
    for i in range(pages_per_step // pages_per_block):
        tot = None
        for pg in range(pages_per_block):
            part = jnp.sum(page_buf[step % n_bufs, i * pages_per_block + pg], axis=0)
            tot = part if tot is None else tot + part
        pm_ref[first_page // pages_per_seq, (first_page % pages_per_seq) // pages_per_block + i] = (
            tot * (1.0 / MOBA_BLOCK))

    ones_rows = jnp.ones((MOBA_DENOM_ROWS, MOBA_BLOCK), BF16)
    lanes = [slice(h * HEAD_DIM, (h + 1) * HEAD_DIM) for h in range(heads)]
    qs = [q_ref[:, ln] for ln in lanes]

    def logits(h, n):
        r = pl.multiple_of(n * MOBA_BLOCK, MOBA_BLOCK)
        return lax.dot_general(k_ref[pl.ds(r, MOBA_BLOCK), lanes[h]], qs[h], NT_DIMS, preferred_element_type=F32)

    def weighted_values(h, n, p):
        lhs = jnp.concatenate([vt_ref[n, lanes[h], :], ones_rows], axis=0)
        return jnp.dot(lhs, p, preferred_element_type=F32)

    def trip_blocks(i):
        return [jnp.minimum(group * i + g, nblk - 1) for g in range(group)]

    scores = [lax.dot_general(mean_ref[:, lanes[h]].astype(BF16), qs[h], NT_DIMS, preferred_element_type=F32)
              for h in range(heads)]
    blk = lax.broadcasted_iota(jnp.int32, scores[0].shape, 0)
    for h in range(heads):
        sel_scr[h] = _top_k_mask(jnp.where(blk < j, scores[h], -jnp.inf), blk.astype(F32), nblk, 0)

    own = [logits(h, j) for h in range(heads)]
    for h in range(heads):
        for g, n in enumerate(trip_blocks(0)):
            lg_scr[h * group + g] = logits(h, n)
    key = lax.broadcasted_iota(jnp.int32, own[0].shape, 0)
    qry = lax.broadcasted_iota(jnp.int32, own[0].shape, 1)
    m0, p0 = [], []
    for h in range(heads):
        lg = jnp.where(key <= qry, own[h], NEG_INF)
        m = jnp.max(lg, axis=0, keepdims=True)
        m0.append(m)
        p0.append(jnp.exp2(lg - m).astype(BF16))
    for h in range(heads):
        acc_scr[h] = weighted_values(h, j, p0[h])

    def past_blocks(i, carry):
        ms = carry
        ns = trip_blocks(i)
        ns_next = trip_blocks(i + 1)
        new_m = []
        for h in range(heads):
            lgs = [lg_scr[h * group + g] for g in range(group)]
            for g, n in enumerate(ns_next):
                lg_scr[h * group + g] = logits(h, n)
            chosen = [(sel_scr[h, pl.ds(n, 1), :] > 0.0) & (group * i + g < j) for g, n in enumerate(ns)]
            m_new = ms[h]
            for lg, ch in zip(lgs, chosen):
                m_new = jnp.maximum(m_new, jnp.where(ch, jnp.max(lg, axis=0, keepdims=True), NEG_INF))
            alpha = jnp.exp2(ms[h] - m_new)
            pv = None
            for lg, ch, n in zip(lgs, chosen, ns):
                p = jnp.exp2(lg - jnp.where(ch, m_new, -NEG_INF))
                part = weighted_values(h, n, p.astype(BF16))
                pv = part if pv is None else pv + part
            acc_scr[h] = alpha * acc_scr[h] + pv
            new_m.append(m_new)
        return tuple(new_m)

    lax.fori_loop(0, (j + group - 1) // group, past_blocks, tuple(m0))

    for h in range(heads):
        z = z_ref[:, lanes[h]].astype(F32)
        out_t = acc_scr[h, :HEAD_DIM, :] / acc_scr[h, HEAD_DIM:HEAD_DIM + 1, :]
        o_ref[:, lanes[h]] = (jnp.transpose(out_t) * _silu(z)).astype(o_ref.dtype)


MOBA_HEADS_PER_STEP = 4
PAGE_LOOKAHEAD = 2
MOBA_DENOM_ROWS = 16
MOBA_BLOCKS_PER_TRIP = 2


def _moba_prompt(main, k16, vt16, means, cache_k, page_table, layer, batch, seq):
    nblk = seq // MOBA_BLOCK
    hp = MOBA_HEADS_PER_STEP
    width = hp * HEAD_DIM
    dec_b, n_pages = page_table.shape
    page = cache_k.shape[2]
    pages_per_block = MOBA_BLOCK // page
    n_steps = batch * (MOBA_HEADS // hp) * nblk
    pages_per_step = dec_b * n_pages // n_steps
    assert pages_per_step * n_steps == dec_b * n_pages, "pages must split evenly over the grid steps"
    assert n_pages % pages_per_step == 0 and pages_per_step % pages_per_block == 0
    n_past_blocks = n_pages // pages_per_block
    qmap = lambda b, h, j, pt: (b * nblk + j, COL_QA // width + h)
    zmap = lambda b, h, j, pt: (b * nblk + j, COL_ZA // width + h)
    return pl.pallas_call(
        functools.partial(_moba_prompt_kernel, nblk=nblk, heads=hp, group=MOBA_BLOCKS_PER_TRIP,
                          layer=layer, pages_per_step=pages_per_step),
        grid_spec=pltpu.PrefetchScalarGridSpec(
            num_scalar_prefetch=1,
            grid=(batch, MOBA_HEADS // hp, nblk),
            in_specs=[pl.BlockSpec((MOBA_BLOCK, width), qmap),
                      pl.BlockSpec((MOBA_BLOCK, width), zmap),
                      pl.BlockSpec((seq, width), lambda b, h, j, pt: (b, h)),
                      pl.BlockSpec((nblk, width, MOBA_BLOCK), lambda b, h, j, pt: (b, h, 0)),
                      pl.BlockSpec((None, nblk, width), lambda b, h, j, pt: (b, 0, h)),
                      pl.BlockSpec(memory_space=pl.ANY)],
            out_specs=[pl.BlockSpec((MOBA_BLOCK, width), lambda b, h, j, pt: (b * nblk + j, h)),
                       pl.BlockSpec((dec_b, n_past_blocks, MOBA_HEADS, HEAD_DIM), lambda b, h, j, pt: (0, 0, 0, 0))],
            scratch_shapes=[pltpu.VMEM((hp, nblk, MOBA_BLOCK), F32),
                            pltpu.VMEM((hp, HEAD_DIM + MOBA_DENOM_ROWS, MOBA_BLOCK), F32),
                            pltpu.VMEM((hp * MOBA_BLOCKS_PER_TRIP, MOBA_BLOCK, MOBA_BLOCK), F32),
                            pltpu.VMEM((PAGE_LOOKAHEAD + 1, pages_per_step, page, MOBA_HEADS, HEAD_DIM), F32),
                            pltpu.SemaphoreType.DMA((PAGE_LOOKAHEAD + 1,))],
        ),
        out_shape=[jax.ShapeDtypeStruct((batch * seq, MOBA_WIDTH), BF16),
                   jax.ShapeDtypeStruct((dec_b, n_past_blocks, MOBA_HEADS, HEAD_DIM), F32)],
        compiler_params=_params("arbitrary", "arbitrary", "arbitrary"),
        name="moba_prompt",
    )(page_table, main, main, k16, vt16, means, cache_k)


def _gla_kernel(q_ref, k_ref, v_ref, g_ref, z_ref, s0_ref, gn_ref, o_ref, sout_ref, s_scr,
                *, chunk, sub, n_valid, heads):
    t = pl.program_id(2)

    @pl.when(t == 0)
    def _load_state():
        s_scr[...] = s0_ref[...]

    hs = range(heads)
    dk = [slice(h * GLA_DK, (h + 1) * GLA_DK) for h in hs]
    dv = [slice(h * GLA_DV, (h + 1) * GLA_DV) for h in hs]
    ri = lax.broadcasted_iota(jnp.int32, (chunk, chunk), 0)
    ci = lax.broadcasted_iota(jnp.int32, (chunk, chunk), 1)
    causal = ci <= ri
    tri = jnp.where(causal, 1.0, 0.0).astype(BF16)
    live = lax.broadcasted_iota(jnp.int32, (chunk, GLA_DK), 0) < n_valid
    nsub = chunk // sub

    ks, b3s = [], []
    for h in hs:
        k = k_ref[:, dk[h]].astype(F32)
        g = g_ref[:, dk[h]]
        if n_valid < chunk:
            g = jnp.where(live, g, 0.0)
            k = jnp.where(live, k, 0.0)
        g_hi = g.astype(BF16)
        g_r1 = g - g_hi.astype(F32)
        g_mid = g_r1.astype(BF16)
        g_lo = (g_r1 - g_mid.astype(F32)).astype(BF16)
        ks.append(k)
        b3s.append(jnp.dot(tri, jnp.concatenate([g_hi, g_mid, g_lo], axis=1), preferred_element_type=F32))

    bs, qs, scores = [], [], []
    for h in hs:
        b3 = b3s[h]
        b = (b3[:, :GLA_DK] + b3[:, GLA_DK:2 * GLA_DK]) + b3[:, 2 * GLA_DK:]
        q = q_ref[:, dk[h]].astype(F32) * (GLA_DK ** -0.5)
        ends = [b[(i + 1) * sub - 1:(i + 1) * sub] for i in range(nsub)]
        own_end = jnp.concatenate([jnp.broadcast_to(e, (sub, GLA_DK)) for e in ends], axis=0)
        k_rel = (ks[h] * jnp.exp(own_end - b)).astype(BF16)
        q_parts, k_rows = [], []
        for i in range(nsub):
            lo = i * sub
            q_rel = (q[lo:] * jnp.exp(b[lo:] - ends[i])).astype(BF16)
            q_parts.append(q_rel if lo == 0 else
                           jnp.concatenate([jnp.zeros((lo, GLA_DK), BF16), q_rel], axis=0))
            k_rows.append(jnp.concatenate(
                [k_rel[lo:lo + sub] if c == i else jnp.zeros((sub, GLA_DK), BF16) for c in range(nsub)], axis=1)
                if nsub > 1 else k_rel)
        q_cat = jnp.concatenate(q_parts, axis=1) if nsub > 1 else q_parts[0]
        k_cat = jnp.concatenate(k_rows, axis=0) if nsub > 1 else k_rows[0]
        scores.append(lax.dot_general(q_cat, k_cat, NT_DIMS, preferred_element_type=F32))
        bs.append(b)
        qs.append(q)

    outs, new_states = [], []
    for h in hs:
        b = bs[h]
        b_last = b[chunk - 1:chunk]
        v = v_ref[:, dv[h]]
        state = s_scr[h]
        a = jnp.where(causal, scores[h], 0.0)
        outs.append(jnp.dot(a.astype(BF16), v, preferred_element_type=F32)
                    + jnp.dot((qs[h] * jnp.exp(b)).astype(BF16), state.astype(BF16), preferred_element_type=F32))
        k_dec = (ks[h] * jnp.exp(b_last - b)).astype(BF16)
        update = lax.dot_general(k_dec, v, TN_DIMS, preferred_element_type=F32)
        decay_t = jnp.transpose(jnp.broadcast_to(jnp.exp(b_last), (GLA_DK, GLA_DK)))
        new_states.append(jnp.concatenate([decay_t] * (GLA_DV // GLA_DK), axis=1) * state + update)

    for h in hs:
        s_scr[h] = new_states[h]
        o = outs[h]
        r = lax.rsqrt(jnp.mean(o * o, axis=-1, keepdims=True) + RMS_EPS)
        z = z_ref[:, dv[h]].astype(F32)
        o_ref[:, dv[h]] = (o * r * gn_ref[h] * _silu(z)).astype(o_ref.dtype)

    @pl.when(t == pl.num_programs(2) - 1)
    def _store_state():
        for h in hs:
            sout_ref[h] = new_states[h]


GLA_HEADS_PER_STEP = 4


def _gla(main, log_a, s0, gn, batch, rows_per_seq, chunk, sub, n_valid):
    nt = rows_per_seq // chunk
    m = main.shape[0]
    hp = GLA_HEADS_PER_STEP
    wk, wv = hp * GLA_DK, hp * GLA_DV
    rmap = lambda off: (lambda b, h, t: (b * nt + t, off + h))
    smap = lambda b, h, t: (b, h, 0, 0)
    return pl.pallas_call(
        functools.partial(_gla_kernel, chunk=chunk, sub=sub, n_valid=n_valid, heads=hp),
        grid=(batch, GLA_HEADS // hp, nt),
        in_specs=[pl.BlockSpec((chunk, wk), rmap(COL_QB // wk)),
                  pl.BlockSpec((chunk, wk), rmap(COL_KB // wk)),
                  pl.BlockSpec((chunk, wv), rmap(COL_VB // wv)),
                  pl.BlockSpec((chunk, wk), rmap(0)),
                  pl.BlockSpec((chunk, wv), rmap(COL_ZB // wv)),
                  pl.BlockSpec((None, hp, GLA_DK, GLA_DV), smap),
                  pl.BlockSpec((hp, 1, GLA_DV), lambda b, h, t: (h, 0, 0))],
        out_specs=[pl.BlockSpec((chunk, wv), rmap(0)),
                   pl.BlockSpec((None, hp, GLA_DK, GLA_DV), smap)],
        out_shape=[jax.ShapeDtypeStruct((m, GLA_V_WIDTH), BF16),
                   jax.ShapeDtypeStruct((batch, GLA_HEADS, GLA_DK, GLA_DV), F32)],
        scratch_shapes=[pltpu.VMEM((hp, GLA_DK, GLA_DV), F32)],
        compiler_params=_params("parallel", "parallel", "arbitrary"),
        name="gla",
    )(main, main, main, log_a, main, s0, gn.reshape(GLA_HEADS, 1, GLA_DV))


def _sample_select_kernel(q_ref, mean_ref, o_ref, *, n_blocks):
    q = q_ref[...]
    means = mean_ref[...].astype(BF16)
    rows = q.shape[0]
    ncol = n_blocks * MOBA_HEADS
    col = lax.broadcasted_iota(jnp.int32, (rows, ncol), 1)
    col_head = col % MOBA_HEADS
    col_f = col.astype(F32)
    s_all = jnp.zeros((rows, ncol), F32)
    for h in range(MOBA_HEADS):
        s_h = lax.dot_general(q[:, h * HEAD_DIM:(h + 1) * HEAD_DIM], means, NT_DIMS, preferred_element_type=F32)
        s_all = jnp.where(col_head == h, s_h, s_all)
    lane = lax.broadcasted_iota(jnp.int32, o_ref.shape, 1)
    out = jnp.zeros(o_ref.shape, jnp.int32)
    for h in range(MOBA_HEADS):
        s = jnp.where(col_head == h, s_all, -jnp.inf)
        for t in range(MOBA_TOPK):
            mx = jnp.max(s, axis=1, keepdims=True)
            cand = jnp.where((s == mx) & (mx > -jnp.inf), col_f, float(ncol))
            first = jnp.min(cand, axis=1, keepdims=True)
            s = jnp.where(col_f == first, -jnp.inf, s)
            blk = first.astype(jnp.int32) // MOBA_HEADS
            out = jnp.where(lane == h * MOBA_TOPK + t, blk, out)
    o_ref[...] = out


def _sample_select(main, means, dec_b):
    n_blocks = means.shape[1]
    means2 = means.reshape(dec_b, n_blocks * MOBA_HEADS, HEAD_DIM)
    return pl.pallas_call(
        functools.partial(_sample_select_kernel, n_blocks=n_blocks),
        grid=(dec_b,),
        in_specs=[pl.BlockSpec((SAMPLE_ROWS, MOBA_WIDTH), lambda b: (b, COL_QA // MOBA_WIDTH)),
                  pl.BlockSpec((None, n_blocks * MOBA_HEADS, HEAD_DIM), lambda b: (b, 0, 0))],
        out_specs=pl.BlockSpec((None, SAMPLE_ROWS, LANES), lambda b: (b, 0, 0)),
        out_shape=jax.ShapeDtypeStruct((dec_b, SAMPLE_ROWS, LANES), jnp.int32),
        compiler_params=_params("parallel"),
        name="moba_sample_select",
    )(main, means2)


def _moba_sample_kernel(pt_ref, sel_ref, q_ref, z_ref, ko_ref, vo_ref, ck_ref, cv_ref, o_ref,
                        kbuf, vbuf, sem, *, layer, n_new, page):
    n_heads = pl.num_programs(1)
    step = pl.program_id(0) * n_heads + pl.program_id(1)
    n_steps = pl.num_programs(0) * n_heads
    pages_per_block = MOBA_BLOCK // page
    n_slots = n_new * MOBA_TOPK * pages_per_block

    def page_copies(s):
        b, h = s // n_heads, s % n_heads
        buf = s % 2
        copies = []
        for slot in range(n_slots):
            qi, rest = divmod(slot, MOBA_TOPK * pages_per_block)
            rank, pg = divmod(rest, pages_per_block)
            blk = sel_ref[b, qi * (MOBA_HEADS * MOBA_TOPK) + h * MOBA_TOPK + rank]
            pid = pt_ref[b, blk * pages_per_block + pg]
            dst = pl.ds(slot * page, page)
            copies.append(pltpu.make_async_copy(ck_ref.at[layer, pid, :, h, :], kbuf.at[buf, dst, :],
                                                sem.at[buf, 0]))
            copies.append(pltpu.make_async_copy(cv_ref.at[layer, pid, :, h, :], vbuf.at[buf, dst, :],
                                                sem.at[buf, 1]))
        return copies

    @pl.when(step == 0)
    def _first_gather():
        for cp in page_copies(step):
            cp.start()

    @pl.when(step + 1 < n_steps)
    def _next_gather():
        for cp in page_copies(step + 1):
            cp.start()

    for cp in page_copies(step):
        cp.wait()

    scale = HEAD_DIM ** -0.5
    q = q_ref[...]
    rows = q.shape[0]
    n_keys = n_slots * page
    k_sel = kbuf[step % 2]
    v_sel = vbuf[step % 2]
    lg = lax.dot_general(q, k_sel.astype(BF16), NT_DIMS, preferred_element_type=F32) * scale
    row = lax.broadcasted_iota(jnp.int32, (rows, n_keys), 0)
    key_owner = lax.broadcasted_iota(jnp.int32, (rows, n_keys), 1) // (MOBA_TOPK * MOBA_BLOCK)
    lg = jnp.where(key_owner == row, lg, NEG_INF)
    lg_own = lax.dot_general(q, ko_ref[...], NT_DIMS, preferred_element_type=F32) * scale
    r2 = lax.broadcasted_iota(jnp.int32, lg_own.shape, 0)
    c2 = lax.broadcasted_iota(jnp.int32, lg_own.shape, 1)
    lg_own = jnp.where((c2 <= r2) & (c2 < n_new), lg_own, NEG_INF)
    m = jnp.maximum(jnp.max(lg, axis=1, keepdims=True), jnp.max(lg_own, axis=1, keepdims=True))
    p = jnp.exp(lg - m)
    p_own = jnp.exp(lg_own - m)
    denom = jnp.sum(p, axis=1, keepdims=True) + jnp.sum(p_own, axis=1, keepdims=True)
    acc = (jnp.dot(p.astype(BF16), v_sel.astype(BF16), preferred_element_type=F32)
           + jnp.dot(p_own.astype(BF16), vo_ref[...], preferred_element_type=F32))
    z = z_ref[...].astype(F32)
    o_ref[...] = (acc / denom * _silu(z)).astype(o_ref.dtype)


def _moba_sample(main, k16, v16, cache_k, cache_v, page_table, sel, layer, n_new):
    dec_b = page_table.shape[0]
    page = cache_k.shape[2]
    n_keys = n_new * MOBA_TOPK * MOBA_BLOCK
    tile = lambda off: pl.BlockSpec((SAMPLE_ROWS, HEAD_DIM), lambda b, h, pt, sl: (b, off + h))
    return pl.pallas_call(
        functools.partial(_moba_sample_kernel, layer=layer, n_new=n_new, page=page),
        grid_spec=pltpu.PrefetchScalarGridSpec(
            num_scalar_prefetch=2,
            grid=(dec_b, MOBA_HEADS),
            in_specs=[tile(COL_QA // HEAD_DIM), tile(COL_ZA // HEAD_DIM), tile(0), tile(0),
                      pl.BlockSpec(memory_space=pl.ANY), pl.BlockSpec(memory_space=pl.ANY)],
            out_specs=tile(0),
            scratch_shapes=[pltpu.VMEM((2, n_keys, HEAD_DIM), F32),
                            pltpu.VMEM((2, n_keys, HEAD_DIM), F32),
                            pltpu.SemaphoreType.DMA((2, 2))],
        ),
        out_shape=jax.ShapeDtypeStruct((dec_b * SAMPLE_ROWS, MOBA_WIDTH), BF16),
        compiler_params=_params("arbitrary", "arbitrary"),
        name="moba_sample",
    )(page_table, sel, main, main, k16, v16, cache_k, cache_v)


def _main_col_block(j):
    return jnp.where(j == 0, W_QA // PROJ_TN, j + (W_ZA // PROJ_TN - 1))


def _token_proj(xn, w_t, layer, prompt, stack_depth=1, k_stacked=None, v_stacked=None):
    k_outs = ((F32, False), (BF16, False)) + (((F32, "block_mean"),) if prompt else ())
    k32, k16, *k_means = _proj(xn, w_t, layer, lambda j: W_KA // PROJ_TN, 1, k_outs, "proj_k",
                               stack_depth, k_stacked)
    v32, v16 = _proj(xn, w_t, layer, lambda j: W_VA // PROJ_TN, 1, ((F32, False), (BF16, prompt)), "proj_v",
                     stack_depth, v_stacked)
    q_scale = (HEAD_DIM ** -0.5) * LOG2E if prompt else None
    (main,) = _proj(xn, w_t, layer, _main_col_block, MAIN_WIDTH // PROJ_TN, ((BF16, False),), "proj_main",
                    first_tile_scale=q_scale)
    gates = _gates(xn, w_t, layer)
    return k32, k16, v32, v16, main, gates, k_means


def kernel(x_prompt, x_sample, cache_k, cache_v, state_gla, page_table, norm_g, w_in, w_gate2, b_gate,
           gla_norm_g, w_branch_a, w_branch_b, b_merge, w_out, final_norm_g):
    depth = w_in.shape[0]
    bp, seq, _ = x_prompt.shape
    dec_b, n_new, _ = x_sample.shape
    assert n_new <= SAMPLE_ROWS and seq % MOBA_BLOCK == 0
    assert cache_k.shape[2] * page_table.shape[1] % MOBA_BLOCK == 0
    assert (W_ZA // PROJ_TN, W_ZB // PROJ_TN) == (3, 6) and W_QB % PROJ_TN == 0

    xp = x_prompt.reshape(bp * seq, D_MODEL)
    xs = jnp.pad(x_sample, ((0, 0), (0, SAMPLE_ROWS - n_new), (0, 0))).reshape(dec_b * SAMPLE_ROWS, D_MODEL)
    zero_state = jnp.zeros((bp, GLA_HEADS, GLA_DK, GLA_DV), F32)
    w_t = jnp.swapaxes(w_in, 1, 2)
    w_g2 = jnp.pad(w_gate2.astype(BF16), ((0, 0), (0, LANES - GLA_GATE_RANK), (0, 0)))
    gate = lambda l: (w_t, l, w_g2[l], b_gate[l])
    xnp, log_a_p = _rmsnorm(xp, norm_g[0], gate(0))
    xns, log_a_s = _rmsnorm(xs, norm_g[0], gate(0))

    sp_l, ks_l, vs_l, ss_l = [], [], [], []
    kp_all = vp_all = None
    for l in range(depth):
        last = l == depth - 1
        g_next = final_norm_g if last else norm_g[l + 1]
        gate_next = None if last else gate(l + 1)

        kp_all, k16, vp_all, vt16, main, gates, (k_means,) = _token_proj(
            xnp, w_t, l, True, depth, kp_all, vp_all)
        means = k_means.reshape(bp, seq // MOBA_BLOCK, MOBA_WIDTH)
        oa, pmeans = _moba_prompt(main, k16, vt16, means, cache_k, page_table, l, bp, seq)
        ob, sp = _gla(main, log_a_p, zero_state, gla_norm_g[l], bp, seq, MOBA_BLOCK, GLA_SUB, MOBA_BLOCK)
        merged = _merge(oa, ob, w_branch_a, w_branch_b, l, gates, b_merge[l])
        outs = _out_proj(merged, w_out, l, xp, g_next, gate_next)
        if last:
            (yp,) = outs
        else:
            xp, xnp, log_a_p = outs
        sp_l.append(sp)

        k32, k16, v32, v16, main, gates, _ = _token_proj(xns, w_t, l, False)
        sel = _sample_select(main, pmeans, dec_b)
        sel = sel[:, :n_new, :MOBA_HEADS * MOBA_TOPK].reshape(dec_b, n_new * MOBA_HEADS * MOBA_TOPK)
        oa = _moba_sample(main, k16, v16, cache_k, cache_v, page_table, sel, l, n_new)
        ob, ss = _gla(main, log_a_s, state_gla[l], gla_norm_g[l], dec_b, SAMPLE_ROWS, SAMPLE_ROWS,
                      SAMPLE_ROWS, n_new)
        merged = _merge(oa, ob, w_branch_a, w_branch_b, l, gates, b_merge[l])
        outs = _out_proj(merged, w_out, l, xs, g_next, gate_next)
        if last:
            (ys,) = outs
        else:
            xs, xns, log_a_s = outs
        rows = lambda a: a.reshape(dec_b, SAMPLE_ROWS, MOBA_HEADS, HEAD_DIM)[:, :n_new]
        ks_l.append(rows(k32))
        vs_l.append(rows(v32))
        ss_l.append(ss)

    y_prompt = yp.reshape(bp, seq, D_MODEL)
    y_sample = ys.reshape(dec_b, SAMPLE_ROWS, D_MODEL)[:, :n_new]
    k_prompt = kp_all.reshape(depth, bp, seq, MOBA_HEADS, HEAD_DIM)
    v_prompt = vp_all.reshape(depth, bp, seq, MOBA_HEADS, HEAD_DIM)
    return (y_prompt, y_sample, k_prompt, v_prompt, jnp.stack(sp_l),
            jnp.stack(ks_l), jnp.stack(vs_l), jnp.stack(ss_l))
```

```python
import functools
import math

import jax
import jax.numpy as jnp
from jax import lax
from jax.experimental import pallas as pl
from jax.experimental.pallas import tpu as pltpu

D_MODEL = 2048
MOBA_HEADS = 8
HEAD_DIM = 128
MOBA_BLOCK = 256
MOBA_TOPK = 3
GLA_HEADS = 4
GLA_DK = 128
GLA_DV = 256
GLA_GATE_RANK = 16
GLA_TAU = 16.0
GLA_SUB = 32
RMS_EPS = 1e-6
NEG_INF = -1e30
MOBA_WIDTH = MOBA_HEADS * HEAD_DIM
GLA_K_WIDTH = GLA_HEADS * GLA_DK
GLA_V_WIDTH = GLA_HEADS * GLA_DV

W_QA = 0
W_KA = W_QA + MOBA_WIDTH
W_VA = W_KA + MOBA_WIDTH
W_ZA = W_VA + MOBA_WIDTH
W_QB = W_ZA + MOBA_WIDTH
W_KB = W_QB + GLA_K_WIDTH
W_VB = W_KB + GLA_K_WIDTH
W_ZB = W_VB + GLA_V_WIDTH
W_ALR = W_ZB + GLA_V_WIDTH
W_GM = W_ALR + GLA_GATE_RANK

COL_QA = 0
COL_ZA = COL_QA + MOBA_WIDTH
COL_QB = COL_ZA + MOBA_WIDTH
COL_KB = COL_QB + GLA_K_WIDTH
COL_VB = COL_KB + GLA_K_WIDTH
COL_ZB = COL_VB + GLA_V_WIDTH
MAIN_WIDTH = COL_ZB + GLA_V_WIDTH

LANES = 128
SUBLANES = 8
SAMPLE_ROWS = SUBLANES
PROJ_TN = 1024
PROJ_TM = 1024
PROJ_TM_BF16 = 2048
VMEM_LIMIT = 56 * 1024 * 1024

F32 = jnp.float32
BF16 = jnp.bfloat16
NT_DIMS = (((1,), (1,)), ((), ()))
TN_DIMS = (((0,), (0,)), ((), ()))
LOG2E = math.log2(math.e)


def _params(*sem):
    return pltpu.CompilerParams(dimension_semantics=sem, vmem_limit_bytes=VMEM_LIMIT)


def _silu(z):
    return z * (1.0 / (1.0 + jnp.exp(-z)))


def _sigmoid(z):
    return 1.0 / (1.0 + jnp.exp(-z))


def _log_forget_gate(xn, w1_ref, w2_ref, b_ref):
    a = lax.dot_general(xn, w1_ref[...].astype(BF16), NT_DIMS, preferred_element_type=F32).astype(BF16)
    y = jnp.dot(a, w2_ref[...], preferred_element_type=F32) + b_ref[...]
    log_sig = jnp.minimum(y, 0.0) - jnp.log(1.0 + jnp.exp(-jnp.abs(y)))
    return log_sig * (1.0 / GLA_TAU)


def _gate_operands(w_t, layer, w2, b):
    assert W_ALR % LANES == 0
    specs = [pl.BlockSpec((None, LANES, w_t.shape[2]), lambda i: (layer, W_ALR // LANES, 0)),
             pl.BlockSpec((LANES, GLA_K_WIDTH), lambda i: (0, 0)),
             pl.BlockSpec((1, GLA_K_WIDTH), lambda i: (0, 0))]
    return specs, [w_t, w2, b.reshape(1, GLA_K_WIDTH)]


def _rmsnorm_kernel(x_ref, g_ref, w1_ref, w2_ref, b_ref, o_ref, la_ref):
    x = x_ref[...]
    r = lax.rsqrt(jnp.mean(x * x, axis=-1, keepdims=True) + RMS_EPS)
    xn = (x * r * g_ref[...]).astype(o_ref.dtype)
    o_ref[...] = xn
    la_ref[...] = _log_forget_gate(xn, w1_ref, w2_ref, b_ref)


def _rmsnorm(x, g, gate):
    m, d = x.shape
    tm = min(m, 512)
    gate_specs, gate_operands = _gate_operands(*gate)
    return pl.pallas_call(
        _rmsnorm_kernel,
        grid=(pl.cdiv(m, tm),),
        in_specs=[pl.BlockSpec((tm, d), lambda i: (i, 0)),
                  pl.BlockSpec((1, d), lambda i: (0, 0))] + gate_specs,
        out_specs=[pl.BlockSpec((tm, d), lambda i: (i, 0)),
                   pl.BlockSpec((tm, GLA_K_WIDTH), lambda i: (i, 0))],
        out_shape=[jax.ShapeDtypeStruct((m, d), BF16),
                   jax.ShapeDtypeStruct((m, GLA_K_WIDTH), F32)],
        compiler_params=_params("parallel"),
        name="rmsnorm",
    )(x, g.reshape(1, d), *gate_operands)


def _proj_kernel(x_ref, w_ref, *refs, transposed, fill_layer, first_tile_scale):
    *o_refs, w_bf = refs[-(len(transposed) + 1):]

    @pl.when(pl.program_id(1) == 0)
    def _cast_weight_tile():
        w_bf[...] = w_ref[...].astype(BF16)

    acc = lax.dot_general(x_ref[...], w_bf[...], NT_DIMS, preferred_element_type=F32)
    if first_tile_scale is not None:
        acc = acc * jnp.where(pl.program_id(0) == 0, first_tile_scale, 1.0)
    for pos, (o_ref, tr) in enumerate(zip(o_refs, transposed)):
        if tr == "block_mean":
            for r in range(o_ref.shape[0]):
                rows = acc[r * MOBA_BLOCK:(r + 1) * MOBA_BLOCK, :]
                o_ref[r] = jnp.sum(rows, axis=0, keepdims=True) * (1.0 / MOBA_BLOCK)
        elif tr:
            for r in range(o_ref.shape[0]):
                rows = acc[r * MOBA_BLOCK:(r + 1) * MOBA_BLOCK, :]
                o_ref[r] = jnp.transpose(rows).astype(o_ref.dtype)
        elif pos == 0 and fill_layer is not None:
            for d in range(o_ref.shape[0]):
                o_ref[d] = acc.astype(o_ref.dtype) if d == fill_layer else jnp.zeros(acc.shape, o_ref.dtype)
        else:
            o_ref[...] = acc.astype(o_ref.dtype)


def _proj(xn, w_t, layer, col_block, n_blocks, outs, name, stack_depth=1, stacked=None, first_tile_scale=None):
    m, k = xn.shape
    tm = min(m, PROJ_TM_BF16 if all(dt == BF16 and not tr for dt, tr in outs) else PROJ_TM)
    tn = PROJ_TN
    n = n_blocks * tn
    ni = pl.cdiv(m, tm)
    fill_layer = layer if stack_depth > 1 and stacked is None else None
    out_specs, out_shape = [], []
    for pos, (dt, tr) in enumerate(outs):
        if tr == "block_mean":
            out_specs.append(pl.BlockSpec((tm // MOBA_BLOCK, 1, tn), lambda j, i: (i, 0, j)))
            out_shape.append(jax.ShapeDtypeStruct((m // MOBA_BLOCK, 1, n), dt))
        elif tr:
            out_specs.append(pl.BlockSpec((tm // MOBA_BLOCK, tn, MOBA_BLOCK), lambda j, i: (i, j, 0)))
            out_shape.append(jax.ShapeDtypeStruct((m // MOBA_BLOCK, n, MOBA_BLOCK), dt))
        elif pos == 0 and stack_depth > 1:
            if stacked is None:
                out_specs.append(pl.BlockSpec((stack_depth, tm, tn), lambda j, i: (0, i, j)))
            else:
                out_specs.append(pl.BlockSpec((None, tm, tn), lambda j, i: (layer, i, j)))
            out_shape.append(jax.ShapeDtypeStruct((stack_depth, m, n), dt))
        else:
            out_specs.append(pl.BlockSpec((tm, tn), lambda j, i: (i, j)))
            out_shape.append(jax.ShapeDtypeStruct((m, n), dt))
    in_specs = [pl.BlockSpec((tm, k), lambda j, i: (i, 0)),
                pl.BlockSpec((None, tn, k), lambda j, i: (layer, col_block(j), 0))]
    operands = [xn, w_t]
    aliases = {}
    if stacked is not None:
        in_specs.append(pl.BlockSpec(memory_space=pl.ANY))
        operands.append(stacked)
        aliases = {2: 0}
    return pl.pallas_call(
        functools.partial(_proj_kernel, transposed=tuple(tr for _, tr in outs), fill_layer=fill_layer,
                          first_tile_scale=first_tile_scale),
        grid=(n_blocks, ni),
        in_specs=in_specs,
        out_specs=out_specs,
        out_shape=out_shape,
        scratch_shapes=[pltpu.VMEM((tn, k), BF16)],
        input_output_aliases=aliases,
        compiler_params=_params("parallel", "arbitrary"),
        name=name,
    )(*operands)


def _gates_kernel(x_ref, wa_ref, wb_ref, o_ref, w_bf, *, shift):
    @pl.when(pl.program_id(1) == 0)
    def _cast_weight_tile():
        w = jnp.concatenate([wa_ref[shift:, :], wb_ref[...]], axis=0)
        w_bf[...] = w.astype(BF16)

    o_ref[...] = lax.dot_general(x_ref[...], w_bf[...], NT_DIMS, preferred_element_type=F32).astype(o_ref.dtype)


def _gates(xn, w_t, layer):
    m, k = xn.shape
    tm = min(m, PROJ_TM_BF16)
    tn = PROJ_TN
    base = W_GM // tn * tn
    shift = W_GM - base
    assert shift % SUBLANES == 0 and tn % shift == 0
    return pl.pallas_call(
        functools.partial(_gates_kernel, shift=shift),
        grid=(2 * D_MODEL // tn, pl.cdiv(m, tm)),
        in_specs=[pl.BlockSpec((tm, k), lambda j, i: (i, 0)),
                  pl.BlockSpec((None, tn, k), lambda j, i: (layer, base // tn + j, 0)),
                  pl.BlockSpec((None, shift, k), lambda j, i: (layer, (base + (j + 1) * tn) // shift, 0))],
        out_specs=pl.BlockSpec((tm, tn), lambda j, i: (i, j)),
        out_shape=jax.ShapeDtypeStruct((m, 2 * D_MODEL), BF16),
        scratch_shapes=[pltpu.VMEM((tn, k), BF16)],
        compiler_params=_params("parallel", "arbitrary"),
        name="proj_gates",
    )(xn, w_t, w_t)


def _merge_kernel(oa_ref, ob_ref, wa_ref, wb_ref, ga_ref, gb_ref, ba_ref, bb_ref, o_ref, wa_bf, wb_bf):
    @pl.when(pl.program_id(1) == 0)
    def _cast_weight_tiles():
        wa_bf[...] = wa_ref[...].astype(BF16)
        wb_bf[...] = wb_ref[...].astype(BF16)

    ya = jnp.dot(oa_ref[...], wa_bf[...], preferred_element_type=F32)
    yb = jnp.dot(ob_ref[...], wb_bf[...], preferred_element_type=F32)
    ga = _sigmoid(ga_ref[...].astype(F32) + ba_ref[...])
    gb = _sigmoid(gb_ref[...].astype(F32) + bb_ref[...])
    o_ref[...] = (ga * ya + gb * yb).astype(o_ref.dtype)


def _merge(oa, ob, w_a, w_b, layer, gates, b_merge):
    m = oa.shape[0]
    tm = min(m, 1024)
    tn = 1024
    nj = D_MODEL // tn
    bm = b_merge.reshape(1, 2 * D_MODEL)
    return pl.pallas_call(
        _merge_kernel,
        grid=(nj, pl.cdiv(m, tm)),
        in_specs=[pl.BlockSpec((tm, MOBA_WIDTH), lambda j, i: (i, 0)),
                  pl.BlockSpec((tm, GLA_V_WIDTH), lambda j, i: (i, 0)),
                  pl.BlockSpec((None, MOBA_WIDTH, tn), lambda j, i: (layer, 0, j)),
                  pl.BlockSpec((None, GLA_V_WIDTH, tn), lambda j, i: (layer, 0, j)),
                  pl.BlockSpec((tm, tn), lambda j, i: (i, j)),
                  pl.BlockSpec((tm, tn), lambda j, i: (i, nj + j)),
                  pl.BlockSpec((1, tn), lambda j, i: (0, j)),
                  pl.BlockSpec((1, tn), lambda j, i: (0, nj + j))],
        out_specs=pl.BlockSpec((tm, tn), lambda j, i: (i, j)),
        out_shape=jax.ShapeDtypeStruct((m, D_MODEL), BF16),
        scratch_shapes=[pltpu.VMEM((MOBA_WIDTH, tn), BF16), pltpu.VMEM((GLA_V_WIDTH, tn), BF16)],
        compiler_params=_params("parallel", "arbitrary"),
        name="gated_merge",
    )(oa, ob, w_a, w_b, gates, gates, bm, bm)


def _out_kernel(m_ref, w_ref, x_ref, g_ref, *refs, emit_x):
    w_bf = refs[-1]

    @pl.when(pl.program_id(0) == 0)
    def _cast_weight():
        w_bf[...] = w_ref[...].astype(BF16)

    x_new = x_ref[...] + jnp.dot(m_ref[...], w_bf[...], preferred_element_type=F32)
    r = lax.rsqrt(jnp.mean(x_new * x_new, axis=-1, keepdims=True) + RMS_EPS)
    xn = x_new * r * g_ref[...]
    if emit_x:
        w1_ref, w2_ref, b_ref, x_out, xn_out, la_out = refs[:-1]
        x_out[...] = x_new
        xn_bf = xn.astype(xn_out.dtype)
        xn_out[...] = xn_bf
        la_out[...] = _log_forget_gate(xn_bf, w1_ref, w2_ref, b_ref)
    else:
        refs[0][...] = xn.astype(refs[0].dtype)


def _out_proj(merged, w_out, layer, x, g_next, gate_next):
    m = x.shape[0]
    tm = min(m, 512)
    row = pl.BlockSpec((tm, D_MODEL), lambda i: (i, 0))
    in_specs = [row,
                pl.BlockSpec((None, D_MODEL, D_MODEL), lambda i: (layer, 0, 0), pipeline_mode=pl.Buffered(1)),
                row,
                pl.BlockSpec((1, D_MODEL), lambda i: (0, 0))]
    operands = [merged, w_out, x, g_next.reshape(1, D_MODEL)]
    if gate_next is not None:
        gate_specs, gate_operands = _gate_operands(*gate_next)
        in_specs += gate_specs
        operands += gate_operands
        out_specs = [row, row, pl.BlockSpec((tm, GLA_K_WIDTH), lambda i: (i, 0))]
        out_shape = [jax.ShapeDtypeStruct((m, D_MODEL), F32),
                     jax.ShapeDtypeStruct((m, D_MODEL), BF16),
                     jax.ShapeDtypeStruct((m, GLA_K_WIDTH), F32)]
    else:
        out_specs = [row]
        out_shape = [jax.ShapeDtypeStruct((m, D_MODEL), F32)]
    return pl.pallas_call(
        functools.partial(_out_kernel, emit_x=gate_next is not None),
        grid=(pl.cdiv(m, tm),),
        in_specs=in_specs,
        out_specs=out_specs,
        out_shape=out_shape,
        scratch_shapes=[pltpu.VMEM((D_MODEL, D_MODEL), BF16)],
        compiler_params=_params("arbitrary"),
        name="out_proj",
    )(*operands)


def _top_k_mask(s, idx, n_cand, axis):
    sel = jnp.zeros(s.shape, F32)
    for _ in range(MOBA_TOPK):
        mx = jnp.max(s, axis=axis, keepdims=True)
        cand = jnp.where((s == mx) & (mx > -jnp.inf), idx, float(n_cand))
        first = jnp.min(cand, axis=axis, keepdims=True)
        pick = idx == first
        sel = jnp.where(pick, 1.0, sel)
        s = jnp.where(pick, -jnp.inf, s)
    return sel


def _moba_prompt_kernel(pt_ref, q_ref, z_ref, k_ref, vt_ref, mean_ref, ck_ref, o_ref, pm_ref,
                        sel_scr, acc_scr, lg_scr, page_buf, page_sem, *, nblk, heads, group, layer, pages_per_step):
    j = pl.program_id(2)

    step = (pl.program_id(0) * pl.num_programs(1) + pl.program_id(1)) * nblk + j
    n_steps = pl.num_programs(0) * pl.num_programs(1) * nblk
    pages_per_seq = pt_ref.shape[1]
    pages_per_block = MOBA_BLOCK // page_buf.shape[2]
    n_bufs = page_buf.shape[0]

    def page_copies(s):
        first = s * pages_per_step
        seq = first // pages_per_seq
        pg0 = first % pages_per_seq
        slot = s % n_bufs
        return [pltpu.make_async_copy(ck_ref.at[layer, pt_ref[seq, pg0 + p]], page_buf.at[slot, p],
                                      page_sem.at[slot]) for p in range(pages_per_step)]

    @pl.when(step == 0)
    def _first_pages():
        for ahead in range(n_bufs - 1):
            for cp in page_copies(step + ahead):
                cp.start(priority=1)

    @pl.when(step + (n_bufs - 1) < n_steps)
    def _next_pages():
        for cp in page_copies(step + (n_bufs - 1)):
            cp.start(priority=1)

    for cp in page_copies(step):
        cp.wait()
    first_page = step * pages_per_step
    for i in range(pages_per_step // pages_per_block):
        tot = None
        for pg in range(pages_per_block):
            part = jnp.sum(page_buf[step % n_bufs, i * pages_per_block + pg], axis=0)
            tot = part if tot is None else tot + part
        pm_ref[first_page // pages_per_seq, (first_page % pages_per_seq) // pages_per_block + i] = (
            tot * (1.0 / MOBA_BLOCK))

    ones_rows = jnp.ones((MOBA_DENOM_ROWS, MOBA_BLOCK), BF16)
    lanes = [slice(h * HEAD_DIM, (h + 1) * HEAD_DIM) for h in range(heads)]
    qs = [q_ref[:, ln] for ln in lanes]

    def logits(h, n):
        r = pl.multiple_of(n * MOBA_BLOCK, MOBA_BLOCK)
        return lax.dot_general(k_ref[pl.ds(r, MOBA_BLOCK), lanes[h]], qs[h], NT_DIMS, preferred_element_type=F32)

    def weighted_values(h, n, p):
        lhs = jnp.concatenate([vt_ref[n, lanes[h], :], ones_rows], axis=0)
        return jnp.dot(lhs, p, preferred_element_type=F32)

    def trip_blocks(i):
        return [jnp.minimum(group * i + g, nblk - 1) for g in range(group)]

    scores = [lax.dot_general(mean_ref[:, lanes[h]].astype(BF16), qs[h], NT_DIMS, preferred_element_type=F32)
              for h in range(heads)]
    blk = lax.broadcasted_iota(jnp.int32, scores[0].shape, 0)
    for h in range(heads):
        sel_scr[h] = _top_k_mask(jnp.where(blk < j, scores[h], -jnp.inf), blk.astype(F32), nblk, 0)

    own = [logits(h, j) for h in range(heads)]
    for h in range(heads):
        for g, n in enumerate(trip_blocks(0)):
            lg_scr[h * group + g] = logits(h, n)
    key = lax.broadcasted_iota(jnp.int32, own[0].shape, 0)
    qry = lax.broadcasted_iota(jnp.int32, own[0].shape, 1)
    m0, p0 = [], []
    for h in range(heads):
        lg = jnp.where(key <= qry, own[h], NEG_INF)
        m = jnp.max(lg, axis=0, keepdims=True)
        m0.append(m)
        p0.append(jnp.exp2(lg - m).astype(BF16))
    for h in range(heads):
        acc_scr[h] = weighted_values(h, j, p0[h])

    def past_blocks(i, carry):
        ms = carry
        ns = trip_blocks(i)
        ns_next = trip_blocks(i + 1)
        new_m = []
        for h in range(heads):
            lgs = [lg_scr[h * group + g] for g in range(group)]
            for g, n in enumerate(ns_next):
                lg_scr[h * group + g] = logits(h, n)
            chosen = [(sel_scr[h, pl.ds(n, 1), :] > 0.0) & (group * i + g < j) for g, n in enumerate(ns)]
            m_new = ms[h]
            for lg, ch in zip(lgs, chosen):
                m_new = jnp.maximum(m_new, jnp.where(ch, jnp.max(lg, axis=0, keepdims=True), NEG_INF))
            alpha = jnp.exp2(ms[h] - m_new)
            pv = None
            for lg, ch, n in zip(lgs, chosen, ns):
                p = jnp.exp2(lg - jnp.where(ch, m_new, -NEG_INF))
                part = weighted_values(h, n, p.astype(BF16))
                pv = part if pv is None else pv + part
            acc_scr[h] = alpha * acc_scr[h] + pv
            new_m.append(m_new)
        return tuple(new_m)

    lax.fori_loop(0, (j + group - 1) // group, past_blocks, tuple(m0))

    for h in range(heads):
        z = z_ref[:, lanes[h]].astype(F32)
        out_t = acc_scr[h, :HEAD_DIM, :] / acc_scr[h, HEAD_DIM:HEAD_DIM + 1, :]
        o_ref[:, lanes[h]] = (jnp.transpose(out_t) * _silu(z)).astype(o_ref.dtype)


MOBA_HEADS_PER_STEP = 4
PAGE_LOOKAHEAD = 2
MOBA_DENOM_ROWS = 16
MOBA_BLOCKS_PER_TRIP = 2


def _moba_prompt(main, k16, vt16, means, cache_k, page_table, layer, batch, seq):
    nblk = seq // MOBA_BLOCK
    hp = MOBA_HEADS_PER_STEP
    width = hp * HEAD_DIM
    dec_b, n_pages = page_table.shape
    page = cache_k.shape[2]
    pages_per_block = MOBA_BLOCK // page
    n_steps = batch * (MOBA_HEADS // hp) * nblk
    pages_per_step = dec_b * n_pages // n_steps
    assert pages_per_step * n_steps == dec_b * n_pages, "pages must split evenly over the grid steps"
    assert n_pages % pages_per_step == 0 and pages_per_step % pages_per_block == 0
    n_past_blocks = n_pages // pages_per_block
    qmap = lambda b, h, j, pt: (b * nblk + j, COL_QA // width + h)
    zmap = lambda b, h, j, pt: (b * nblk + j, COL_ZA // width + h)
    return pl.pallas_call(
        functools.partial(_moba_prompt_kernel, nblk=nblk, heads=hp, group=MOBA_BLOCKS_PER_TRIP,
                          layer=layer, pages_per_step=pages_per_step),
        grid_spec=pltpu.PrefetchScalarGridSpec(
            num_scalar_prefetch=1,
            grid=(batch, MOBA_HEADS // hp, nblk),
            in_specs=[pl.BlockSpec((MOBA_BLOCK, width), qmap),
                      pl.BlockSpec((MOBA_BLOCK, width), zmap),
                      pl.BlockSpec((seq, width), lambda b, h, j, pt: (b, h)),
                      pl.BlockSpec((nblk, width, MOBA_BLOCK), lambda b, h, j, pt: (b, h, 0)),
                      pl.BlockSpec((None, nblk, width), lambda b, h, j, pt: (b, 0, h)),
                      pl.BlockSpec(memory_space=pl.ANY)],
            out_specs=[pl.BlockSpec((MOBA_BLOCK, width), lambda b, h, j, pt: (b * nblk + j, h)),
                       pl.BlockSpec((dec_b, n_past_blocks, MOBA_HEADS, HEAD_DIM), lambda b, h, j, pt: (0, 0, 0, 0))],
            scratch_shapes=[pltpu.VMEM((hp, nblk, MOBA_BLOCK), F32),
                            pltpu.VMEM((hp, HEAD_DIM + MOBA_DENOM_ROWS, MOBA_BLOCK), F32),
                            pltpu.VMEM((hp * MOBA_BLOCKS_PER_TRIP, MOBA_BLOCK, MOBA_BLOCK), F32),
                            pltpu.VMEM((PAGE_LOOKAHEAD + 1, pages_per_step, page, MOBA_HEADS, HEAD_DIM), F32),
                            pltpu.SemaphoreType.DMA((PAGE_LOOKAHEAD + 1,))],
        ),
        out_shape=[jax.ShapeDtypeStruct((batch * seq, MOBA_WIDTH), BF16),
                   jax.ShapeDtypeStruct((dec_b, n_past_blocks, MOBA_HEADS, HEAD_DIM), F32)],
        compiler_params=_params("arbitrary", "arbitrary", "arbitrary"),
        name="moba_prompt",
    )(page_table, main, main, k16, vt16, means, cache_k)


def _gla_kernel(q_ref, k_ref, v_ref, g_ref, z_ref, s0_ref, gn_ref, o_ref, sout_ref, s_scr,
                *, chunk, sub, n_valid, heads):
    t = pl.program_id(2)

    @pl.when(t == 0)
    def _load_state():
        s_scr[...] = s0_ref[...]

    hs = range(heads)
    dk = [slice(h * GLA_DK, (h + 1) * GLA_DK) for h in hs]
    dv = [slice(h * GLA_DV, (h + 1) * GLA_DV) for h in hs]
    ri = lax.broadcasted_iota(jnp.int32, (chunk, chunk), 0)
    ci = lax.broadcasted_iota(jnp.int32, (chunk, chunk), 1)
    causal = ci <= ri
    tri = jnp.where(causal, 1.0, 0.0).astype(BF16)
    live = lax.broadcasted_iota(jnp.int32, (chunk, GLA_DK), 0) < n_valid
    nsub = chunk // sub

    ks, b3s = [], []
    for h in hs:
        k = k_ref[:, dk[h]].astype(F32)
        g = g_ref[:, dk[h]]
        if n_valid < chunk:
            g = jnp.where(live, g, 0.0)
            k = jnp.where(live, k, 0.0)
        g_hi = g.astype(BF16)
        g_r1 = g - g_hi.astype(F32)
        g_mid = g_r1.astype(BF16)
        g_lo = (g_r1 - g_mid.astype(F32)).astype(BF16)
        ks.append(k)
        b3s.append(jnp.dot(tri, jnp.concatenate([g_hi, g_mid, g_lo], axis=1), preferred_element_type=F32))

    bs, qs, scores = [], [], []
    for h in hs:
        b3 = b3s[h]
        b = (b3[:, :GLA_DK] + b3[:, GLA_DK:2 * GLA_DK]) + b3[:, 2 * GLA_DK:]
        q = q_ref[:, dk[h]].astype(F32) * (GLA_DK ** -0.5)
        ends = [b[(i + 1) * sub - 1:(i + 1) * sub] for i in range(nsub)]
        own_end = jnp.concatenate([jnp.broadcast_to(e, (sub, GLA_DK)) for e in ends], axis=0)
        k_rel = (ks[h] * jnp.exp(own_end - b)).astype(BF16)
        q_parts, k_rows = [], []
        for i in range(nsub):
            lo = i * sub
            q_rel = (q[lo:] * jnp.exp(b[lo:] - ends[i])).astype(BF16)
            q_parts.append(q_rel if lo == 0 else
                           jnp.concatenate([jnp.zeros((lo, GLA_DK), BF16), q_rel], axis=0))
            k_rows.append(jnp.concatenate(
                [k_rel[lo:lo + sub] if c == i else jnp.zeros((sub, GLA_DK), BF16) for c in range(nsub)], axis=1)
                if nsub > 1 else k_rel)
        q_cat = jnp.concatenate(q_parts, axis=1) if nsub > 1 else q_parts[0]
        k_cat = jnp.concatenate(k_rows, axis=0) if nsub > 1 else k_rows[0]
        scores.append(lax.dot_general(q_cat, k_cat, NT_DIMS, preferred_element_type=F32))
        bs.append(b)
        qs.append(q)

    outs, new_states = [], []
    for h in hs:
        b = bs[h]
        b_last = b[chunk - 1:chunk]
        v = v_ref[:, dv[h]]
        state = s_scr[h]
        a = jnp.where(causal, scores[h], 0.0)
        outs.append(jnp.dot(a.astype(BF16), v, preferred_element_type=F32)
                    + jnp.dot((qs[h] * jnp.exp(b)).astype(BF16), state.astype(BF16), preferred_element_type=F32))
        k_dec = (ks[h] * jnp.exp(b_last - b)).astype(BF16)
        update = lax.dot_general(k_dec, v, TN_DIMS, preferred_element_type=F32)
        decay_t = jnp.transpose(jnp.broadcast_to(jnp.exp(b_last), (GLA_DK, GLA_DK)))
        new_states.append(jnp.concatenate([decay_t] * (GLA_DV // GLA_DK), axis=1) * state + update)

    for h in hs:
        s_scr[h] = new_states[h]
        o = outs[h]
        r = lax.rsqrt(jnp.mean(o * o, axis=-1, keepdims=True) + RMS_EPS)
        z = z_ref[:, dv[h]].astype(F32)
        o_ref[:, dv[h]] = (o * r * gn_ref[h] * _silu(z)).astype(o_ref.dtype)

    @pl.when(t == pl.num_programs(2) - 1)
    def _store_state():
        for h in hs:
            sout_ref[h] = new_states[h]


GLA_HEADS_PER_STEP = 4


def _gla(main, log_a, s0, gn, batch, rows_per_seq, chunk, sub, n_valid):
    nt = rows_per_seq // chunk
    m = main.shape[0]
    hp = GLA_HEADS_PER_STEP
    wk, wv = hp * GLA_DK, hp * GLA_DV
    rmap = lambda off: (lambda b, h, t: (b * nt + t, off + h))
    smap = lambda b, h, t: (b, h, 0, 0)
    return pl.pallas_call(
        functools.partial(_gla_kernel, chunk=chunk, sub=sub, n_valid=n_valid, heads=hp),
        grid=(batch, GLA_HEADS // hp, nt),
        in_specs=[pl.BlockSpec((chunk, wk), rmap(COL_QB // wk)),
                  pl.BlockSpec((chunk, wk), rmap(COL_KB // wk)),
                  pl.BlockSpec((chunk, wv), rmap(COL_VB // wv)),
                  pl.BlockSpec((chunk, wk), rmap(0)),
                  pl.BlockSpec((chunk, wv), rmap(COL_ZB // wv)),
                  pl.BlockSpec((None, hp, GLA_DK, GLA_DV), smap),
                  pl.BlockSpec((hp, 1, GLA_DV), lambda b, h, t: (h, 0, 0))],
        out_specs=[pl.BlockSpec((chunk, wv), rmap(0)),
                   pl.BlockSpec((None, hp, GLA_DK, GLA_DV), smap)],
        out_shape=[jax.ShapeDtypeStruct((m, GLA_V_WIDTH), BF16),
                   jax.ShapeDtypeStruct((batch, GLA_HEADS, GLA_DK, GLA_DV), F32)],
        scratch_shapes=[pltpu.VMEM((hp, GLA_DK, GLA_DV), F32)],
        compiler_params=_params("parallel", "parallel", "arbitrary"),
        name="gla",
    )(main, main, main, log_a, main, s0, gn.reshape(GLA_HEADS, 1, GLA_DV))


def _sample_select_kernel(q_ref, mean_ref, o_ref, *, n_blocks):
    q = q_ref[...]
    means = mean_ref[...].astype(BF16)
    rows = q.shape[0]
    ncol = n_blocks * MOBA_HEADS
    col = lax.broadcasted_iota(jnp.int32, (rows, ncol), 1)
    col_head = col % MOBA_HEADS
    col_f = col.astype(F32)
    s_all = jnp.zeros((rows, ncol), F32)
    for h in range(MOBA_HEADS):
        s_h = lax.dot_general(q[:, h * HEAD_DIM:(h + 1) * HEAD_DIM], means, NT_DIMS, preferred_element_type=F32)
        s_all = jnp.where(col_head == h, s_h, s_all)
    lane = lax.broadcasted_iota(jnp.int32, o_ref.shape, 1)
    out = jnp.zeros(o_ref.shape, jnp.int32)
    for h in range(MOBA_HEADS):
        s = jnp.where(col_head == h, s_all, -jnp.inf)
        for t in range(MOBA_TOPK):
            mx = jnp.max(s, axis=1, keepdims=True)
            cand = jnp.where((s == mx) & (mx > -jnp.inf), col_f, float(ncol))
            first = jnp.min(cand, axis=1, keepdims=True)
            s = jnp.where(col_f == first, -jnp.inf, s)
            blk = first.astype(jnp.int32) // MOBA_HEADS
            out = jnp.where(lane == h * MOBA_TOPK + t, blk, out)
    o_ref[...] = out


def _sample_select(main, means, dec_b):
    n_blocks = means.shape[1]
    means2 = means.reshape(dec_b, n_blocks * MOBA_HEADS, HEAD_DIM)
    return pl.pallas_call(
        functools.partial(_sample_select_kernel, n_blocks=n_blocks),
        grid=(dec_b,),
        in_specs=[pl.BlockSpec((SAMPLE_ROWS, MOBA_WIDTH), lambda b: (b, COL_QA // MOBA_WIDTH)),
                  pl.BlockSpec((None, n_blocks * MOBA_HEADS, HEAD_DIM), lambda b: (b, 0, 0))],
        out_specs=pl.BlockSpec((None, SAMPLE_ROWS, LANES), lambda b: (b, 0, 0)),
        out_shape=jax.ShapeDtypeStruct((dec_b, SAMPLE_ROWS, LANES), jnp.int32),
        compiler_params=_params("parallel"),
        name="moba_sample_select",
    )(main, means2)


def _moba_sample_kernel(pt_ref, sel_ref, q_ref, z_ref, ko_ref, vo_ref, ck_ref, cv_ref, o_ref,
                        kbuf, vbuf, sem, *, layer, n_new, page):
    n_heads = pl.num_programs(1)
    step = pl.program_id(0) * n_heads + pl.program_id(1)
    n_steps = pl.num_programs(0) * n_heads
    pages_per_block = MOBA_BLOCK // page
    n_slots = n_new * MOBA_TOPK * pages_per_block

    def page_copies(s):
        b, h = s // n_heads, s % n_heads
        buf = s % 2
        copies = []
        for slot in range(n_slots):
            qi, rest = divmod(slot, MOBA_TOPK * pages_per_block)
            rank, pg = divmod(rest, pages_per_block)
            blk = sel_ref[b, qi * (MOBA_HEADS * MOBA_TOPK) + h * MOBA_TOPK + rank]
            pid = pt_ref[b, blk * pages_per_block + pg]
            dst = pl.ds(slot * page, page)
            copies.append(pltpu.make_async_copy(ck_ref.at[layer, pid, :, h, :], kbuf.at[buf, dst, :],
                                                sem.at[buf, 0]))
            copies.append(pltpu.make_async_copy(cv_ref.at[layer, pid, :, h, :], vbuf.at[buf, dst, :],
                                                sem.at[buf, 1]))
        return copies

    @pl.when(step == 0)
    def _first_gather():
        for n, cp in enumerate(page_copies(step)):
            cp.start(priority=n % 2)

    @pl.when(step + 1 < n_steps)
    def _next_gather():
        for n, cp in enumerate(page_copies(step + 1)):
            cp.start(priority=n % 2)

    for cp in page_copies(step):
        cp.wait()

    scale = HEAD_DIM ** -0.5
    q = q_ref[...]
    rows = q.shape[0]
    n_keys = n_slots * page
    k_sel = kbuf[step % 2]
    v_sel = vbuf[step % 2]
    lg = lax.dot_general(q, k_sel.astype(BF16), NT_DIMS, preferred_element_type=F32) * scale
    row = lax.broadcasted_iota(jnp.int32, (rows, n_keys), 0)
    key_owner = lax.broadcasted_iota(jnp.int32, (rows, n_keys), 1) // (MOBA_TOPK * MOBA_BLOCK)
    lg = jnp.where(key_owner == row, lg, NEG_INF)
    lg_own = lax.dot_general(q, ko_ref[...], NT_DIMS, preferred_element_type=F32) * scale
    r2 = lax.broadcasted_iota(jnp.int32, lg_own.shape, 0)
    c2 = lax.broadcasted_iota(jnp.int32, lg_own.shape, 1)
    lg_own = jnp.where((c2 <= r2) & (c2 < n_new), lg_own, NEG_INF)
    m = jnp.maximum(jnp.max(lg, axis=1, keepdims=True), jnp.max(lg_own, axis=1, keepdims=True))
    p = jnp.exp(lg - m)
    p_own = jnp.exp(lg_own - m)
    denom = jnp.sum(p, axis=1, keepdims=True) + jnp.sum(p_own, axis=1, keepdims=True)
    acc = (jnp.dot(p.astype(BF16), v_sel.astype(BF16), preferred_element_type=F32)
           + jnp.dot(p_own.astype(BF16), vo_ref[...], preferred_element_type=F32))
    z = z_ref[...].astype(F32)
    o_ref[...] = (acc / denom * _silu(z)).astype(o_ref.dtype)


def _moba_sample(main, k16, v16, cache_k, cache_v, page_table, sel, layer, n_new):
    dec_b = page_table.shape[0]
    page = cache_k.shape[2]
    n_keys = n_new * MOBA_TOPK * MOBA_BLOCK
    tile = lambda off: pl.BlockSpec((SAMPLE_ROWS, HEAD_DIM), lambda b, h, pt, sl: (b, off + h))
    return pl.pallas_call(
        functools.partial(_moba_sample_kernel, layer=layer, n_new=n_new, page=page),
        grid_spec=pltpu.PrefetchScalarGridSpec(
            num_scalar_prefetch=2,
            grid=(dec_b, MOBA_HEADS),
            in_specs=[tile(COL_QA // HEAD_DIM), tile(COL_ZA // HEAD_DIM), tile(0), tile(0),
                      pl.BlockSpec(memory_space=pl.ANY), pl.BlockSpec(memory_space=pl.ANY)],
            out_specs=tile(0),
            scratch_shapes=[pltpu.VMEM((2, n_keys, HEAD_DIM), F32),
                            pltpu.VMEM((2, n_keys, HEAD_DIM), F32),
                            pltpu.SemaphoreType.DMA((2, 2))],
        ),
        out_shape=jax.ShapeDtypeStruct((dec_b * SAMPLE_ROWS, MOBA_WIDTH), BF16),
        compiler_params=_params("arbitrary", "arbitrary"),
        name="moba_sample",
    )(page_table, sel, main, main, k16, v16, cache_k, cache_v)


def _main_col_block(j):
    return jnp.where(j == 0, W_QA // PROJ_TN, j + (W_ZA // PROJ_TN - 1))


def _token_proj(xn, w_t, layer, prompt, stack_depth=1, k_stacked=None, v_stacked=None):
    k_outs = ((F32, False), (BF16, False)) + (((F32, "block_mean"),) if prompt else ())
    k32, k16, *k_means = _proj(xn, w_t, layer, lambda j: W_KA // PROJ_TN, 1, k_outs, "proj_k",
                               stack_depth, k_stacked)
    v32, v16 = _proj(xn, w_t, layer, lambda j: W_VA // PROJ_TN, 1, ((F32, False), (BF16, prompt)), "proj_v",
                     stack_depth, v_stacked)
    q_scale = (HEAD_DIM ** -0.5) * LOG2E if prompt else None
    (main,) = _proj(xn, w_t, layer, _main_col_block, MAIN_WIDTH // PROJ_TN, ((BF16, False),), "proj_main",
                    first_tile_scale=q_scale)
    gates = _gates(xn, w_t, layer)
    return k32, k16, v32, v16, main, gates, k_means


def kernel(x_prompt, x_sample, cache_k, cache_v, state_gla, page_table, norm_g, w_in, w_gate2, b_gate,
           gla_norm_g, w_branch_a, w_branch_b, b_merge, w_out, final_norm_g):
    depth = w_in.shape[0]
    bp, seq, _ = x_prompt.shape
    dec_b, n_new, _ = x_sample.shape
    assert n_new <= SAMPLE_ROWS and seq % MOBA_BLOCK == 0
    assert cache_k.shape[2] * page_table.shape[1] % MOBA_BLOCK == 0
    assert (W_ZA // PROJ_TN, W_ZB // PROJ_TN) == (3, 6) and W_QB % PROJ_TN == 0

    xp = x_prompt.reshape(bp * seq, D_MODEL)
    xs = jnp.pad(x_sample, ((0, 0), (0, SAMPLE_ROWS - n_new), (0, 0))).reshape(dec_b * SAMPLE_ROWS, D_MODEL)
    zero_state = jnp.zeros((bp, GLA_HEADS, GLA_DK, GLA_DV), F32)
    w_t = jnp.swapaxes(w_in, 1, 2)
    w_g2 = jnp.pad(w_gate2.astype(BF16), ((0, 0), (0, LANES - GLA_GATE_RANK), (0, 0)))
    gate = lambda l: (w_t, l, w_g2[l], b_gate[l])
    xnp, log_a_p = _rmsnorm(xp, norm_g[0], gate(0))
    xns, log_a_s = _rmsnorm(xs, norm_g[0], gate(0))

    sp_l, ks_l, vs_l, ss_l = [], [], [], []
    kp_all = vp_all = None
    for l in range(depth):
        last = l == depth - 1
        g_next = final_norm_g if last else norm_g[l + 1]
        gate_next = None if last else gate(l + 1)

        kp_all, k16, vp_all, vt16, main, gates, (k_means,) = _token_proj(
            xnp, w_t, l, True, depth, kp_all, vp_all)
        means = k_means.reshape(bp, seq // MOBA_BLOCK, MOBA_WIDTH)
        oa, pmeans = _moba_prompt(main, k16, vt16, means, cache_k, page_table, l, bp, seq)
        ob, sp = _gla(main, log_a_p, zero_state, gla_norm_g[l], bp, seq, MOBA_BLOCK, GLA_SUB, MOBA_BLOCK)
        merged = _merge(oa, ob, w_branch_a, w_branch_b, l, gates, b_merge[l])
        outs = _out_proj(merged, w_out, l, xp, g_next, gate_next)
        if last:
            (yp,) = outs
        else:
            xp, xnp, log_a_p = outs
        sp_l.append(sp)

        k32, k16, v32, v16, main, gates, _ = _token_proj(xns, w_t, l, False)
        sel = _sample_select(main, pmeans, dec_b)
        sel = sel[:, :n_new, :MOBA_HEADS * MOBA_TOPK].reshape(dec_b, n_new * MOBA_HEADS * MOBA_TOPK)
        oa = _moba_sample(main, k16, v16, cache_k, cache_v, page_table, sel, l, n_new)
        ob, ss = _gla(main, log_a_s, state_gla[l], gla_norm_g[l], dec_b, SAMPLE_ROWS, SAMPLE_ROWS,
                      SAMPLE_ROWS, n_new)
        merged = _merge(oa, ob, w_branch_a, w_branch_b, l, gates, b_merge[l])
        outs = _out_proj(merged, w_out, l, xs, g_next, gate_next)
        if last:
            (ys,) = outs
        else:
            xs, xns, log_a_s = outs
        rows = lambda a: a.reshape(dec_b, SAMPLE_ROWS, MOBA_HEADS, HEAD_DIM)[:, :n_new]
        ks_l.append(rows(k32))
        vs_l.append(rows(v32))
        ss_l.append(ss)

    y_prompt = yp.reshape(bp, seq, D_MODEL)
    y_sample = ys.reshape(dec_b, SAMPLE_ROWS, D_MODEL)[:, :n_new]
    k_prompt = kp_all.reshape(depth, bp, seq, MOBA_HEADS, HEAD_DIM)
    v_prompt = vp_all.reshape(depth, bp, seq, MOBA_HEADS, HEAD_DIM)
    return (y_prompt, y_sample, k_prompt, v_prompt, jnp.stack(sp_l),
            jnp.stack(ks_l), jnp.stack(vs_l), jnp.stack(ss_l))
```

```python
import functools
import math

import jax
import jax.numpy as jnp
from jax import lax
from jax.experimental import pallas as pl
from jax.experimental.pallas import tpu as pltpu

D_MODEL = 2048
MOBA_HEADS = 8
HEAD_DIM = 128
MOBA_BLOCK = 256
MOBA_TOPK = 3
GLA_HEADS = 4
GLA_DK = 128
GLA_DV = 256
GLA_GATE_RANK = 16
GLA_TAU = 16.0
GLA_SUB = 32
RMS_EPS = 1e-6
NEG_INF = -1e30
MOBA_WIDTH = MOBA_HEADS * HEAD_DIM
GLA_K_WIDTH = GLA_HEADS * GLA_DK
GLA_V_WIDTH = GLA_HEADS * GLA_DV

W_QA = 0
W_KA = W_QA + MOBA_WIDTH
W_VA = W_KA + MOBA_WIDTH
W_ZA = W_VA + MOBA_WIDTH
W_QB = W_ZA + MOBA_WIDTH
W_KB = W_QB + GLA_K_WIDTH
W_VB = W_KB + GLA_K_WIDTH
W_ZB = W_VB + GLA_V_WIDTH
W_ALR = W_ZB + GLA_V_WIDTH
W_GM = W_ALR + GLA_GATE_RANK

COL_QA = 0
COL_ZA = COL_QA + MOBA_WIDTH
COL_QB = COL_ZA + MOBA_WIDTH
COL_KB = COL_QB + GLA_K_WIDTH
COL_VB = COL_KB + GLA_K_WIDTH
COL_ZB = COL_VB + GLA_V_WIDTH
MAIN_WIDTH = COL_ZB + GLA_V_WIDTH

LANES = 128
SUBLANES = 8
SAMPLE_ROWS = SUBLANES
PROJ_TN = 1024
PROJ_TM = 1024
PROJ_TM_BF16 = 2048
MERGE_ROW_CHUNK = 256
VMEM_LIMIT = 56 * 1024 * 1024

F32 = jnp.float32
BF16 = jnp.bfloat16
NT_DIMS = (((1,), (1,)), ((), ()))
TN_DIMS = (((0,), (0,)), ((), ()))
LOG2E = math.log2(math.e)


def _params(*sem):
    return pltpu.CompilerParams(dimension_semantics=sem, vmem_limit_bytes=VMEM_LIMIT)


def _silu(z):
    return z * (1.0 / (1.0 + jnp.exp(-z)))


def _sigmoid(z):
    return 1.0 / (1.0 + jnp.exp(-z))


def _log_forget_gate(xn, w1_ref, w2_ref, b_ref):
    a = lax.dot_general(xn, w1_ref[...].astype(BF16), NT_DIMS, preferred_element_type=F32).astype(BF16)
    y = jnp.dot(a, w2_ref[...], preferred_element_type=F32) + b_ref[...]
    log_sig = jnp.minimum(y, 0.0) - jnp.log(1.0 + jnp.exp(-jnp.abs(y)))
    return log_sig * (1.0 / GLA_TAU)


def _gate_operands(w_t, layer, w2, b):
    assert W_ALR % LANES == 0
    specs = [pl.BlockSpec((None, LANES, w_t.shape[2]), lambda i: (layer, W_ALR // LANES, 0)),
             pl.BlockSpec((LANES, GLA_K_WIDTH), lambda i: (0, 0)),
             pl.BlockSpec((1, GLA_K_WIDTH), lambda i: (0, 0))]
    return specs, [w_t, w2, b.reshape(1, GLA_K_WIDTH)]


def _rmsnorm_kernel(x_ref, g_ref, w1_ref, w2_ref, b_ref, o_ref, la_ref):
    x = x_ref[...]
    r = lax.rsqrt(jnp.mean(x * x, axis=-1, keepdims=True) + RMS_EPS)
    xn = (x * r * g_ref[...]).astype(o_ref.dtype)
    o_ref[...] = xn
    la_ref[...] = _log_forget_gate(xn, w1_ref, w2_ref, b_ref)


def _rmsnorm(x, g, gate):
    m, d = x.shape
    tm = min(m, 512)
    gate_specs, gate_operands = _gate_operands(*gate)
    return pl.pallas_call(
        _rmsnorm_kernel,
        grid=(pl.cdiv(m, tm),),
        in_specs=[pl.BlockSpec((tm, d), lambda i: (i, 0)),
                  pl.BlockSpec((1, d), lambda i: (0, 0))] + gate_specs,
        out_specs=[pl.BlockSpec((tm, d), lambda i: (i, 0)),
                   pl.BlockSpec((tm, GLA_K_WIDTH), lambda i: (i, 0))],
        out_shape=[jax.ShapeDtypeStruct((m, d), BF16),
                   jax.ShapeDtypeStruct((m, GLA_K_WIDTH), F32)],
        compiler_params=_params("parallel"),
        name="rmsnorm",
    )(x, g.reshape(1, d), *gate_operands)


def _proj_kernel(x_ref, w_ref, *refs, transposed, fill_layer, first_tile_scale):
    *o_refs, w_bf = refs[-(len(transposed) + 1):]

    @pl.when(pl.program_id(1) == 0)
    def _cast_weight_tile():
        w_bf[...] = w_ref[...].astype(BF16)

    acc = lax.dot_general(x_ref[...], w_bf[...], NT_DIMS, preferred_element_type=F32)
    if first_tile_scale is not None:
        acc = acc * jnp.where(pl.program_id(0) == 0, first_tile_scale, 1.0)
    for pos, (o_ref, tr) in enumerate(zip(o_refs, transposed)):
        if tr == "block_mean":
            for r in range(o_ref.shape[0]):
                rows = acc[r * MOBA_BLOCK:(r + 1) * MOBA_BLOCK, :]
                o_ref[r] = jnp.sum(rows, axis=0, keepdims=True) * (1.0 / MOBA_BLOCK)
        elif tr:
            for r in range(o_ref.shape[0]):
                rows = acc[r * MOBA_BLOCK:(r + 1) * MOBA_BLOCK, :]
                o_ref[r] = jnp.transpose(rows).astype(o_ref.dtype)
        elif pos == 0 and fill_layer is not None:
            for d in range(o_ref.shape[0]):
                o_ref[d] = acc.astype(o_ref.dtype) if d == fill_layer else jnp.zeros(acc.shape, o_ref.dtype)
        else:
            o_ref[...] = acc.astype(o_ref.dtype)


def _proj(xn, w_t, layer, col_block, n_blocks, outs, name, stack_depth=1, stacked=None, first_tile_scale=None):
    m, k = xn.shape
    tm = min(m, PROJ_TM_BF16 if all(dt == BF16 and not tr for dt, tr in outs) else PROJ_TM)
    tn = PROJ_TN
    n = n_blocks * tn
    ni = pl.cdiv(m, tm)
    fill_layer = layer if stack_depth > 1 and stacked is None else None
    out_specs, out_shape = [], []
    for pos, (dt, tr) in enumerate(outs):
        if tr == "block_mean":
            out_specs.append(pl.BlockSpec((tm // MOBA_BLOCK, 1, tn), lambda j, i: (i, 0, j)))
            out_shape.append(jax.ShapeDtypeStruct((m // MOBA_BLOCK, 1, n), dt))
        elif tr:
            out_specs.append(pl.BlockSpec((tm // MOBA_BLOCK, tn, MOBA_BLOCK), lambda j, i: (i, j, 0)))
            out_shape.append(jax.ShapeDtypeStruct((m // MOBA_BLOCK, n, MOBA_BLOCK), dt))
        elif pos == 0 and stack_depth > 1:
            if stacked is None:
                out_specs.append(pl.BlockSpec((stack_depth, tm, tn), lambda j, i: (0, i, j)))
            else:
                out_specs.append(pl.BlockSpec((None, tm, tn), lambda j, i: (layer, i, j)))
            out_shape.append(jax.ShapeDtypeStruct((stack_depth, m, n), dt))
        else:
            out_specs.append(pl.BlockSpec((tm, tn), lambda j, i: (i, j)))
            out_shape.append(jax.ShapeDtypeStruct((m, n), dt))
    in_specs = [pl.BlockSpec((tm, k), lambda j, i: (i, 0)),
                pl.BlockSpec((None, tn, k), lambda j, i: (layer, col_block(j), 0))]
    operands = [xn, w_t]
    aliases = {}
    if stacked is not None:
        in_specs.append(pl.BlockSpec(memory_space=pl.ANY))
        operands.append(stacked)
        aliases = {2: 0}
    return pl.pallas_call(
        functools.partial(_proj_kernel, transposed=tuple(tr for _, tr in outs), fill_layer=fill_layer,
                          first_tile_scale=first_tile_scale),
        grid=(n_blocks, ni),
        in_specs=in_specs,
        out_specs=out_specs,
        out_shape=out_shape,
        scratch_shapes=[pltpu.VMEM((tn, k), BF16)],
        input_output_aliases=aliases,
        compiler_params=_params("parallel", "arbitrary"),
        name=name,
    )(*operands)


def _gates_kernel(x_ref, wa_ref, wb_ref, o_ref, w_bf, *, shift):
    @pl.when(pl.program_id(1) == 0)
    def _cast_weight_tile():
        w = jnp.concatenate([wa_ref[shift:, :], wb_ref[...]], axis=0)
        w_bf[...] = w.astype(BF16)

    o_ref[...] = lax.dot_general(x_ref[...], w_bf[...], NT_DIMS, preferred_element_type=F32).astype(o_ref.dtype)


def _gates(xn, w_t, layer):
    m, k = xn.shape
    tm = min(m, PROJ_TM_BF16)
    tn = PROJ_TN
    base = W_GM // tn * tn
    shift = W_GM - base
    assert shift % SUBLANES == 0 and tn % shift == 0
    return pl.pallas_call(
        functools.partial(_gates_kernel, shift=shift),
        grid=(2 * D_MODEL // tn, pl.cdiv(m, tm)),
        in_specs=[pl.BlockSpec((tm, k), lambda j, i: (i, 0)),
                  pl.BlockSpec((None, tn, k), lambda j, i: (layer, base // tn + j, 0)),
                  pl.BlockSpec((None, shift, k), lambda j, i: (layer, (base + (j + 1) * tn) // shift, 0))],
        out_specs=pl.BlockSpec((tm, tn), lambda j, i: (i, j)),
        out_shape=jax.ShapeDtypeStruct((m, 2 * D_MODEL), BF16),
        scratch_shapes=[pltpu.VMEM((tn, k), BF16)],
        compiler_params=_params("parallel", "arbitrary"),
        name="proj_gates",
    )(xn, w_t, w_t)


def _merge_kernel(oa_ref, ob_ref, wa_ref, wb_ref, ga_ref, gb_ref, ba_ref, bb_ref, o_ref, wa_bf, wb_bf):
    @pl.when(pl.program_id(1) == 0)
    def _cast_weight_tiles():
        wa_bf[...] = wa_ref[...].astype(BF16)
        wb_bf[...] = wb_ref[...].astype(BF16)

    rows = o_ref.shape[0]
    chunk = min(rows, MERGE_ROW_CHUNK)
    for r in range(0, rows, chunk):
        sl = slice(r, r + chunk)
        ya = jnp.dot(oa_ref[sl, :], wa_bf[...], preferred_element_type=F32)
        yb = jnp.dot(ob_ref[sl, :], wb_bf[...], preferred_element_type=F32)
        ga = _sigmoid(ga_ref[sl, :].astype(F32) + ba_ref[...])
        gb = _sigmoid(gb_ref[sl, :].astype(F32) + bb_ref[...])
        o_ref[sl, :] = (ga * ya + gb * yb).astype(o_ref.dtype)


def _merge(oa, ob, w_a, w_b, layer, gates, b_merge):
    m = oa.shape[0]
    tm = min(m, 1024)
    tn = 1024
    nj = D_MODEL // tn
    bm = b_merge.reshape(1, 2 * D_MODEL)
    return pl.pallas_call(
        _merge_kernel,
        grid=(nj, pl.cdiv(m, tm)),
        in_specs=[pl.BlockSpec((tm, MOBA_WIDTH), lambda j, i: (i, 0)),
                  pl.BlockSpec((tm, GLA_V_WIDTH), lambda j, i: (i, 0)),
                  pl.BlockSpec((None, MOBA_WIDTH, tn), lambda j, i: (layer, 0, j)),
                  pl.BlockSpec((None, GLA_V_WIDTH, tn), lambda j, i: (layer, 0, j)),
                  pl.BlockSpec((tm, tn), lambda j, i: (i, j)),
                  pl.BlockSpec((tm, tn), lambda j, i: (i, nj + j)),
                  pl.BlockSpec((1, tn), lambda j, i: (0, j)),
                  pl.BlockSpec((1, tn), lambda j, i: (0, nj + j))],
        out_specs=pl.BlockSpec((tm, tn), lambda j, i: (i, j)),
        out_shape=jax.ShapeDtypeStruct((m, D_MODEL), BF16),
        scratch_shapes=[pltpu.VMEM((MOBA_WIDTH, tn), BF16), pltpu.VMEM((GLA_V_WIDTH, tn), BF16)],
        compiler_params=_params("parallel", "arbitrary"),
        name="gated_merge",
    )(oa, ob, w_a, w_b, gates, gates, bm, bm)


def _out_kernel(m_ref, w_ref, x_ref, g_ref, *refs, emit_x):
    w_bf = refs[-1]

    @pl.when(pl.program_id(0) == 0)
    def _cast_weight():
        w_bf[...] = w_ref[...].astype(BF16)

    x_new = x_ref[...] + jnp.dot(m_ref[...], w_bf[...], preferred_element_type=F32)
    r = lax.rsqrt(jnp.mean(x_new * x_new, axis=-1, keepdims=True) + RMS_EPS)
    xn = x_new * r * g_ref[...]
    if emit_x:
        w1_ref, w2_ref, b_ref, x_out, xn_out, la_out = refs[:-1]
        x_out[...] = x_new
        xn_bf = xn.astype(xn_out.dtype)
        xn_out[...] = xn_bf
        la_out[...] = _log_forget_gate(xn_bf, w1_ref, w2_ref, b_ref)
    else:
        refs[0][...] = xn.astype(refs[0].dtype)


def _out_proj(merged, w_out, layer, x, g_next, gate_next):
    m = x.shape[0]
    tm = min(m, 512)
    row = pl.BlockSpec((tm, D_MODEL), lambda i: (i, 0))
    in_specs = [row,
                pl.BlockSpec((None, D_MODEL, D_MODEL), lambda i: (layer, 0, 0), pipeline_mode=pl.Buffered(1)),
                row,
                pl.BlockSpec((1, D_MODEL), lambda i: (0, 0))]
    operands = [merged, w_out, x, g_next.reshape(1, D_MODEL)]
    if gate_next is not None:
        gate_specs, gate_operands = _gate_operands(*gate_next)
        in_specs += gate_specs
        operands += gate_operands
        out_specs = [row, row, pl.BlockSpec((tm, GLA_K_WIDTH), lambda i: (i, 0))]
        out_shape = [jax.ShapeDtypeStruct((m, D_MODEL), F32),
                     jax.ShapeDtypeStruct((m, D_MODEL), BF16),
                     jax.ShapeDtypeStruct((m, GLA_K_WIDTH), F32)]
    else:
        out_specs = [row]
        out_shape = [jax.ShapeDtypeStruct((m, D_MODEL), F32)]
    return pl.pallas_call(
        functools.partial(_out_kernel, emit_x=gate_next is not None),
        grid=(pl.cdiv(m, tm),),
        in_specs=in_specs,
        out_specs=out_specs,
        out_shape=out_shape,
        scratch_shapes=[pltpu.VMEM((D_MODEL, D_MODEL), BF16)],
        compiler_params=_params("arbitrary"),
        name="out_proj",
    )(*operands)


def _top_k_mask(s, idx, n_cand, axis):
    sel = jnp.zeros(s.shape, F32)
    for _ in range(MOBA_TOPK):
        mx = jnp.max(s, axis=axis, keepdims=True)
        cand = jnp.where((s == mx) & (mx > -jnp.inf), idx, float(n_cand))
        first = jnp.min(cand, axis=axis, keepdims=True)
        pick = idx == first
        sel = jnp.where(pick, 1.0, sel)
        s = jnp.where(pick, -jnp.inf, s)
    return sel


def _moba_prompt_kernel(pt_ref, q_ref, z_ref, k_ref, vt_ref, mean_ref, ck_ref, o_ref, pm_ref,
                        sel_scr, acc_scr, lg_scr, page_buf, page_sem, *, nblk, heads, group, layer, pages_per_step):
    j = pl.program_id(2)

    step = (pl.program_id(0) * pl.num_programs(1) + pl.program_id(1)) * nblk + j
    n_steps = pl.num_programs(0) * pl.num_programs(1) * nblk
    pages_per_seq = pt_ref.shape[1]
    pages_per_block = MOBA_BLOCK // page_buf.shape[2]
    n_bufs = page_buf.shape[0]

    def page_copies(s):
        first = s * pages_per_step
        seq = first // pages_per_seq
        pg0 = first % pages_per_seq
        slot = s % n_bufs
        return [pltpu.make_async_copy(ck_ref.at[layer, pt_ref[seq, pg0 + p]], page_buf.at[slot, p],
                                      page_sem.at[slot]) for p in range(pages_per_step)]

    @pl.when(step == 0)
    def _first_pages():
        for ahead in range(n_bufs - 1):
            for cp in page_copies(step + ahead):
                cp.start()

    @pl.when(step + (n_bufs - 1) < n_steps)
    def _next_pages():
        for cp in page_copies(step + (n_bufs - 1)):
            cp.start()

    for cp in page_copies(step):
        cp.wait()
    first_page = step * pages_per_step
    for i in range(pages_per_step // pages_per_block):
        tot = None
        for pg in range(pages_per_block):
            part = jnp.sum(page_buf[step % n_bufs, i * pages_per_block + pg], axis=0)
            tot = part if tot is None else tot + part
        pm_ref[first_page // pages_per_seq, (first_page % pages_per_seq) // pages_per_block + i] = (
            tot * (1.0 / MOBA_BLOCK))

    ones_rows = jnp.ones((MOBA_DENOM_ROWS, MOBA_BLOCK), BF16)
    lanes = [slice(h * HEAD_DIM, (h + 1) * HEAD_DIM) for h in range(heads)]
    qs = [q_ref[:, ln] for ln in lanes]

    def logits(h, n):
        r = pl.multiple_of(n * MOBA_BLOCK, MOBA_BLOCK)
        return lax.dot_general(k_ref[pl.ds(r, MOBA_BLOCK), lanes[h]], qs[h], NT_DIMS, preferred_element_type=F32)

    def weighted_values(h, n, p):
        lhs = jnp.concatenate([vt_ref[n, lanes[h], :], ones_rows], axis=0)
        return jnp.dot(lhs, p, preferred_element_type=F32)

    def trip_blocks(i):
        return [jnp.minimum(group * i + g, nblk - 1) for g in range(group)]

    scores = [lax.dot_general(mean_ref[:, lanes[h]].astype(BF16), qs[h], NT_DIMS, preferred_element_type=F32)
              for h in range(heads)]
    blk = lax.broadcasted_iota(jnp.int32, scores[0].shape, 0)
    for h in range(heads):
        sel_scr[h] = _top_k_mask(jnp.where(blk < j, scores[h], -jnp.inf), blk.astype(F32), nblk, 0)

    own = [logits(h, j) for h in range(heads)]
    for h in range(heads):
        for g, n in enumerate(trip_blocks(0)):
            lg_scr[h * group + g] = logits(h, n)
    key = lax.broadcasted_iota(jnp.int32, own[0].shape, 0)
    qry = lax.broadcasted_iota(jnp.int32, own[0].shape, 1)
    m0, p0 = [], []
    for h in range(heads):
        lg = jnp.where(key <= qry, own[h], NEG_INF)
        m = jnp.max(lg, axis=0, keepdims=True)
        m0.append(m)
        p0.append(jnp.exp2(lg - m).astype(BF16))
    for h in range(heads):
        acc_scr[h] = weighted_values(h, j, p0[h])

    def past_blocks(i, carry):
        ms = carry
        ns = trip_blocks(i)
        ns_next = trip_blocks(i + 1)
        new_m = []
        for h in range(heads):
            lgs = [lg_scr[h * group + g] for g in range(group)]
            for g, n in enumerate(ns_next):
                lg_scr[h * group + g] = logits(h, n)
            chosen = [(sel_scr[h, pl.ds(n, 1), :] > 0.0) & (group * i + g < j) for g, n in enumerate(ns)]
            m_new = ms[h]
            for lg, ch in zip(lgs, chosen):
                m_new = jnp.maximum(m_new, jnp.where(ch, jnp.max(lg, axis=0, keepdims=True), NEG_INF))
            alpha = jnp.exp2(ms[h] - m_new)
            pv = None
            for lg, ch, n in zip(lgs, chosen, ns):
                p = jnp.exp2(lg - jnp.where(ch, m_new, -NEG_INF))
                part = weighted_values(h, n, p.astype(BF16))
                pv = part if pv is None else pv + part
            acc_scr[h] = alpha * acc_scr[h] + pv
            new_m.append(m_new)
        return tuple(new_m)

    lax.fori_loop(0, (j + group - 1) // group, past_blocks, tuple(m0))

    for h in range(heads):
        z = z_ref[:, lanes[h]].astype(F32)
        out_t = acc_scr[h, :HEAD_DIM, :] / acc_scr[h, HEAD_DIM:HEAD_DIM + 1, :]
        o_ref[:, lanes[h]] = (jnp.transpose(out_t) * _silu(z)).astype(o_ref.dtype)


MOBA_HEADS_PER_STEP = 4
PAGE_LOOKAHEAD = 2
MOBA_DENOM_ROWS = 16
MOBA_BLOCKS_PER_TRIP = 2


def _moba_prompt(main, k16, vt16, means, cache_k, page_table, layer, batch, seq):
    nblk = seq // MOBA_BLOCK
    hp = MOBA_HEADS_PER_STEP
    width = hp * HEAD_DIM
    dec_b, n_pages = page_table.shape
    page = cache_k.shape[2]
    pages_per_block = MOBA_BLOCK // page
    n_steps = batch * (MOBA_HEADS // hp) * nblk
    pages_per_step = dec_b * n_pages // n_steps
    assert pages_per_step * n_steps == dec_b * n_pages, "pages must split evenly over the grid steps"
    assert n_pages % pages_per_step == 0 and pages_per_step % pages_per_block == 0
    n_past_blocks = n_pages // pages_per_block
    qmap = lambda b, h, j, pt: (b * nblk + j, COL_QA // width + h)
    zmap = lambda b, h, j, pt: (b * nblk + j, COL_ZA // width + h)
    return pl.pallas_call(
        functools.partial(_moba_prompt_kernel, nblk=nblk, heads=hp, group=MOBA_BLOCKS_PER_TRIP,
                          layer=layer, pages_per_step=pages_per_step),
        grid_spec=pltpu.PrefetchScalarGridSpec(
            num_scalar_prefetch=1,
            grid=(batch, MOBA_HEADS // hp, nblk),
            in_specs=[pl.BlockSpec((MOBA_BLOCK, width), qmap),
                      pl.BlockSpec((MOBA_BLOCK, width), zmap),
                      pl.BlockSpec((seq, width), lambda b, h, j, pt: (b, h)),
                      pl.BlockSpec((nblk, width, MOBA_BLOCK), lambda b, h, j, pt: (b, h, 0)),
                      pl.BlockSpec((None, nblk, width), lambda b, h, j, pt: (b, 0, h)),
                      pl.BlockSpec(memory_space=pl.ANY)],
            out_specs=[pl.BlockSpec((MOBA_BLOCK, width), lambda b, h, j, pt: (b * nblk + j, h)),
                       pl.BlockSpec((dec_b, n_past_blocks, MOBA_HEADS, HEAD_DIM), lambda b, h, j, pt: (0, 0, 0, 0))],
            scratch_shapes=[pltpu.VMEM((hp, nblk, MOBA_BLOCK), F32),
                            pltpu.VMEM((hp, HEAD_DIM + MOBA_DENOM_ROWS, MOBA_BLOCK), F32),
                            pltpu.VMEM((hp * MOBA_BLOCKS_PER_TRIP, MOBA_BLOCK, MOBA_BLOCK), F32),
                            pltpu.VMEM((PAGE_LOOKAHEAD + 1, pages_per_step, page, MOBA_HEADS, HEAD_DIM), F32),
                            pltpu.SemaphoreType.DMA((PAGE_LOOKAHEAD + 1,))],
        ),
        out_shape=[jax.ShapeDtypeStruct((batch * seq, MOBA_WIDTH), BF16),
                   jax.ShapeDtypeStruct((dec_b, n_past_blocks, MOBA_HEADS, HEAD_DIM), F32)],
        compiler_params=_params("arbitrary", "arbitrary", "arbitrary"),
        name="moba_prompt",
    )(page_table, main, main, k16, vt16, means, cache_k)


def _gla_kernel(q_ref, k_ref, v_ref, g_ref, z_ref, s0_ref, gn_ref, o_ref, sout_ref, s_scr,
                *, chunk, sub, n_valid, heads):
    t = pl.program_id(2)

    @pl.when(t == 0)
    def _load_state():
        s_scr[...] = s0_ref[...]

    hs = range(heads)
    dk = [slice(h * GLA_DK, (h + 1) * GLA_DK) for h in hs]
    dv = [slice(h * GLA_DV, (h + 1) * GLA_DV) for h in hs]
    ri = lax.broadcasted_iota(jnp.int32, (chunk, chunk), 0)
    ci = lax.broadcasted_iota(jnp.int32, (chunk, chunk), 1)
    causal = ci <= ri
    tri = jnp.where(causal, 1.0, 0.0).astype(BF16)
    live = lax.broadcasted_iota(jnp.int32, (chunk, GLA_DK), 0) < n_valid
    nsub = chunk // sub

    ks, b3s = [], []
    for h in hs:
        k = k_ref[:, dk[h]].astype(F32)
        g = g_ref[:, dk[h]]
        if n_valid < chunk:
            g = jnp.where(live, g, 0.0)
            k = jnp.where(live, k, 0.0)
        g_hi = g.astype(BF16)
        g_r1 = g - g_hi.astype(F32)
        g_mid = g_r1.astype(BF16)
        g_lo = (g_r1 - g_mid.astype(F32)).astype(BF16)
        ks.append(k)
        b3s.append(jnp.dot(tri, jnp.concatenate([g_hi, g_mid, g_lo], axis=1), preferred_element_type=F32))

    bs, qs, scores = [], [], []
    for h in hs:
        b3 = b3s[h]
        b = (b3[:, :GLA_DK] + b3[:, GLA_DK:2 * GLA_DK]) + b3[:, 2 * GLA_DK:]
        q = q_ref[:, dk[h]].astype(F32) * (GLA_DK ** -0.5)
        ends = [b[(i + 1) * sub - 1:(i + 1) * sub] for i in range(nsub)]
        own_end = jnp.concatenate([jnp.broadcast_to(e, (sub, GLA_DK)) for e in ends], axis=0)
        k_rel = (ks[h] * jnp.exp(own_end - b)).astype(BF16)
        q_parts, k_rows = [], []
        for i in range(nsub):
            lo = i * sub
            q_rel = (q[lo:] * jnp.exp(b[lo:] - ends[i])).astype(BF16)
            q_parts.append(q_rel if lo == 0 else
                           jnp.concatenate([jnp.zeros((lo, GLA_DK), BF16), q_rel], axis=0))
            k_rows.append(jnp.concatenate(
                [k_rel[lo:lo + sub] if c == i else jnp.zeros((sub, GLA_DK), BF16) for c in range(nsub)], axis=1)
                if nsub > 1 else k_rel)
        q_cat = jnp.concatenate(q_parts, axis=1) if nsub > 1 else q_parts[0]
        k_cat = jnp.concatenate(k_rows, axis=0) if nsub > 1 else k_rows[0]
        scores.append(lax.dot_general(q_cat, k_cat, NT_DIMS, preferred_element_type=F32))
        bs.append(b)
        qs.append(q)

    outs, new_states = [], []
    for h in hs:
        b = bs[h]
        b_last = b[chunk - 1:chunk]
        v = v_ref[:, dv[h]]
        state = s_scr[h]
        a = jnp.where(causal, scores[h], 0.0)
        outs.append(jnp.dot(a.astype(BF16), v, preferred_element_type=F32)
                    + jnp.dot((qs[h] * jnp.exp(b)).astype(BF16), state.astype(BF16), preferred_element_type=F32))
        k_dec = (ks[h] * jnp.exp(b_last - b)).astype(BF16)
        update = lax.dot_general(k_dec, v, TN_DIMS, preferred_element_type=F32)
        decay_t = jnp.transpose(jnp.broadcast_to(jnp.exp(b_last), (GLA_DK, GLA_DK)))
        new_states.append(jnp.concatenate([decay_t] * (GLA_DV // GLA_DK), axis=1) * state + update)

    for h in hs:
        s_scr[h] = new_states[h]
        o = outs[h]
        r = lax.rsqrt(jnp.mean(o * o, axis=-1, keepdims=True) + RMS_EPS)
        z = z_ref[:, dv[h]].astype(F32)
        o_ref[:, dv[h]] = (o * r * gn_ref[h] * _silu(z)).astype(o_ref.dtype)

    @pl.when(t == pl.num_programs(2) - 1)
    def _store_state():
        for h in hs:
            sout_ref[h] = new_states[h]


GLA_HEADS_PER_STEP = 4


def _gla(main, log_a, s0, gn, batch, rows_per_seq, chunk, sub, n_valid):
    nt = rows_per_seq // chunk
    m = main.shape[0]
    hp = GLA_HEADS_PER_STEP
    wk, wv = hp * GLA_DK, hp * GLA_DV
    rmap = lambda off: (lambda b, h, t: (b * nt + t, off + h))
    smap = lambda b, h, t: (b, h, 0, 0)
    return pl.pallas_call(
        functools.partial(_gla_kernel, chunk=chunk, sub=sub, n_valid=n_valid, heads=hp),
        grid=(batch, GLA_HEADS // hp, nt),
        in_specs=[pl.BlockSpec((chunk, wk), rmap(COL_QB // wk)),
                  pl.BlockSpec((chunk, wk), rmap(COL_KB // wk)),
                  pl.BlockSpec((chunk, wv), rmap(COL_VB // wv)),
                  pl.BlockSpec((chunk, wk), rmap(0)),
                  pl.BlockSpec((chunk, wv), rmap(COL_ZB // wv)),
                  pl.BlockSpec((None, hp, GLA_DK, GLA_DV), smap),
                  pl.BlockSpec((hp, 1, GLA_DV), lambda b, h, t: (h, 0, 0))],
        out_specs=[pl.BlockSpec((chunk, wv), rmap(0)),
                   pl.BlockSpec((None, hp, GLA_DK, GLA_DV), smap)],
        out_shape=[jax.ShapeDtypeStruct((m, GLA_V_WIDTH), BF16),
                   jax.ShapeDtypeStruct((batch, GLA_HEADS, GLA_DK, GLA_DV), F32)],
        scratch_shapes=[pltpu.VMEM((hp, GLA_DK, GLA_DV), F32)],
        compiler_params=_params("parallel", "parallel", "arbitrary"),
        name="gla",
    )(main, main, main, log_a, main, s0, gn.reshape(GLA_HEADS, 1, GLA_DV))


def _sample_select_kernel(q_ref, mean_ref, o_ref, *, n_blocks):
    q = q_ref[...]
    means = mean_ref[...].astype(BF16)
    rows = q.shape[0]
    ncol = n_blocks * MOBA_HEADS
    col = lax.broadcasted_iota(jnp.int32, (rows, ncol), 1)
    col_head = col % MOBA_HEADS
    col_f = col.astype(F32)
    s_all = jnp.zeros((rows, ncol), F32)
    for h in range(MOBA_HEADS):
        s_h = lax.dot_general(q[:, h * HEAD_DIM:(h + 1) * HEAD_DIM], means, NT_DIMS, preferred_element_type=F32)
        s_all = jnp.where(col_head == h, s_h, s_all)
    lane = lax.broadcasted_iota(jnp.int32, o_ref.shape, 1)
    out = jnp.zeros(o_ref.shape, jnp.int32)
    for h in range(MOBA_HEADS):
        s = jnp.where(col_head == h, s_all, -jnp.inf)
        for t in range(MOBA_TOPK):
            mx = jnp.max(s, axis=1, keepdims=True)
            cand = jnp.where((s == mx) & (mx > -jnp.inf), col_f, float(ncol))
            first = jnp.min(cand, axis=1, keepdims=True)
            s = jnp.where(col_f == first, -jnp.inf, s)
            blk = first.astype(jnp.int32) // MOBA_HEADS
            out = jnp.where(lane == h * MOBA_TOPK + t, blk, out)
    o_ref[...] = out


def _sample_select(main, means, dec_b):
    n_blocks = means.shape[1]
    means2 = means.reshape(dec_b, n_blocks * MOBA_HEADS, HEAD_DIM)
    return pl.pallas_call(
        functools.partial(_sample_select_kernel, n_blocks=n_blocks),
        grid=(dec_b,),
        in_specs=[pl.BlockSpec((SAMPLE_ROWS, MOBA_WIDTH), lambda b: (b, COL_QA // MOBA_WIDTH)),
                  pl.BlockSpec((None, n_blocks * MOBA_HEADS, HEAD_DIM), lambda b: (b, 0, 0))],
        out_specs=pl.BlockSpec((None, SAMPLE_ROWS, LANES), lambda b: (b, 0, 0)),
        out_shape=jax.ShapeDtypeStruct((dec_b, SAMPLE_ROWS, LANES), jnp.int32),
        compiler_params=_params("parallel"),
        name="moba_sample_select",
    )(main, means2)


def _moba_sample_kernel(pt_ref, sel_ref, q_ref, z_ref, ko_ref, vo_ref, ck_ref, cv_ref, o_ref,
                        kbuf, vbuf, sem, *, layer, n_new, page):
    n_heads = pl.num_programs(1)
    step = pl.program_id(0) * n_heads + pl.program_id(1)
    n_steps = pl.num_programs(0) * n_heads
    pages_per_block = MOBA_BLOCK // page
    n_slots = n_new * MOBA_TOPK * pages_per_block

    def page_copies(s):
        b, h = s // n_heads, s % n_heads
        buf = s % 2
        copies = []
        for slot in range(n_slots):
            qi, rest = divmod(slot, MOBA_TOPK * pages_per_block)
            rank, pg = divmod(rest, pages_per_block)
            blk = sel_ref[b, qi * (MOBA_HEADS * MOBA_TOPK) + h * MOBA_TOPK + rank]
            pid = pt_ref[b, blk * pages_per_block + pg]
            dst = pl.ds(slot * page, page)
            copies.append(pltpu.make_async_copy(ck_ref.at[layer, pid, :, h, :], kbuf.at[buf, dst, :],
                                                sem.at[buf, 0]))
            copies.append(pltpu.make_async_copy(cv_ref.at[layer, pid, :, h, :], vbuf.at[buf, dst, :],
                                                sem.at[buf, 1]))
        return copies

    @pl.when(step == 0)
    def _first_gather():
        for cp in page_copies(step):
            cp.start()

    @pl.when(step + 1 < n_steps)
    def _next_gather():
        for cp in page_copies(step + 1):
            cp.start()

    for cp in page_copies(step):
        cp.wait()

    scale = HEAD_DIM ** -0.5
    q = q_ref[...]
    rows = q.shape[0]
    n_keys = n_slots * page
    k_sel = kbuf[step % 2]
    v_sel = vbuf[step % 2]
    lg = lax.dot_general(q, k_sel.astype(BF16), NT_DIMS, preferred_element_type=F32) * scale
    row = lax.broadcasted_iota(jnp.int32, (rows, n_keys), 0)
    key_owner = lax.broadcasted_iota(jnp.int32, (rows, n_keys), 1) // (MOBA_TOPK * MOBA_BLOCK)
    lg = jnp.where(key_owner == row, lg, NEG_INF)
    lg_own = lax.dot_general(q, ko_ref[...], NT_DIMS, preferred_element_type=F32) * scale
    r2 = lax.broadcasted_iota(jnp.int32, lg_own.shape, 0)
    c2 = lax.broadcasted_iota(jnp.int32, lg_own.shape, 1)
    lg_own = jnp.where((c2 <= r2) & (c2 < n_new), lg_own, NEG_INF)
    m = jnp.maximum(jnp.max(lg, axis=1, keepdims=True), jnp.max(lg_own, axis=1, keepdims=True))
    p = jnp.exp(lg - m)
    p_own = jnp.exp(lg_own - m)
    denom = jnp.sum(p, axis=1, keepdims=True) + jnp.sum(p_own, axis=1, keepdims=True)
    acc = (jnp.dot(p.astype(BF16), v_sel.astype(BF16), preferred_element_type=F32)
           + jnp.dot(p_own.astype(BF16), vo_ref[...], preferred_element_type=F32))
    z = z_ref[...].astype(F32)
    o_ref[...] = (acc / denom * _silu(z)).astype(o_ref.dtype)


def _moba_sample(main, k16, v16, cache_k, cache_v, page_table, sel, layer, n_new):
    dec_b = page_table.shape[0]
    page = cache_k.shape[2]
    n_keys = n_new * MOBA_TOPK * MOBA_BLOCK
    tile = lambda off: pl.BlockSpec((SAMPLE_ROWS, HEAD_DIM), lambda b, h, pt, sl: (b, off + h))
    return pl.pallas_call(
        functools.partial(_moba_sample_kernel, layer=layer, n_new=n_new, page=page),
        grid_spec=pltpu.PrefetchScalarGridSpec(
            num_scalar_prefetch=2,
            grid=(dec_b, MOBA_HEADS),
            in_specs=[tile(COL_QA // HEAD_DIM), tile(COL_ZA // HEAD_DIM), tile(0), tile(0),
                      pl.BlockSpec(memory_space=pl.ANY), pl.BlockSpec(memory_space=pl.ANY)],
            out_specs=tile(0),
            scratch_shapes=[pltpu.VMEM((2, n_keys, HEAD_DIM), F32),
                            pltpu.VMEM((2, n_keys, HEAD_DIM), F32),
                            pltpu.SemaphoreType.DMA((2, 2))],
        ),
        out_shape=jax.ShapeDtypeStruct((dec_b * SAMPLE_ROWS, MOBA_WIDTH), BF16),
        compiler_params=_params("arbitrary", "arbitrary"),
        name="moba_sample",
    )(page_table, sel, main, main, k16, v16, cache_k, cache_v)


def _main_col_block(j):
    return jnp.where(j == 0, W_QA // PROJ_TN, j + (W_ZA // PROJ_TN - 1))


def _token_proj(xn, w_t, layer, prompt, stack_depth=1, k_stacked=None, v_stacked=None):
    k_outs = ((F32, False), (BF16, False)) + (((F32, "block_mean"),) if prompt else ())
    k32, k16, *k_means = _proj(xn, w_t, layer, lambda j: W_KA // PROJ_TN, 1, k_outs, "proj_k",
                               stack_depth, k_stacked)
    v32, v16 = _proj(xn, w_t, layer, lambda j: W_VA // PROJ_TN, 1, ((F32, False), (BF16, prompt)), "proj_v",
                     stack_depth, v_stacked)
    q_scale = (HEAD_DIM ** -0.5) * LOG2E if prompt else None
    (main,) = _proj(xn, w_t, layer, _main_col_block, MAIN_WIDTH // PROJ_TN, ((BF16, False),), "proj_main",
                    first_tile_scale=q_scale)
    gates = _gates(xn, w_t, layer)
    return k32, k16, v32, v16, main, gates, k_means


def kernel(x_prompt, x_sample, cache_k, cache_v, state_gla, page_table, norm_g, w_in, w_gate2, b_gate,
           gla_norm_g, w_branch_a, w_branch_b, b_merge, w_out, final_norm_g):
    depth = w_in.shape[0]
    bp, seq, _ = x_prompt.shape
    dec_b, n_new, _ = x_sample.shape
    assert n_new <= SAMPLE_ROWS and seq % MOBA_BLOCK == 0
    assert cache_k.shape[2] * page_table.shape[1] % MOBA_BLOCK == 0
    assert (W_ZA // PROJ_TN, W_ZB // PROJ_TN) == (3, 6) and W_QB % PROJ_TN == 0

    xp = x_prompt.reshape(bp * seq, D_MODEL)
    xs = jnp.pad(x_sample, ((0, 0), (0, SAMPLE_ROWS - n_new), (0, 0))).reshape(dec_b * SAMPLE_ROWS, D_MODEL)
    zero_state = jnp.zeros((bp, GLA_HEADS, GLA_DK, GLA_DV), F32)
    w_t = jnp.swapaxes(w_in, 1, 2)
    w_g2 = jnp.pad(w_gate2.astype(BF16), ((0, 0), (0, LANES - GLA_GATE_RANK), (0, 0)))
    gate = lambda l: (w_t, l, w_g2[l], b_gate[l])
    xnp, log_a_p = _rmsnorm(xp, norm_g[0], gate(0))
    xns, log_a_s = _rmsnorm(xs, norm_g[0], gate(0))

    sp_l, ks_l, vs_l, ss_l = [], [], [], []
    kp_all = vp_all = None
    for l in range(depth):
        last = l == depth - 1
        g_next = final_norm_g if last else norm_g[l + 1]
        gate_next = None if last else gate(l + 1)

        kp_all, k16, vp_all, vt16, main, gates, (k_means,) = _token_proj(
            xnp, w_t, l, True, depth, kp_all, vp_all)
        means = k_means.reshape(bp, seq // MOBA_BLOCK, MOBA_WIDTH)
        oa, pmeans = _moba_prompt(main, k16, vt16, means, cache_k, page_table, l, bp, seq)
        ob, sp = _gla(main, log_a_p, zero_state, gla_norm_g[l], bp, seq, MOBA_BLOCK, GLA_SUB, MOBA_BLOCK)
        merged = _merge(oa, ob, w_branch_a, w_branch_b, l, gates, b_merge[l])
        outs = _out_proj(merged, w_out, l, xp, g_next, gate_next)
        if last:
            (yp,) = outs
        else:
            xp, xnp, log_a_p = outs
        sp_l.append(sp)

        k32, k16, v32, v16, main, gates, _ = _token_proj(xns, w_t, l, False)
        sel = _sample_select(main, pmeans, dec_b)
        sel = sel[:, :n_new, :MOBA_HEADS * MOBA_TOPK].reshape(dec_b, n_new * MOBA_HEADS * MOBA_TOPK)
        oa = _moba_sample(main, k16, v16, cache_k, cache_v, page_table, sel, l, n_new)
        ob, ss = _gla(main, log_a_s, state_gla[l], gla_norm_g[l], dec_b, SAMPLE_ROWS, SAMPLE_ROWS,
                      SAMPLE_ROWS, n_new)
        merged = _merge(oa, ob, w_branch_a, w_branch_b, l, gates, b_merge[l])
        outs = _out_proj(merged, w_out, l, xs, g_next, gate_next)
        if last:
            (ys,) = outs
        else:
            xs, xns, log_a_s = outs
        rows = lambda a: a.reshape(dec_b, SAMPLE_ROWS, MOBA_HEADS, HEAD_DIM)[:, :n_new]
        ks_l.append(rows(k32))
        vs_l.append(rows(v32))
        ss_l.append(ss)

    y_prompt = yp.reshape(bp, seq, D_MODEL)
    y_sample = ys.reshape(dec_b, SAMPLE_ROWS, D_MODEL)[:, :n_new]
    k_prompt = kp_all.reshape(depth, bp, seq, MOBA_HEADS, HEAD_DIM)
    v_prompt = vp_all.reshape(depth, bp, seq, MOBA_HEADS, HEAD_DIM)
    return (y_prompt, y_sample, k_prompt, v_prompt, jnp.stack(sp_l),
            jnp.stack(ks_l), jnp.stack(vs_l), jnp.stack(ss_l))
```
